```python
import math
import jax, jax.numpy as jnp
from jax import lax
import numpy as np

D_MODEL = 1024
BATCH = 2
SEQ = 16384
DEPTH = 2
DEC_BATCH = 1
DEC_SEQ = 16384
PAST_LEN = 128

N_MEM = 256
ATT_HEADS = 8
ATT_KV_HEADS = 2
ATT_HEAD_DIM = 64
ATT_WIDTH = ATT_HEADS * ATT_HEAD_DIM
ATT_KV_WIDTH = ATT_KV_HEADS * ATT_HEAD_DIM
WINDOW = 128
BLOCK = 128
N_BUCKETS = 32
MAX_DISTANCE = 128
HYENA_WIDTH = 512
HYENA_ORDER = 2
SHORT_CONV = 3
FILTER_EMB = 33
FILTER_BANDS = (FILTER_EMB - 1) // 2
FILTER_HIDDEN = 64
FILTER_INNER = 2
FAST_DECAY_PCT = 0.3
SLOW_DECAY_PCT = 1.5
DECAY_TARGET = 1e-2
MEM_HEADS = 4
MEM_HEAD_DIM = 128
MEM_WIDTH = MEM_HEADS * MEM_HEAD_DIM
N_BRANCH = 3
BRANCH_WIDTH = 512
PEER_HEADS = 8
N_KEYS = 128
N_EXPERTS = N_KEYS * N_KEYS
PEER_TOPK = 16
PEER_HALF = 128
PEER_QDIM = 2 * PEER_HALF
PEER_BLOCK = 128
O_K = ATT_WIDTH
O_V = O_K + ATT_KV_WIDTH
O_HY = O_V + ATT_KV_WIDTH
O_MQ = O_HY + (HYENA_ORDER + 1) * HYENA_WIDTH
O_GATE = O_MQ + MEM_WIDTH
IN_WIDTH = O_GATE + N_BRANCH * D_MODEL
ALPHA = (2 * DEPTH) ** 0.25
BETA = (8 * DEPTH) ** -0.25
LN_EPS = 1e-5
NEG_INF = -1e30

kernel_name = 'hybrid_bidir_encoder'


def layer_norm(x, g, b):
    xf = x.astype(jnp.float32)
    mu = jnp.mean(xf, axis=-1, keepdims=True)
    var = jnp.mean(jnp.square(xf - mu), axis=-1, keepdims=True)
    y = (xf - mu) * lax.rsqrt(var + LN_EPS) * g.astype(jnp.float32) + b.astype(jnp.float32)
    return y.astype(x.dtype)


def t5_bucket(rel):
    nb = N_BUCKETS // 2
    max_exact = nb // 2
    ret = jnp.where(rel > 0, nb, 0)
    n = jnp.abs(rel)
    nf = jnp.maximum(n, 1).astype(jnp.float32)
    large = max_exact + (jnp.log(nf / max_exact) / math.log(MAX_DISTANCE / max_exact)
                         * (nb - max_exact)).astype(jnp.int32)
    large = jnp.minimum(large, nb - 1)
    return ret + jnp.where(n < max_exact, n, large)


def window_attention(q, k, v, rel_bias, sink):
    B, S = q.shape[0], q.shape[1]
    nb = S // BLOCK
    G = ATT_HEADS // ATT_KV_HEADS
    qb = q.reshape(B, nb, BLOCK, ATT_KV_HEADS, G, ATT_HEAD_DIM)
    pad = ((0, 0), (BLOCK, BLOCK), (0, 0), (0, 0))
    kp = jnp.pad(k, pad).reshape(B, nb + 2, BLOCK, ATT_KV_HEADS, ATT_HEAD_DIM)
    vp = jnp.pad(v, pad).reshape(B, nb + 2, BLOCK, ATT_KV_HEADS, ATT_HEAD_DIM)
    kb = jnp.concatenate([kp[:, :-2], kp[:, 1:-1], kp[:, 2:]], axis=2)
    vb = jnp.concatenate([vp[:, :-2], vp[:, 1:-1], vp[:, 2:]], axis=2)
    qpos = jnp.arange(BLOCK)
    kpos = jnp.arange(3 * BLOCK) - BLOCK
    rel = kpos[None, :] - qpos[:, None]
    bias = rel_bias[t5_bucket(rel)].astype(jnp.float32)
    bias = bias.transpose(2, 0, 1).reshape(ATT_KV_HEADS, G, BLOCK, 3 * BLOCK)
    abs_k = (jnp.arange(nb) * BLOCK)[:, None] + kpos[None, :]
    valid = (abs_k >= 0) & (abs_k < S)
    mask = valid[:, None, :] & (jnp.abs(rel) <= WINDOW)[None]
    scale = ATT_HEAD_DIM ** -0.5
    s = jnp.einsum('bnqhgd,bnkhd->bnhgqk', qb, kb).astype(jnp.float32) * scale + bias
    s = jnp.where(mask[None, :, None, None], s, NEG_INF)
    sink_l = sink.astype(jnp.float32).reshape(ATT_KV_HEADS, G)[None, None, :, :, None, None]
    m = jnp.maximum(jnp.max(s, axis=-1, keepdims=True), sink_l)
    p = jnp.exp(s - m)
    denom = jnp.sum(p, axis=-1, keepdims=True) + jnp.exp(sink_l - m)
    o = jnp.einsum('bnhgqk,bnkhd->bnqhgd', (p / denom).astype(vb.dtype), vb)
    return o.reshape(B, S, ATT_WIDTH)


def short_conv(u, w, b):
    L = u.shape[1]
    r = SHORT_CONV // 2
    up = jnp.pad(u, ((0, 0), (r, r), (0, 0)))
    out = b
    for j in range(SHORT_CONV):
        out = out + up[:, j:j + L] * w[j]
    return out


def hyena_filters(L, f_w_in, f_w_mid, f_b, f_freq, f_w_out):
    f32 = jnp.float32
    pos = jnp.arange(L, dtype=f32)
    t = pos / max(L - 1, 1)
    w = 2.0 * math.pi * pos / L
    bands = jnp.linspace(1e-4, FILTER_BANDS - 1, FILTER_BANDS, dtype=f32)
    ang = w[:, None] * bands[None, :]
    z = jnp.concatenate([t[:, None], jnp.cos(ang), -jnp.sin(ang)], axis=-1)
    fb = f_b.astype(f32)
    ff = f_freq.astype(f32)
    h = jnp.sin(ff[0] * (z @ f_w_in.astype(f32) + fb[0]))
    for i in range(FILTER_INNER):
        h = jnp.sin(ff[i + 1] * (h @ f_w_mid[i].astype(f32) + fb[i + 1]))
    h = (h @ f_w_out.astype(f32)).reshape(L, HYENA_ORDER, 2, HYENA_WIDTH)
    max_decay = math.log(DECAY_TARGET) / FAST_DECAY_PCT
    min_decay = math.log(DECAY_TARGET) / SLOW_DECAY_PCT
    deltas = jnp.linspace(min_decay, max_decay, HYENA_WIDTH, dtype=f32)
    h = h * jnp.exp(-t[:, None, None, None] * jnp.abs(deltas))
    fwd, bwd = h[:, :, 0], h[:, :, 1]
    two = jnp.concatenate([fwd[:1] + bwd[:1], fwd[1:], jnp.zeros_like(fwd[:1]), bwd[:0:-1]], axis=0)
    two = two / jnp.sum(jnp.abs(two), axis=0, keepdims=True)
    return jnp.fft.rfft(two, axis=0)


def fft_long_conv(z, filt_f, bias):
    L = z.shape[1]
    zf = jnp.fft.rfft(z.astype(jnp.float32), n=2 * L, axis=1)
    y = jnp.fft.irfft(zf * filt_f[None], n=2 * L, axis=1)[:, :L]
    return (y + z.astype(jnp.float32) * bias.astype(jnp.float32)).astype(z.dtype)


def hyena_mixer(u, conv_w, conv_b, f_w_in, f_w_mid, f_b, f_freq, f_w_out, hyena_bias):
    L = u.shape[1]
    uc = short_conv(u, conv_w, conv_b)
    parts = jnp.split(uc, HYENA_ORDER + 1, axis=-1)
    filt = hyena_filters(L, f_w_in, f_w_mid, f_b, f_freq, f_w_out)
    z = parts[0]
    for o in range(HYENA_ORDER):
        z = parts[o + 1] * fft_long_conv(z, filt[:, o], hyena_bias[o])
    return z


def memory_attention(q, mem, w_mem_kv):
    B, S = q.shape[0], q.shape[1]
    M = mem.shape[1]
    kv = mem @ w_mem_kv
    k = kv[..., :MEM_WIDTH].reshape(B, M, MEM_HEADS, MEM_HEAD_DIM)
    v = kv[..., MEM_WIDTH:].reshape(B, M, MEM_HEADS, MEM_HEAD_DIM)
    qh = q.reshape(B, S, MEM_HEADS, MEM_HEAD_DIM)
    s = jnp.einsum('bshd,bmhd->bhsm', qh, k).astype(jnp.float32) * (MEM_HEAD_DIM ** -0.5)
    p = jax.nn.softmax(s, axis=-1).astype(v.dtype)
    o = jnp.einsum('bhsm,bmhd->bshd', p, v)
    return o.reshape(B, S, MEM_WIDTH)


def peer_ffn(x, w_query, sub_keys, expert_u, expert_v):
    B, S, D = x.shape
    T = B * S
    xb = x.reshape(T // PEER_BLOCK, PEER_BLOCK, D)

    def block(xt):
        q = (xt @ w_query).reshape(PEER_BLOCK, PEER_HEADS, 2, PEER_HALF)
        s = jnp.einsum('thcd,hckd->thck', q, sub_keys).astype(jnp.float32)
        s_top, i_top = lax.top_k(s, PEER_TOPK)
        cand = (s_top[:, :, 0, :, None] + s_top[:, :, 1, None, :]).reshape(PEER_BLOCK, PEER_HEADS, PEER_TOPK * PEER_TOPK)
        cand_idx = (i_top[:, :, 0, :, None] * N_KEYS + i_top[:, :, 1, None, :]).reshape(PEER_BLOCK, PEER_HEADS, PEER_TOPK * PEER_TOPK)
        sc, pos = lax.top_k(cand, PEER_TOPK)
        eidx = jnp.take_along_axis(cand_idx, pos, axis=-1)
        g = jax.nn.softmax(sc, axis=-1)
        u = expert_u[eidx]
        act = jax.nn.gelu(jnp.einsum('thkd,td->thk', u, xt).astype(jnp.float32), approximate=False)
        coef = (g * act).astype(xt.dtype)
        return jnp.einsum('thk,thkd->td', coef, expert_v[eidx])

    return lax.map(block, xb).reshape(B, S, D)


def encoder_layer(x, mem, rel_bias, w_in, b_in, conv_w, conv_b, attn_sink, f_w_in, f_w_mid, f_b, f_freq,
                  f_w_out, hyena_bias, w_mem_kv, w_branch, w_out, ln1_g, ln1_b, w_query, sub_keys,
                  expert_u, expert_v, ln2_g, ln2_b):
    B, S, _ = x.shape
    proj = x @ w_in + b_in
    q_a = proj[..., :O_K].reshape(B, S, ATT_HEADS, ATT_HEAD_DIM)
    k_a = proj[..., O_K:O_V].reshape(B, S, ATT_KV_HEADS, ATT_HEAD_DIM)
    v_a = proj[..., O_V:O_HY].reshape(B, S, ATT_KV_HEADS, ATT_HEAD_DIM)
    hy = proj[..., O_HY:O_MQ]
    q_m = proj[..., O_MQ:O_GATE]
    gates = jax.nn.sigmoid(proj[..., O_GATE:].reshape(B, S, N_BRANCH, D_MODEL))
    branches = (
        window_attention(q_a, k_a, v_a, rel_bias, attn_sink),
        hyena_mixer(hy, conv_w, conv_b, f_w_in, f_w_mid, f_b, f_freq, f_w_out, hyena_bias),
        memory_attention(q_m, mem, w_mem_kv),
    )
    merged = gates[:, :, 0] * (branches[0] @ w_branch[0])
    for n in range(1, N_BRANCH):
        merged = merged + gates[:, :, n] * (branches[n] @ w_branch[n])
    x = layer_norm(ALPHA * x + merged @ w_out, ln1_g, ln1_b)
    x = layer_norm(ALPHA * x + peer_ffn(x, w_query, sub_keys, expert_u, expert_v), ln2_g, ln2_b)
    return x


def setup_inputs(seed: int = 0) -> dict:
    key = jax.random.key(seed)
    ks = jax.random.split(key, 32)
    nrm = jax.random.normal
    f32 = jnp.float32
    D = D_MODEL
    HYW = HYENA_WIDTH
    return {
        'x_prompt': nrm(ks[0], (BATCH, SEQ, D), f32),
        'x_sample': nrm(ks[1], (DEC_BATCH, DEC_SEQ, D), f32),
        'mem_prompt': nrm(ks[2], (BATCH, N_MEM, D), f32),
        'mem_sample': nrm(ks[3], (DEC_BATCH, N_MEM, D), f32),
        'rel_bias': 0.5 * nrm(ks[4], (N_BUCKETS, ATT_HEADS), f32),
        'w_in': nrm(ks[5], (DEPTH, D, IN_WIDTH), f32) * D ** -0.5,
        'b_in': 0.02 * nrm(ks[6], (DEPTH, IN_WIDTH), f32),
        'conv_w': nrm(ks[7], (DEPTH, SHORT_CONV, (HYENA_ORDER + 1) * HYW), f32) * SHORT_CONV ** -0.5,
        'conv_b': 0.02 * nrm(ks[8], (DEPTH, (HYENA_ORDER + 1) * HYW), f32),
        'attn_sink': 0.5 * nrm(ks[9], (DEPTH, ATT_HEADS), f32),
        'f_w_in': nrm(ks[10], (DEPTH, FILTER_EMB, FILTER_HIDDEN), f32) * FILTER_EMB ** -0.5,
        'f_w_mid': nrm(ks[11], (DEPTH, FILTER_INNER, FILTER_HIDDEN, FILTER_HIDDEN), f32) * FILTER_HIDDEN ** -0.5,
        'f_b': 0.1 * nrm(ks[12], (DEPTH, FILTER_INNER + 1, FILTER_HIDDEN), f32),
        'f_freq': 1.0 + 0.05 * nrm(ks[13], (DEPTH, FILTER_INNER + 1, FILTER_HIDDEN), f32),
        'f_w_out': nrm(ks[14], (DEPTH, FILTER_HIDDEN, HYENA_ORDER * 2 * HYW), f32) * FILTER_HIDDEN ** -0.5,
        'hyena_bias': nrm(ks[15], (DEPTH, HYENA_ORDER, HYW), f32),
        'w_mem_kv': nrm(ks[16], (DEPTH, D, 2 * MEM_WIDTH), f32) * D ** -0.5,
        'w_branch': nrm(ks[17], (DEPTH, N_BRANCH, BRANCH_WIDTH, D), f32) * (BETA * BRANCH_WIDTH ** -0.5),
        'w_out': nrm(ks[18], (DEPTH, D, D), f32) * (BETA * D ** -0.5),
        'ln1_g': 1.0 + 0.02 * nrm(ks[19], (DEPTH, D), f32),
        'ln1_b': 0.02 * nrm(ks[20], (DEPTH, D), f32),
        'w_query': nrm(ks[21], (DEPTH, D, PEER_HEADS * PEER_QDIM), f32) * D ** -0.5,
        'sub_keys': nrm(ks[22], (DEPTH, PEER_HEADS, 2, N_KEYS, PEER_HALF), f32) * PEER_HALF ** -0.5,
        'expert_u': nrm(ks[23], (DEPTH, N_EXPERTS, D), f32) * D ** -0.5,
        'expert_v': nrm(ks[24], (DEPTH, N_EXPERTS, D), f32) * (BETA * PEER_HEADS ** -0.5),
        'ln2_g': 1.0 + 0.02 * nrm(ks[25], (DEPTH, D), f32),
        'ln2_b': 0.02 * nrm(ks[26], (DEPTH, D), f32),
    }


def reference(x_prompt, x_sample, mem_prompt, mem_sample, rel_bias, w_in, b_in, conv_w, conv_b, attn_sink,
              f_w_in, f_w_mid, f_b, f_freq, f_w_out, hyena_bias, w_mem_kv, w_branch, w_out, ln1_g, ln1_b,
              w_query, sub_keys, expert_u, expert_v, ln2_g, ln2_b):
    def trunk(x, mem):
        for l in range(DEPTH):
            x = encoder_layer(x, mem, rel_bias, w_in[l], b_in[l], conv_w[l], conv_b[l], attn_sink[l],
                              f_w_in[l], f_w_mid[l], f_b[l], f_freq[l], f_w_out[l], hyena_bias[l],
                              w_mem_kv[l], w_branch[l], w_out[l], ln1_g[l], ln1_b[l], w_query[l],
                              sub_keys[l], expert_u[l], expert_v[l], ln2_g[l], ln2_b[l])
        return x

    y_prompt = trunk(x_prompt, mem_prompt)
    y_sample = trunk(x_sample, mem_sample)
    return (y_prompt, y_sample)
```

```python
import functools
import math

import jax
import jax.numpy as jnp
from jax import lax
from jax.experimental import pallas as pl
from jax.experimental.pallas import tpu as pltpu

D_MODEL = 1024
DEPTH = 2
N_MEM = 256
ATT_HEADS = 8
ATT_KV_HEADS = 2
ATT_HEAD_DIM = 64
ATT_WIDTH = ATT_HEADS * ATT_HEAD_DIM
ATT_KV_WIDTH = ATT_KV_HEADS * ATT_HEAD_DIM
WINDOW = 128
BLOCK = 128
N_BUCKETS = 32
MAX_DISTANCE = 128
HYENA_WIDTH = 512
HYENA_ORDER = 2
SHORT_CONV = 3
FILTER_EMB = 33
FILTER_BANDS = (FILTER_EMB - 1) // 2
FILTER_HIDDEN = 64
FILTER_INNER = 2
FAST_DECAY_PCT = 0.3
SLOW_DECAY_PCT = 1.5
DECAY_TARGET = 1e-2
MEM_HEADS = 4
MEM_HEAD_DIM = 128
MEM_WIDTH = MEM_HEADS * MEM_HEAD_DIM
N_BRANCH = 3
BRANCH_WIDTH = 512
PEER_HEADS = 8
N_KEYS = 128
N_EXPERTS = N_KEYS * N_KEYS
PEER_TOPK = 16
PEER_HALF = 128
PEER_QDIM = 2 * PEER_HALF
PEER_BLOCK = 128
O_K = ATT_WIDTH
O_V = O_K + ATT_KV_WIDTH
O_HY = O_V + ATT_KV_WIDTH
O_MQ = O_HY + (HYENA_ORDER + 1) * HYENA_WIDTH
O_GATE = O_MQ + MEM_WIDTH
IN_WIDTH = O_GATE + N_BRANCH * D_MODEL
ALPHA = (2 * DEPTH) ** 0.25
BETA = (8 * DEPTH) ** -0.25
LN_EPS = 1e-5
NEG_INF = -1e30

VMEM_LIMIT_BYTES = 56 * 1024 * 1024


def _linear_kernel(x_ref, w_ref, b_ref, o_ref):
    x = x_ref[...].astype(jnp.bfloat16)
    acc = jnp.dot(x, w_ref[...], preferred_element_type=jnp.float32)
    o_ref[...] = acc + b_ref[...]


def linear(x, w, b, *, tm=512, tn=None):
    T, K = x.shape
    N = w.shape[1]
    tn = N if tn is None else tn
    wb = w.astype(jnp.bfloat16)
    return pl.pallas_call(
        _linear_kernel,
        grid=(N // tn, T // tm),
        in_specs=[
            pl.BlockSpec((tm, K), lambda j, i: (i, 0)),
            pl.BlockSpec((K, tn), lambda j, i: (0, j)),
            pl.BlockSpec((1, tn), lambda j, i: (0, j)),
        ],
        out_specs=pl.BlockSpec((tm, tn), lambda j, i: (i, j)),
        out_shape=jax.ShapeDtypeStruct((T, N), jnp.float32),
        compiler_params=pltpu.CompilerParams(
            dimension_semantics=("arbitrary", "arbitrary"),
            vmem_limit_bytes=VMEM_LIMIT_BYTES),
        name="linear",
    )(x, wb, b.reshape(1, N))


def layer_norm(x, g, b):
    mu = jnp.mean(x, axis=-1, keepdims=True)
    var = jnp.mean(jnp.square(x - mu), axis=-1, keepdims=True)
    return (x - mu) * lax.rsqrt(var + LN_EPS) * g + b


def t5_bucket(rel):
    nb = N_BUCKETS // 2
    max_exact = nb // 2
    ret = jnp.where(rel > 0, nb, 0)
    n = jnp.abs(rel)
    nf = jnp.maximum(n, 1).astype(jnp.float32)
    large = max_exact + (jnp.log(nf / max_exact) / math.log(MAX_DISTANCE / max_exact)
                         * (nb - max_exact)).astype(jnp.int32)
    large = jnp.minimum(large, nb - 1)
    return ret + jnp.where(n < max_exact, n, large)


def window_attention(q, k, v, rel_bias, sink):
    B, S = q.shape[0], q.shape[1]
    nb = S // BLOCK
    G = ATT_HEADS // ATT_KV_HEADS
    qb = q.reshape(B, nb, BLOCK, ATT_KV_HEADS, G, ATT_HEAD_DIM)
    pad = ((0, 0), (BLOCK, BLOCK), (0, 0), (0, 0))
    kp = jnp.pad(k, pad).reshape(B, nb + 2, BLOCK, ATT_KV_HEADS, ATT_HEAD_DIM)
    vp = jnp.pad(v, pad).reshape(B, nb + 2, BLOCK, ATT_KV_HEADS, ATT_HEAD_DIM)
    kb = jnp.concatenate([kp[:, :-2], kp[:, 1:-1], kp[:, 2:]], axis=2)
    vb = jnp.concatenate([vp[:, :-2], vp[:, 1:-1], vp[:, 2:]], axis=2)
    qpos = jnp.arange(BLOCK)
    kpos = jnp.arange(3 * BLOCK) - BLOCK
    rel = kpos[None, :] - qpos[:, None]
    bias = rel_bias[t5_bucket(rel)].astype(jnp.float32)
    bias = bias.transpose(2, 0, 1).reshape(ATT_KV_HEADS, G, BLOCK, 3 * BLOCK)
    abs_k = (jnp.arange(nb) * BLOCK)[:, None] + kpos[None, :]
    valid = (abs_k >= 0) & (abs_k < S)
    mask = valid[:, None, :] & (jnp.abs(rel) <= WINDOW)[None]
    scale = ATT_HEAD_DIM ** -0.5
    s = jnp.einsum('bnqhgd,bnkhd->bnhgqk', qb, kb).astype(jnp.float32) * scale + bias
    s = jnp.where(mask[None, :, None, None], s, NEG_INF)
    sink_l = sink.astype(jnp.float32).reshape(ATT_KV_HEADS, G)[None, None, :, :, None, None]
    m = jnp.maximum(jnp.max(s, axis=-1, keepdims=True), sink_l)
    p = jnp.exp(s - m)
    denom = jnp.sum(p, axis=-1, keepdims=True) + jnp.exp(sink_l - m)
    o = jnp.einsum('bnhgqk,bnkhd->bnqhgd', (p / denom).astype(vb.dtype), vb)
    return o.reshape(B, S, ATT_WIDTH)


def short_conv(u, w, b):
    L = u.shape[1]
    r = SHORT_CONV // 2
    up = jnp.pad(u, ((0, 0), (r, r), (0, 0)))
    out = b
    for j in range(SHORT_CONV):
        out = out + up[:, j:j + L] * w[j]
    return out


def hyena_filters(L, f_w_in, f_w_mid, f_b, f_freq, f_w_out):
    f32 = jnp.float32
    pos = jnp.arange(L, dtype=f32)
    t = pos / max(L - 1, 1)
    w = 2.0 * math.pi * pos / L
    bands = jnp.linspace(1e-4, FILTER_BANDS - 1, FILTER_BANDS, dtype=f32)
    ang = w[:, None] * bands[None, :]
    z = jnp.concatenate([t[:, None], jnp.cos(ang), -jnp.sin(ang)], axis=-1)
    fb = f_b.astype(f32)
    ff = f_freq.astype(f32)
    h = jnp.sin(ff[0] * (z @ f_w_in.astype(f32) + fb[0]))
    for i in range(FILTER_INNER):
        h = jnp.sin(ff[i + 1] * (h @ f_w_mid[i].astype(f32) + fb[i + 1]))
    h = (h @ f_w_out.astype(f32)).reshape(L, HYENA_ORDER, 2, HYENA_WIDTH)
    max_decay = math.log(DECAY_TARGET) / FAST_DECAY_PCT
    min_decay = math.log(DECAY_TARGET) / SLOW_DECAY_PCT
    deltas = jnp.linspace(min_decay, max_decay, HYENA_WIDTH, dtype=f32)
    h = h * jnp.exp(-t[:, None, None, None] * jnp.abs(deltas))
    fwd, bwd = h[:, :, 0], h[:, :, 1]
    two = jnp.concatenate([fwd[:1] + bwd[:1], fwd[1:], jnp.zeros_like(fwd[:1]), bwd[:0:-1]], axis=0)
    two = two / jnp.sum(jnp.abs(two), axis=0, keepdims=True)
    return jnp.fft.rfft(two, axis=0)


def fft_long_conv(z, filt_f, bias):
    L = z.shape[1]
    zf = jnp.fft.rfft(z.astype(jnp.float32), n=2 * L, axis=1)
    y = jnp.fft.irfft(zf * filt_f[None], n=2 * L, axis=1)[:, :L]
    return (y + z.astype(jnp.float32) * bias.astype(jnp.float32)).astype(z.dtype)


def hyena_mixer(u, conv_w, conv_b, f_w_in, f_w_mid, f_b, f_freq, f_w_out, hyena_bias):
    L = u.shape[1]
    uc = short_conv(u, conv_w, conv_b)
    parts = jnp.split(uc, HYENA_ORDER + 1, axis=-1)
    filt = hyena_filters(L, f_w_in, f_w_mid, f_b, f_freq, f_w_out)
    z = parts[0]
    for o in range(HYENA_ORDER):
        z = parts[o + 1] * fft_long_conv(z, filt[:, o], hyena_bias[o])
    return z


def memory_attention(q, mem, w_mem_kv):
    B, S = q.shape[0], q.shape[1]
    M = mem.shape[1]
    kv = mem @ w_mem_kv
    k = kv[..., :MEM_WIDTH].reshape(B, M, MEM_HEADS, MEM_HEAD_DIM)
    v = kv[..., MEM_WIDTH:].reshape(B, M, MEM_HEADS, MEM_HEAD_DIM)
    qh = q.reshape(B, S, MEM_HEADS, MEM_HEAD_DIM)
    s = jnp.einsum('bshd,bmhd->bhsm', qh, k).astype(jnp.float32) * (MEM_HEAD_DIM ** -0.5)
    p = jax.nn.softmax(s, axis=-1).astype(v.dtype)
    o = jnp.einsum('bhsm,bmhd->bshd', p, v)
    return o.reshape(B, S, MEM_WIDTH)


def peer_ffn(x, w_query, sub_keys, expert_u, expert_v):
    B, S, D = x.shape
    T = B * S
    xb = x.reshape(T // PEER_BLOCK, PEER_BLOCK, D)

    def block(xt):
        q = (xt @ w_query).reshape(PEER_BLOCK, PEER_HEADS, 2, PEER_HALF)
        s = jnp.einsum('thcd,hckd->thck', q, sub_keys).astype(jnp.float32)
        s_top, i_top = lax.top_k(s, PEER_TOPK)
        cand = (s_top[:, :, 0, :, None] + s_top[:, :, 1, None, :]).reshape(PEER_BLOCK, PEER_HEADS, PEER_TOPK * PEER_TOPK)
        cand_idx = (i_top[:, :, 0, :, None] * N_KEYS + i_top[:, :, 1, None, :]).reshape(PEER_BLOCK, PEER_HEADS, PEER_TOPK * PEER_TOPK)
        sc, pos = lax.top_k(cand, PEER_TOPK)
        eidx = jnp.take_along_axis(cand_idx, pos, axis=-1)
        g = jax.nn.softmax(sc, axis=-1)
        u = expert_u[eidx]
        act = jax.nn.gelu(jnp.einsum('thkd,td->thk', u, xt).astype(jnp.float32), approximate=False)
        coef = (g * act).astype(xt.dtype)
        return jnp.einsum('thk,thkd->td', coef, expert_v[eidx])

    return lax.map(block, xb).reshape(B, S, D)


def encoder_layer(x, mem, rel_bias, w_in, b_in, conv_w, conv_b, attn_sink, f_w_in, f_w_mid, f_b, f_freq,
                  f_w_out, hyena_bias, w_mem_kv, w_branch, w_out, ln1_g, ln1_b, w_query, sub_keys,
                  expert_u, expert_v, ln2_g, ln2_b):
    B, S, _ = x.shape
    proj = linear(x.reshape(B * S, D_MODEL), w_in, b_in, tm=512, tn=IN_WIDTH // 2).reshape(B, S, IN_WIDTH)
    q_a = proj[..., :O_K].reshape(B, S, ATT_HEADS, ATT_HEAD_DIM)
    k_a = proj[..., O_K:O_V].reshape(B, S, ATT_KV_HEADS, ATT_HEAD_DIM)
    v_a = proj[..., O_V:O_HY].reshape(B, S, ATT_KV_HEADS, ATT_HEAD_DIM)
    hy = proj[..., O_HY:O_MQ]
    q_m = proj[..., O_MQ:O_GATE]
    gates = jax.nn.sigmoid(proj[..., O_GATE:].reshape(B, S, N_BRANCH, D_MODEL))
    branches = (
        window_attention(q_a, k_a, v_a, rel_bias, attn_sink),
        hyena_mixer(hy, conv_w, conv_b, f_w_in, f_w_mid, f_b, f_freq, f_w_out, hyena_bias),
        memory_attention(q_m, mem, w_mem_kv),
    )
    merged = gates[:, :, 0] * (branches[0] @ w_branch[0])
    for n in range(1, N_BRANCH):
        merged = merged + gates[:, :, n] * (branches[n] @ w_branch[n])
    x = layer_norm(ALPHA * x + merged @ w_out, ln1_g, ln1_b)
    x = layer_norm(ALPHA * x + peer_ffn(x, w_query, sub_keys, expert_u, expert_v), ln2_g, ln2_b)
    return x


def kernel(x_prompt, x_sample, mem_prompt, mem_sample, rel_bias, w_in, b_in, conv_w, conv_b, attn_sink,
           f_w_in, f_w_mid, f_b, f_freq, f_w_out, hyena_bias, w_mem_kv, w_branch, w_out, ln1_g, ln1_b,
           w_query, sub_keys, expert_u, expert_v, ln2_g, ln2_b):
    nb = x_prompt.shape[0]
    x = jnp.concatenate([x_prompt, x_sample], axis=0)
    mem = jnp.concatenate([mem_prompt, mem_sample], axis=0)
    for l in range(DEPTH):
        x = encoder_layer(x, mem, rel_bias, w_in[l], b_in[l], conv_w[l], conv_b[l], attn_sink[l],
                          f_w_in[l], f_w_mid[l], f_b[l], f_freq[l], f_w_out[l], hyena_bias[l],
                          w_mem_kv[l], w_branch[l], w_out[l], ln1_g[l], ln1_b[l], w_query[l],
                          sub_keys[l], expert_u[l], expert_v[l], ln2_g[l], ln2_b[l])
    return (x[:nb], x[nb:])
```

```python
import functools
import math

import jax
import jax.numpy as jnp
from jax import lax
from jax.experimental import pallas as pl
from jax.experimental.pallas import tpu as pltpu

D_MODEL = 1024
DEPTH = 2
N_MEM = 256
ATT_HEADS = 8
ATT_KV_HEADS = 2
ATT_HEAD_DIM = 64
ATT_WIDTH = ATT_HEADS * ATT_HEAD_DIM
ATT_KV_WIDTH = ATT_KV_HEADS * ATT_HEAD_DIM
WINDOW = 128
BLOCK = 128
N_BUCKETS = 32
MAX_DISTANCE = 128
HYENA_WIDTH = 512
HYENA_ORDER = 2
SHORT_CONV = 3
FILTER_EMB = 33
FILTER_BANDS = (FILTER_EMB - 1) // 2
FILTER_HIDDEN = 64
FILTER_INNER = 2
FAST_DECAY_PCT = 0.3
SLOW_DECAY_PCT = 1.5
DECAY_TARGET = 1e-2
MEM_HEADS = 4
MEM_HEAD_DIM = 128
MEM_WIDTH = MEM_HEADS * MEM_HEAD_DIM
N_BRANCH = 3
BRANCH_WIDTH = 512
PEER_HEADS = 8
N_KEYS = 128
N_EXPERTS = N_KEYS * N_KEYS
PEER_TOPK = 16
PEER_HALF = 128
PEER_QDIM = 2 * PEER_HALF
PEER_BLOCK = 128
O_K = ATT_WIDTH
O_V = O_K + ATT_KV_WIDTH
O_HY = O_V + ATT_KV_WIDTH
O_MQ = O_HY + (HYENA_ORDER + 1) * HYENA_WIDTH
O_GATE = O_MQ + MEM_WIDTH
IN_WIDTH = O_GATE + N_BRANCH * D_MODEL
ALPHA = (2 * DEPTH) ** 0.25
BETA = (8 * DEPTH) ** -0.25
LN_EPS = 1e-5
NEG_INF = -1e30

VMEM_LIMIT_BYTES = 56 * 1024 * 1024


def _linear_kernel(x_ref, w_ref, b_ref, o_ref):
    x = x_ref[...].astype(jnp.bfloat16)
    acc = jnp.dot(x, w_ref[...], preferred_element_type=jnp.float32)
    o_ref[...] = acc + b_ref[...]


def linear(x, w, b, *, tm=512, tn=None):
    T, K = x.shape
    N = w.shape[1]
    tn = N if tn is None else tn
    wb = w.astype(jnp.bfloat16)
    return pl.pallas_call(
        _linear_kernel,
        grid=(N // tn, T // tm),
        in_specs=[
            pl.BlockSpec((tm, K), lambda j, i: (i, 0)),
            pl.BlockSpec((K, tn), lambda j, i: (0, j)),
            pl.BlockSpec((1, tn), lambda j, i: (0, j)),
        ],
        out_specs=pl.BlockSpec((tm, tn), lambda j, i: (i, j)),
        out_shape=jax.ShapeDtypeStruct((T, N), jnp.float32),
        compiler_params=pltpu.CompilerParams(
            dimension_semantics=("arbitrary", "arbitrary"),
            vmem_limit_bytes=VMEM_LIMIT_BYTES),
        name="linear",
    )(x, wb, b.reshape(1, N))


def layer_norm(x, g, b):
    mu = jnp.mean(x, axis=-1, keepdims=True)
    var = jnp.mean(jnp.square(x - mu), axis=-1, keepdims=True)
    return (x - mu) * lax.rsqrt(var + LN_EPS) * g + b


def t5_bucket(rel):
    nb = N_BUCKETS // 2
    max_exact = nb // 2
    ret = jnp.where(rel > 0, nb, 0)
    n = jnp.abs(rel)
    nf = jnp.maximum(n, 1).astype(jnp.float32)
    large = max_exact + (jnp.log(nf / max_exact) / math.log(MAX_DISTANCE / max_exact)
                         * (nb - max_exact)).astype(jnp.int32)
    large = jnp.minimum(large, nb - 1)
    return ret + jnp.where(n < max_exact, n, large)


def window_attention(q, k, v, rel_bias, sink):
    B, S = q.shape[0], q.shape[1]
    nb = S // BLOCK
    G = ATT_HEADS // ATT_KV_HEADS
    qb = q.reshape(B, nb, BLOCK, ATT_KV_HEADS, G, ATT_HEAD_DIM)
    pad = ((0, 0), (BLOCK, BLOCK), (0, 0), (0, 0))
    kp = jnp.pad(k, pad).reshape(B, nb + 2, BLOCK, ATT_KV_HEADS, ATT_HEAD_DIM)
    vp = jnp.pad(v, pad).reshape(B, nb + 2, BLOCK, ATT_KV_HEADS, ATT_HEAD_DIM)
    kb = jnp.concatenate([kp[:, :-2], kp[:, 1:-1], kp[:, 2:]], axis=2)
    vb = jnp.concatenate([vp[:, :-2], vp[:, 1:-1], vp[:, 2:]], axis=2)
    qpos = jnp.arange(BLOCK)
    kpos = jnp.arange(3 * BLOCK) - BLOCK
    rel = kpos[None, :] - qpos[:, None]
    bias = rel_bias[t5_bucket(rel)].astype(jnp.float32)
    bias = bias.transpose(2, 0, 1).reshape(ATT_KV_HEADS, G, BLOCK, 3 * BLOCK)
    abs_k = (jnp.arange(nb) * BLOCK)[:, None] + kpos[None, :]
    valid = (abs_k >= 0) & (abs_k < S)
    mask = valid[:, None, :] & (jnp.abs(rel) <= WINDOW)[None]
    scale = ATT_HEAD_DIM ** -0.5
    s = jnp.einsum('bnqhgd,bnkhd->bnhgqk', qb, kb).astype(jnp.float32) * scale + bias
    s = jnp.where(mask[None, :, None, None], s, NEG_INF)
    sink_l = sink.astype(jnp.float32).reshape(ATT_KV_HEADS, G)[None, None, :, :, None, None]
    m = jnp.maximum(jnp.max(s, axis=-1, keepdims=True), sink_l)
    p = jnp.exp(s - m)
    denom = jnp.sum(p, axis=-1, keepdims=True) + jnp.exp(sink_l - m)
    o = jnp.einsum('bnhgqk,bnkhd->bnqhgd', (p / denom).astype(vb.dtype), vb)
    return o.reshape(B, S, ATT_WIDTH)


def short_conv(u, w, b):
    L = u.shape[1]
    r = SHORT_CONV // 2
    up = jnp.pad(u, ((0, 0), (r, r), (0, 0)))
    out = b
    for j in range(SHORT_CONV):
        out = out + up[:, j:j + L] * w[j]
    return out


def hyena_filters(L, f_w_in, f_w_mid, f_b, f_freq, f_w_out):
    f32 = jnp.float32
    pos = jnp.arange(L, dtype=f32)
    t = pos / max(L - 1, 1)
    w = 2.0 * math.pi * pos / L
    bands = jnp.linspace(1e-4, FILTER_BANDS - 1, FILTER_BANDS, dtype=f32)
    ang = w[:, None] * bands[None, :]
    z = jnp.concatenate([t[:, None], jnp.cos(ang), -jnp.sin(ang)], axis=-1)
    fb = f_b.astype(f32)
    ff = f_freq.astype(f32)
    h = jnp.sin(ff[0] * (z @ f_w_in.astype(f32) + fb[0]))
    for i in range(FILTER_INNER):
        h = jnp.sin(ff[i + 1] * (h @ f_w_mid[i].astype(f32) + fb[i + 1]))
    h = (h @ f_w_out.astype(f32)).reshape(L, HYENA_ORDER, 2, HYENA_WIDTH)
    max_decay = math.log(DECAY_TARGET) / FAST_DECAY_PCT
    min_decay = math.log(DECAY_TARGET) / SLOW_DECAY_PCT
    deltas = jnp.linspace(min_decay, max_decay, HYENA_WIDTH, dtype=f32)
    h = h * jnp.exp(-t[:, None, None, None] * jnp.abs(deltas))
    fwd, bwd = h[:, :, 0], h[:, :, 1]
    two = jnp.concatenate([fwd[:1] + bwd[:1], fwd[1:], jnp.zeros_like(fwd[:1]), bwd[:0:-1]], axis=0)
    two = two / jnp.sum(jnp.abs(two), axis=0, keepdims=True)
    return jnp.fft.rfft(two, axis=0)


def fft_long_conv(z, filt_f, bias):
    L = z.shape[1]
    zf = jnp.fft.rfft(z.astype(jnp.float32), n=2 * L, axis=1)
    y = jnp.fft.irfft(zf * filt_f[None], n=2 * L, axis=1)[:, :L]
    return (y + z.astype(jnp.float32) * bias.astype(jnp.float32)).astype(z.dtype)


def hyena_mixer(u, conv_w, conv_b, f_w_in, f_w_mid, f_b, f_freq, f_w_out, hyena_bias):
    L = u.shape[1]
    uc = short_conv(u, conv_w, conv_b)
    parts = jnp.split(uc, HYENA_ORDER + 1, axis=-1)
    filt = hyena_filters(L, f_w_in, f_w_mid, f_b, f_freq, f_w_out)
    z = parts[0]
    for o in range(HYENA_ORDER):
        z = parts[o + 1] * fft_long_conv(z, filt[:, o], hyena_bias[o])
    return z


def memory_attention(q, mem, w_mem_kv):
    B, S = q.shape[0], q.shape[1]
    M = mem.shape[1]
    kv = mem @ w_mem_kv
    k = kv[..., :MEM_WIDTH].reshape(B, M, MEM_HEADS, MEM_HEAD_DIM)
    v = kv[..., MEM_WIDTH:].reshape(B, M, MEM_HEADS, MEM_HEAD_DIM)
    qh = q.reshape(B, S, MEM_HEADS, MEM_HEAD_DIM)
    s = jnp.einsum('bshd,bmhd->bhsm', qh, k).astype(jnp.float32) * (MEM_HEAD_DIM ** -0.5)
    p = jax.nn.softmax(s, axis=-1).astype(v.dtype)
    o = jnp.einsum('bhsm,bmhd->bshd', p, v)
    return o.reshape(B, S, MEM_WIDTH)


PEER_SEL = PEER_HEADS * PEER_TOPK
PEER_TB = 64
VREG_SUBLANES = 8
VREG_LANES = 128
EXPERT_ROWS = 4
_COMBINE_POS = (3, 7, 1, 5, 2, 6, 0, 4)


def pack_expert_table(tab):
    e = tab.shape[0]
    b = lax.bitcast_convert_type(tab.astype(jnp.bfloat16), jnp.uint16).astype(jnp.uint32)
    b = b.reshape(e, 2, EXPERT_ROWS, VREG_LANES)
    w = (b[:, 0] | (b[:, 1] << 16)).reshape(e * EXPERT_ROWS, VREG_LANES)
    return jnp.pad(w, ((0, VREG_SUBLANES), (0, 0)))


def _expert_halves(tab_ref, row):
    w = tab_ref[pl.ds(row, VREG_SUBLANES), :]
    lo = lax.bitcast_convert_type(w << 16, jnp.float32)
    hi = lax.bitcast_convert_type(w & jnp.uint32(0xFFFF0000), jnp.float32)
    return lo, hi


def _sublane_sums(ps, sub):
    lo4 = (sub & 4) == 0
    c = []
    for a, b in ((ps[0], ps[1]), (ps[2], ps[3]), (ps[4], ps[5]), (ps[6], ps[7])):
        c.append(jnp.where(lo4, a, pltpu.roll(b, 4, 0)))
    mv = (sub & 2) != 0
    e = []
    for c1, c2 in ((c[0], c[1]), (c[2], c[3])):
        e.append(jnp.where(mv, c1 + pltpu.roll(c1, 2, 0), c2 + pltpu.roll(c2, 6, 0)))
    mo = (sub & 1) != 0
    return jnp.where(mo, e[0] + pltpu.roll(e[0], 1, 0), e[1] + pltpu.roll(e[1], 7, 0))


def _gelu_exact(x):
    return 0.5 * x * (1.0 + lax.erf(x * (2.0 ** -0.5)))


def _peer_u_kernel(idx_ref, x_ref, g_ref, tab_ref, o_ref, act_ref):
    sub = lax.broadcasted_iota(jnp.int32, (VREG_SUBLANES, VREG_LANES), 0)
    ones = jnp.ones((VREG_SUBLANES, VREG_LANES), jnp.bfloat16)
    nt = (((1,), (1,)), ((), ()))

    def token(t, carry):
        x_lo = x_ref[t]
        x_hi = pltpu.roll(x_lo, EXPERT_ROWS, 0)
        rs = []
        for grp in range(PEER_SEL // VREG_SUBLANES):
            ps = [None] * VREG_SUBLANES
            for q in range(VREG_SUBLANES):
                lo, hi = _expert_halves(tab_ref, idx_ref[t, grp * VREG_SUBLANES + _COMBINE_POS[q]])
                ps[q] = lo * x_lo + hi * x_hi
            rs.append(_sublane_sums(ps, sub))
        r = jnp.concatenate(rs, axis=0)
        r_hi = r.astype(jnp.bfloat16)
        r_lo = (r - r_hi.astype(jnp.float32)).astype(jnp.bfloat16)
        s = (lax.dot_general(ones, r_hi, nt, preferred_element_type=jnp.float32)
             + lax.dot_general(ones, r_lo, nt, preferred_element_type=jnp.float32))
        act_ref[pl.ds(t, 1), :] = s[0:1]
        return carry

    lax.fori_loop(0, PEER_TB, token, 0)
    o_ref[...] = g_ref[...] * _gelu_exact(act_ref[...])


def _peer_v_kernel(idx_ref, coef_ref, tab_ref, o_ref):
    n_acc = 2
    sub = lax.broadcasted_iota(jnp.int32, (VREG_SUBLANES, VREG_LANES), 0)

    def token(t, carry):
        zero = jnp.zeros((VREG_SUBLANES, VREG_LANES), jnp.float32)
        acc_lo = [zero] * n_acc
        acc_hi = [zero] * n_acc
        for k in range(PEER_SEL):
            lo, hi = _expert_halves(tab_ref, idx_ref[t, k])
            c = coef_ref[t, k]
            acc_lo[k % n_acc] = acc_lo[k % n_acc] + lo * c
            acc_hi[k % n_acc] = acc_hi[k % n_acc] + hi * c
        o_ref[t] = jnp.where(sub < EXPERT_ROWS, acc_lo[0] + acc_lo[1],
                             pltpu.roll(acc_hi[0] + acc_hi[1], EXPERT_ROWS, 0))
        return carry

    lax.fori_loop(0, PEER_TB, token, 0)


def peer_experts(x, eidx, g, tab_u, tab_v):
    T = x.shape[0]
    grid = (T // PEER_TB,)
    smem_blk = pl.BlockSpec((PEER_TB, PEER_SEL), lambda i: (i, 0), memory_space=pltpu.SMEM)
    vec_blk = pl.BlockSpec((PEER_TB, PEER_SEL), lambda i: (i, 0))
    row_blk = pl.BlockSpec((PEER_TB, VREG_SUBLANES, VREG_LANES), lambda i: (i, 0, 0))
    tab_spec = pl.BlockSpec(memory_space=pltpu.VMEM)
    params = pltpu.CompilerParams(dimension_semantics=("arbitrary",), vmem_limit_bytes=VMEM_LIMIT_BYTES)
    coef = pl.pallas_call(
        _peer_u_kernel,
        grid=grid,
        in_specs=[smem_blk, row_blk, vec_blk, tab_spec],
        out_specs=vec_blk,
        out_shape=jax.ShapeDtypeStruct((T, PEER_SEL), jnp.float32),
        scratch_shapes=[pltpu.VMEM((PEER_TB, PEER_SEL), jnp.float32)],
        compiler_params=params,
        name="peer_u",
    )(eidx, x.reshape(T, VREG_SUBLANES, VREG_LANES), g, tab_u)
    out = pl.pallas_call(
        _peer_v_kernel,
        grid=grid,
        in_specs=[smem_blk, smem_blk, tab_spec],
        out_specs=row_blk,
        out_shape=jax.ShapeDtypeStruct((T, VREG_SUBLANES, VREG_LANES), jnp.float32),
        compiler_params=params,
        name="peer_v",
    )(eidx, coef, tab_v)
    return out.reshape(T, D_MODEL)


def peer_route(x, w_query, sub_keys):
    T = x.shape[0]
    q = linear(x, w_query, jnp.zeros((w_query.shape[1],), jnp.float32), tm=512)
    q = q.reshape(T, PEER_HEADS, 2, PEER_HALF)
    s = jnp.einsum('thcd,hckd->thck', q, sub_keys).astype(jnp.float32)
    s_top, i_top = lax.top_k(s, PEER_TOPK)
    cand = (s_top[:, :, 0, :, None] + s_top[:, :, 1, None, :]).reshape(T, PEER_HEADS, PEER_TOPK * PEER_TOPK)
    cand_idx = (i_top[:, :, 0, :, None] * N_KEYS + i_top[:, :, 1, None, :]).reshape(T, PEER_HEADS, PEER_TOPK * PEER_TOPK)
    sc, pos = lax.top_k(cand, PEER_TOPK)
    eidx = jnp.take_along_axis(cand_idx, pos, axis=-1)
    g = jax.nn.softmax(sc, axis=-1)
    rows = (eidx * EXPERT_ROWS).reshape(T, PEER_SEL).astype(jnp.int32)
    return rows, g.reshape(T, PEER_SEL)


def peer_ffn(x, w_query, sub_keys, tab_u, tab_v):
    B, S, D = x.shape
    xf = x.reshape(B * S, D)
    eidx, g = peer_route(xf, w_query, sub_keys)
    return peer_experts(xf, eidx, g, tab_u, tab_v).reshape(B, S, D)


def encoder_layer(x, mem, rel_bias, w_in, b_in, conv_w, conv_b, attn_sink, f_w_in, f_w_mid, f_b, f_freq,
                  f_w_out, hyena_bias, w_mem_kv, w_branch, w_out, ln1_g, ln1_b, w_query, sub_keys,
                  expert_u, expert_v, ln2_g, ln2_b):
    B, S, _ = x.shape
    proj = linear(x.reshape(B * S, D_MODEL), w_in, b_in, tm=512, tn=IN_WIDTH // 2).reshape(B, S, IN_WIDTH)
    q_a = proj[..., :O_K].reshape(B, S, ATT_HEADS, ATT_HEAD_DIM)
    k_a = proj[..., O_K:O_V].reshape(B, S, ATT_KV_HEADS, ATT_HEAD_DIM)
    v_a = proj[..., O_V:O_HY].reshape(B, S, ATT_KV_HEADS, ATT_HEAD_DIM)
    hy = proj[..., O_HY:O_MQ]
    q_m = proj[..., O_MQ:O_GATE]
    gates = jax.nn.sigmoid(proj[..., O_GATE:].reshape(B, S, N_BRANCH, D_MODEL))
    branches = (
        window_attention(q_a, k_a, v_a, rel_bias, attn_sink),
        hyena_mixer(hy, conv_w, conv_b, f_w_in, f_w_mid, f_b, f_freq, f_w_out, hyena_bias),
        memory_attention(q_m, mem, w_mem_kv),
    )
    merged = gates[:, :, 0] * (branches[0] @ w_branch[0])
    for n in range(1, N_BRANCH):
        merged = merged + gates[:, :, n] * (branches[n] @ w_branch[n])
    x = layer_norm(ALPHA * x + merged @ w_out, ln1_g, ln1_b)
    x = layer_norm(ALPHA * x + peer_ffn(x, w_query, sub_keys, pack_expert_table(expert_u), pack_expert_table(expert_v)), ln2_g, ln2_b)
    return x


def kernel(x_prompt, x_sample, mem_prompt, mem_sample, rel_bias, w_in, b_in, conv_w, conv_b, attn_sink,
           f_w_in, f_w_mid, f_b, f_freq, f_w_out, hyena_bias, w_mem_kv, w_branch, w_out, ln1_g, ln1_b,
           w_query, sub_keys, expert_u, expert_v, ln2_g, ln2_b):
    nb = x_prompt.shape[0]
    x = jnp.concatenate([x_prompt, x_sample], axis=0)
    mem = jnp.concatenate([mem_prompt, mem_sample], axis=0)
    for l in range(DEPTH):
        x = encoder_layer(x, mem, rel_bias, w_in[l], b_in[l], conv_w[l], conv_b[l], attn_sink[l],
                          f_w_in[l], f_w_mid[l], f_b[l], f_freq[l], f_w_out[l], hyena_bias[l],
                          w_mem_kv[l], w_branch[l], w_out[l], ln1_g[l], ln1_b[l], w_query[l],
                          sub_keys[l], expert_u[l], expert_v[l], ln2_g[l], ln2_b[l])
    return (x[:nb], x[nb:])
```

```python
import functools
import math

import jax
import jax.numpy as jnp
from jax import lax
from jax.experimental import pallas as pl
from jax.experimental.pallas import tpu as pltpu

D_MODEL = 1024
DEPTH = 2
N_MEM = 256
ATT_HEADS = 8
ATT_KV_HEADS = 2
ATT_HEAD_DIM = 64
ATT_WIDTH = ATT_HEADS * ATT_HEAD_DIM
ATT_KV_WIDTH = ATT_KV_HEADS * ATT_HEAD_DIM
WINDOW = 128
BLOCK = 128
N_BUCKETS = 32
MAX_DISTANCE = 128
HYENA_WIDTH = 512
HYENA_ORDER = 2
SHORT_CONV = 3
FILTER_EMB = 33
FILTER_BANDS = (FILTER_EMB - 1) // 2
FILTER_HIDDEN = 64
FILTER_INNER = 2
FAST_DECAY_PCT = 0.3
SLOW_DECAY_PCT = 1.5
DECAY_TARGET = 1e-2
MEM_HEADS = 4
MEM_HEAD_DIM = 128
MEM_WIDTH = MEM_HEADS * MEM_HEAD_DIM
N_BRANCH = 3
BRANCH_WIDTH = 512
PEER_HEADS = 8
N_KEYS = 128
N_EXPERTS = N_KEYS * N_KEYS
PEER_TOPK = 16
PEER_HALF = 128
PEER_QDIM = 2 * PEER_HALF
PEER_BLOCK = 128
O_K = ATT_WIDTH
O_V = O_K + ATT_KV_WIDTH
O_HY = O_V + ATT_KV_WIDTH
O_MQ = O_HY + (HYENA_ORDER + 1) * HYENA_WIDTH
O_GATE = O_MQ + MEM_WIDTH
IN_WIDTH = O_GATE + N_BRANCH * D_MODEL
ALPHA = (2 * DEPTH) ** 0.25
BETA = (8 * DEPTH) ** -0.25
LN_EPS = 1e-5
NEG_INF = -1e30

VMEM_LIMIT_BYTES = 56 * 1024 * 1024


def _linear_kernel(x_ref, w_ref, b_ref, o_ref):
    x = x_ref[...].astype(jnp.bfloat16)
    acc = jnp.dot(x, w_ref[...], preferred_element_type=jnp.float32)
    o_ref[...] = acc + b_ref[...]


def linear(x, w, b, *, tm=512, tn=None):
    T, K = x.shape
    N = w.shape[1]
    tn = N if tn is None else tn
    wb = w.astype(jnp.bfloat16)
    return pl.pallas_call(
        _linear_kernel,
        grid=(N // tn, T // tm),
        in_specs=[
            pl.BlockSpec((tm, K), lambda j, i: (i, 0)),
            pl.BlockSpec((K, tn), lambda j, i: (0, j)),
            pl.BlockSpec((1, tn), lambda j, i: (0, j)),
        ],
        out_specs=pl.BlockSpec((tm, tn), lambda j, i: (i, j)),
        out_shape=jax.ShapeDtypeStruct((T, N), jnp.float32),
        compiler_params=pltpu.CompilerParams(
            dimension_semantics=("arbitrary", "arbitrary"),
            vmem_limit_bytes=VMEM_LIMIT_BYTES),
        name="linear",
    )(x, wb, b.reshape(1, N))


def layer_norm(x, g, b):
    mu = jnp.mean(x, axis=-1, keepdims=True)
    var = jnp.mean(jnp.square(x - mu), axis=-1, keepdims=True)
    return (x - mu) * lax.rsqrt(var + LN_EPS) * g + b


def t5_bucket(rel):
    nb = N_BUCKETS // 2
    max_exact = nb // 2
    ret = jnp.where(rel > 0, nb, 0)
    n = jnp.abs(rel)
    nf = jnp.maximum(n, 1).astype(jnp.float32)
    large = max_exact + (jnp.log(nf / max_exact) / math.log(MAX_DISTANCE / max_exact)
                         * (nb - max_exact)).astype(jnp.int32)
    large = jnp.minimum(large, nb - 1)
    return ret + jnp.where(n < max_exact, n, large)


def window_attention(q, k, v, rel_bias, sink):
    B, S = q.shape[0], q.shape[1]
    nb = S // BLOCK
    G = ATT_HEADS // ATT_KV_HEADS
    qb = q.reshape(B, nb, BLOCK, ATT_KV_HEADS, G, ATT_HEAD_DIM)
    pad = ((0, 0), (BLOCK, BLOCK), (0, 0), (0, 0))
    kp = jnp.pad(k, pad).reshape(B, nb + 2, BLOCK, ATT_KV_HEADS, ATT_HEAD_DIM)
    vp = jnp.pad(v, pad).reshape(B, nb + 2, BLOCK, ATT_KV_HEADS, ATT_HEAD_DIM)
    kb = jnp.concatenate([kp[:, :-2], kp[:, 1:-1], kp[:, 2:]], axis=2)
    vb = jnp.concatenate([vp[:, :-2], vp[:, 1:-1], vp[:, 2:]], axis=2)
    qpos = jnp.arange(BLOCK)
    kpos = jnp.arange(3 * BLOCK) - BLOCK
    rel = kpos[None, :] - qpos[:, None]
    bias = rel_bias[t5_bucket(rel)].astype(jnp.float32)
    bias = bias.transpose(2, 0, 1).reshape(ATT_KV_HEADS, G, BLOCK, 3 * BLOCK)
    abs_k = (jnp.arange(nb) * BLOCK)[:, None] + kpos[None, :]
    valid = (abs_k >= 0) & (abs_k < S)
    mask = valid[:, None, :] & (jnp.abs(rel) <= WINDOW)[None]
    scale = ATT_HEAD_DIM ** -0.5
    s = jnp.einsum('bnqhgd,bnkhd->bnhgqk', qb, kb).astype(jnp.float32) * scale + bias
    s = jnp.where(mask[None, :, None, None], s, NEG_INF)
    sink_l = sink.astype(jnp.float32).reshape(ATT_KV_HEADS, G)[None, None, :, :, None, None]
    m = jnp.maximum(jnp.max(s, axis=-1, keepdims=True), sink_l)
    p = jnp.exp(s - m)
    denom = jnp.sum(p, axis=-1, keepdims=True) + jnp.exp(sink_l - m)
    o = jnp.einsum('bnhgqk,bnkhd->bnqhgd', (p / denom).astype(vb.dtype), vb)
    return o.reshape(B, S, ATT_WIDTH)


def short_conv(u, w, b):
    L = u.shape[1]
    r = SHORT_CONV // 2
    up = jnp.pad(u, ((0, 0), (r, r), (0, 0)))
    out = b
    for j in range(SHORT_CONV):
        out = out + up[:, j:j + L] * w[j]
    return out


def hyena_filters(L, f_w_in, f_w_mid, f_b, f_freq, f_w_out):
    f32 = jnp.float32
    pos = jnp.arange(L, dtype=f32)
    t = pos / max(L - 1, 1)
    w = 2.0 * math.pi * pos / L
    bands = jnp.linspace(1e-4, FILTER_BANDS - 1, FILTER_BANDS, dtype=f32)
    ang = w[:, None] * bands[None, :]
    z = jnp.concatenate([t[:, None], jnp.cos(ang), -jnp.sin(ang)], axis=-1)
    fb = f_b.astype(f32)
    ff = f_freq.astype(f32)
    h = jnp.sin(ff[0] * (z @ f_w_in.astype(f32) + fb[0]))
    for i in range(FILTER_INNER):
        h = jnp.sin(ff[i + 1] * (h @ f_w_mid[i].astype(f32) + fb[i + 1]))
    h = (h @ f_w_out.astype(f32)).reshape(L, HYENA_ORDER, 2, HYENA_WIDTH)
    max_decay = math.log(DECAY_TARGET) / FAST_DECAY_PCT
    min_decay = math.log(DECAY_TARGET) / SLOW_DECAY_PCT
    deltas = jnp.linspace(min_decay, max_decay, HYENA_WIDTH, dtype=f32)
    h = h * jnp.exp(-t[:, None, None, None] * jnp.abs(deltas))
    fwd, bwd = h[:, :, 0], h[:, :, 1]
    two = jnp.concatenate([fwd[:1] + bwd[:1], fwd[1:], jnp.zeros_like(fwd[:1]), bwd[:0:-1]], axis=0)
    two = two / jnp.sum(jnp.abs(two), axis=0, keepdims=True)
    return jnp.fft.rfft(two, axis=0)


def fft_long_conv(z, filt_f, bias):
    L = z.shape[1]
    zf = jnp.fft.rfft(z.astype(jnp.float32), n=2 * L, axis=1)
    y = jnp.fft.irfft(zf * filt_f[None], n=2 * L, axis=1)[:, :L]
    return (y + z.astype(jnp.float32) * bias.astype(jnp.float32)).astype(z.dtype)


def hyena_mixer(u, conv_w, conv_b, f_w_in, f_w_mid, f_b, f_freq, f_w_out, hyena_bias):
    L = u.shape[1]
    uc = short_conv(u, conv_w, conv_b)
    parts = jnp.split(uc, HYENA_ORDER + 1, axis=-1)
    filt = hyena_filters(L, f_w_in, f_w_mid, f_b, f_freq, f_w_out)
    z = parts[0]
    for o in range(HYENA_ORDER):
        z = parts[o + 1] * fft_long_conv(z, filt[:, o], hyena_bias[o])
    return z


def memory_attention(q, mem, w_mem_kv):
    B, S = q.shape[0], q.shape[1]
    M = mem.shape[1]
    kv = mem @ w_mem_kv
    k = kv[..., :MEM_WIDTH].reshape(B, M, MEM_HEADS, MEM_HEAD_DIM)
    v = kv[..., MEM_WIDTH:].reshape(B, M, MEM_HEADS, MEM_HEAD_DIM)
    qh = q.reshape(B, S, MEM_HEADS, MEM_HEAD_DIM)
    s = jnp.einsum('bshd,bmhd->bhsm', qh, k).astype(jnp.float32) * (MEM_HEAD_DIM ** -0.5)
    p = jax.nn.softmax(s, axis=-1).astype(v.dtype)
    o = jnp.einsum('bhsm,bmhd->bshd', p, v)
    return o.reshape(B, S, MEM_WIDTH)


PEER_SEL = PEER_HEADS * PEER_TOPK
PEER_TB = 64
VREG_SUBLANES = 8
VREG_LANES = 128
EXPERT_ROWS = 4
_COMBINE_POS = (3, 7, 1, 5, 2, 6, 0, 4)


def pack_expert_table(tab):
    e = tab.shape[0]
    b = lax.bitcast_convert_type(tab.astype(jnp.bfloat16), jnp.uint16).astype(jnp.uint32)
    b = b.reshape(e, 2, EXPERT_ROWS, VREG_LANES)
    w = (b[:, 0] | (b[:, 1] << 16)).reshape(e * EXPERT_ROWS, VREG_LANES)
    return jnp.pad(w, ((0, VREG_SUBLANES), (0, 0)))


def _expert_halves(tab_ref, row):
    w = tab_ref[pl.ds(row, VREG_SUBLANES), :]
    lo = lax.bitcast_convert_type(w << 16, jnp.float32)
    hi = lax.bitcast_convert_type(w & jnp.uint32(0xFFFF0000), jnp.float32)
    return lo, hi


def _sublane_sums(ps, sub):
    lo4 = (sub & 4) == 0
    c = []
    for a, b in ((ps[0], ps[1]), (ps[2], ps[3]), (ps[4], ps[5]), (ps[6], ps[7])):
        c.append(jnp.where(lo4, a, pltpu.roll(b, 4, 0)))
    mv = (sub & 2) != 0
    e = []
    for c1, c2 in ((c[0], c[1]), (c[2], c[3])):
        e.append(jnp.where(mv, c1 + pltpu.roll(c1, 2, 0), c2 + pltpu.roll(c2, 6, 0)))
    mo = (sub & 1) != 0
    return jnp.where(mo, e[0] + pltpu.roll(e[0], 1, 0), e[1] + pltpu.roll(e[1], 7, 0))


def _gelu_exact(x):
    return 0.5 * x * (1.0 + lax.erf(x * (2.0 ** -0.5)))


def _peer_u_kernel(idx_ref, x_ref, g_ref, tab_ref, o_ref, act_ref, rhi_ref, rlo_ref):
    sub = lax.broadcasted_iota(jnp.int32, (VREG_SUBLANES, VREG_LANES), 0)
    ones = jnp.ones((VREG_SUBLANES, VREG_LANES), jnp.bfloat16)
    nt = (((1,), (1,)), ((), ()))

    def token(t, carry):
        x_lo = x_ref[t]
        x_hi = pltpu.roll(x_lo, EXPERT_ROWS, 0)
        rs = []
        for grp in range(PEER_SEL // VREG_SUBLANES):
            ps = [None] * VREG_SUBLANES
            for q in range(VREG_SUBLANES):
                lo, hi = _expert_halves(tab_ref, idx_ref[t, grp * VREG_SUBLANES + _COMBINE_POS[q]])
                ps[q] = lo * x_lo + hi * x_hi
            rs.append(_sublane_sums(ps, sub))
        r = jnp.concatenate(rs, axis=0)
        r_hi = r.astype(jnp.bfloat16)
        r_lo = (r - r_hi.astype(jnp.float32)).astype(jnp.bfloat16)
        r0 = pl.multiple_of(t * PEER_SEL, PEER_SEL)
        rhi_ref[pl.ds(r0, PEER_SEL), :] = r_hi
        rlo_ref[pl.ds(r0, PEER_SEL), :] = r_lo
        return carry

    lax.fori_loop(0, PEER_TB, token, 0)
    chunk = VREG_SUBLANES * PEER_SEL
    for c in range(PEER_TB // VREG_SUBLANES):
        s = (lax.dot_general(ones, rhi_ref[c * chunk:(c + 1) * chunk, :], nt, preferred_element_type=jnp.float32)
             + lax.dot_general(ones, rlo_ref[c * chunk:(c + 1) * chunk, :], nt, preferred_element_type=jnp.float32))
        for j in range(VREG_SUBLANES):
            act_ref[c * VREG_SUBLANES + j:c * VREG_SUBLANES + j + 1, :] = s[0:1, j * PEER_SEL:(j + 1) * PEER_SEL]
    o_ref[...] = g_ref[...] * _gelu_exact(act_ref[...])


def _peer_v_kernel(idx_ref, coef_ref, tab_ref, o_ref):
    n_acc = 2
    sub = lax.broadcasted_iota(jnp.int32, (VREG_SUBLANES, VREG_LANES), 0)

    def token(t, carry):
        zero = jnp.zeros((VREG_SUBLANES, VREG_LANES), jnp.float32)
        acc_lo = [zero] * n_acc
        acc_hi = [zero] * n_acc
        for k in range(PEER_SEL):
            lo, hi = _expert_halves(tab_ref, idx_ref[t, k])
            c = coef_ref[t, k]
            acc_lo[k % n_acc] = acc_lo[k % n_acc] + lo * c
            acc_hi[k % n_acc] = acc_hi[k % n_acc] + hi * c
        o_ref[t] = jnp.where(sub < EXPERT_ROWS, acc_lo[0] + acc_lo[1],
                             pltpu.roll(acc_hi[0] + acc_hi[1], EXPERT_ROWS, 0))
        return carry

    lax.fori_loop(0, PEER_TB, token, 0)


def peer_experts(x, eidx, g, tab_u, tab_v):
    T = x.shape[0]
    grid = (T // PEER_TB,)
    smem_blk = pl.BlockSpec((PEER_TB, PEER_SEL), lambda i: (i, 0), memory_space=pltpu.SMEM)
    vec_blk = pl.BlockSpec((PEER_TB, PEER_SEL), lambda i: (i, 0))
    row_blk = pl.BlockSpec((PEER_TB, VREG_SUBLANES, VREG_LANES), lambda i: (i, 0, 0))
    tab_spec = pl.BlockSpec(memory_space=pltpu.VMEM)
    params = pltpu.CompilerParams(dimension_semantics=("arbitrary",), vmem_limit_bytes=VMEM_LIMIT_BYTES)
    coef = pl.pallas_call(
        _peer_u_kernel,
        grid=grid,
        in_specs=[smem_blk, row_blk, vec_blk, tab_spec],
        out_specs=vec_blk,
        out_shape=jax.ShapeDtypeStruct((T, PEER_SEL), jnp.float32),
        scratch_shapes=[pltpu.VMEM((PEER_TB, PEER_SEL), jnp.float32),
                        pltpu.VMEM((PEER_TB * PEER_SEL, VREG_LANES), jnp.bfloat16),
                        pltpu.VMEM((PEER_TB * PEER_SEL, VREG_LANES), jnp.bfloat16)],
        compiler_params=params,
        name="peer_u",
    )(eidx, x.reshape(T, VREG_SUBLANES, VREG_LANES), g, tab_u)
    out = pl.pallas_call(
        _peer_v_kernel,
        grid=grid,
        in_specs=[smem_blk, smem_blk, tab_spec],
        out_specs=row_blk,
        out_shape=jax.ShapeDtypeStruct((T, VREG_SUBLANES, VREG_LANES), jnp.float32),
        compiler_params=params,
        name="peer_v",
    )(eidx, coef, tab_v)
    return out.reshape(T, D_MODEL)


ROUTE_TM = 256
_NT_DIMS = (((1,), (1,)), ((), ()))


def _top16_rows(s, key_id):
    row16 = lax.broadcasted_iota(jnp.int32, (PEER_TOPK, VREG_LANES), 0)
    vals = jnp.zeros((PEER_TOPK, VREG_LANES), jnp.float32)
    ids = jnp.zeros((PEER_TOPK, VREG_LANES), jnp.int32)
    big = jnp.int32(2 ** 30)
    for j in range(PEER_TOPK):
        m = jnp.max(s, axis=0, keepdims=True)
        am = jnp.min(jnp.where(s == m, key_id, big), axis=0, keepdims=True)
        vals = jnp.where(row16 == j, m, vals)
        ids = jnp.where(row16 == j, am, ids)
        s = jnp.where(key_id == am, -jnp.inf, s)
    return vals, ids


_PAIR_GROUPS = ((0, 0, 8), (0, 8, 8), (1, 0, 8), (2, 0, 5), (3, 0, 4), (4, 0, 3), (5, 0, 2), (6, 0, 2), (7, 0, 2))


def _route_head(s0, i0, s1, i1):
    sub = lax.broadcasted_iota(jnp.int32, (VREG_SUBLANES, VREG_LANES), 0)
    cands, flat, eids = [], [], []
    for a, b0, nb in _PAIR_GROUPS:
        c = s0[a:a + 1] + s1[b0:b0 + VREG_SUBLANES]
        cands.append(jnp.where(sub < nb, c, -jnp.inf) if nb < VREG_SUBLANES else c)
        flat.append(a * PEER_TOPK + b0 + sub)
        eids.append(i0[a:a + 1] * N_KEYS + i1[b0:b0 + VREG_SUBLANES])
    cands.append(s0[VREG_SUBLANES:] + s1[0:1])
    flat.append((sub + VREG_SUBLANES) * PEER_TOPK)
    eids.append(i0[VREG_SUBLANES:] * N_KEYS + i1[0:1])
    cand = jnp.concatenate(cands, axis=0)
    flat = jnp.concatenate(flat, axis=0)
    eid = jnp.concatenate(eids, axis=0)
    row16 = lax.broadcasted_iota(jnp.int32, (PEER_TOPK, VREG_LANES), 0)
    sc = jnp.zeros((PEER_TOPK, VREG_LANES), jnp.float32)
    sel = jnp.zeros((PEER_TOPK, VREG_LANES), jnp.int32)
    big = jnp.int32(2 ** 30)
    for j in range(PEER_TOPK):
        m = jnp.max(cand, axis=0, keepdims=True)
        am = jnp.min(jnp.where(cand == m, flat, big), axis=0, keepdims=True)
        hit = flat == am
        e = jnp.max(jnp.where(hit, eid, -1), axis=0, keepdims=True)
        sc = jnp.where(row16 == j, m, sc)
        sel = jnp.where(row16 == j, e, sel)
        cand = jnp.where(hit, -jnp.inf, cand)
    p = jnp.exp(sc - sc[0:1])
    return sel, p / jnp.sum(p, axis=0, keepdims=True)


def _route_kernel(x_ref, wq_ref, sk_ref, rows_ref, g_ref, q_ref):
    q = jnp.dot(x_ref[...].astype(jnp.bfloat16), wq_ref[...], preferred_element_type=jnp.float32)
    q_ref[...] = q.astype(jnp.bfloat16)
    key_id = lax.broadcasted_iota(jnp.int32, (N_KEYS, VREG_LANES), 0)

    def head(h, carry):
        tops = []
        for c in range(2):
            hc = h * 2 + c
            qhc = q_ref[:, pl.ds(pl.multiple_of(hc * PEER_HALF, PEER_HALF), PEER_HALF)]
            s = lax.dot_general(sk_ref[hc], qhc, _NT_DIMS, preferred_element_type=jnp.float32)
            tops.append([_top16_rows(s[:, j * VREG_LANES:(j + 1) * VREG_LANES], key_id)
                         for j in range(ROUTE_TM // VREG_LANES)])
        r0 = pl.multiple_of(h * PEER_TOPK, PEER_TOPK)
        for j in range(ROUTE_TM // VREG_LANES):
            (s0, i0), (s1, i1) = tops[0][j], tops[1][j]
            sel, g = _route_head(s0, i0, s1, i1)
            rows_ref[pl.ds(r0, PEER_TOPK), j * VREG_LANES:(j + 1) * VREG_LANES] = sel * EXPERT_ROWS
            g_ref[pl.ds(r0, PEER_TOPK), j * VREG_LANES:(j + 1) * VREG_LANES] = g
        return carry

    lax.fori_loop(0, PEER_HEADS, head, 0)


def peer_route(x, w_query, sub_keys):
    T = x.shape[0]
    wq = w_query.astype(jnp.bfloat16)
    sk = sub_keys.reshape(PEER_HEADS * 2, N_KEYS, PEER_HALF).astype(jnp.bfloat16)
    out_blk = pl.BlockSpec((PEER_SEL, ROUTE_TM), lambda i: (0, i))
    rows, g = pl.pallas_call(
        _route_kernel,
        grid=(T // ROUTE_TM,),
        in_specs=[
            pl.BlockSpec((ROUTE_TM, D_MODEL), lambda i: (i, 0)),
            pl.BlockSpec(wq.shape, lambda i: (0, 0)),
            pl.BlockSpec(sk.shape, lambda i: (0, 0, 0)),
        ],
        out_specs=[out_blk, out_blk],
        out_shape=[jax.ShapeDtypeStruct((PEER_SEL, T), jnp.int32),
                   jax.ShapeDtypeStruct((PEER_SEL, T), jnp.float32)],
        scratch_shapes=[pltpu.VMEM((ROUTE_TM, PEER_HEADS * PEER_QDIM), jnp.bfloat16)],
        compiler_params=pltpu.CompilerParams(dimension_semantics=("arbitrary",),
                                             vmem_limit_bytes=VMEM_LIMIT_BYTES),
        name="peer_route",
    )(x, wq, sk)
    return rows.T, g.T


def peer_ffn(x, w_query, sub_keys, tab_u, tab_v):
    B, S, D = x.shape
    xf = x.reshape(B * S, D)
    eidx, g = peer_route(xf, w_query, sub_keys)
    return peer_experts(xf, eidx, g, tab_u, tab_v).reshape(B, S, D)


def encoder_layer(x, mem, rel_bias, w_in, b_in, conv_w, conv_b, attn_sink, f_w_in, f_w_mid, f_b, f_freq,
                  f_w_out, hyena_bias, w_mem_kv, w_branch, w_out, ln1_g, ln1_b, w_query, sub_keys,
                  expert_u, expert_v, ln2_g, ln2_b):
    B, S, _ = x.shape
    proj = linear(x.reshape(B * S, D_MODEL), w_in, b_in, tm=512, tn=IN_WIDTH // 2).reshape(B, S, IN_WIDTH)
    q_a = proj[..., :O_K].reshape(B, S, ATT_HEADS, ATT_HEAD_DIM)
    k_a = proj[..., O_K:O_V].reshape(B, S, ATT_KV_HEADS, ATT_HEAD_DIM)
    v_a = proj[..., O_V:O_HY].reshape(B, S, ATT_KV_HEADS, ATT_HEAD_DIM)
    hy = proj[..., O_HY:O_MQ]
    q_m = proj[..., O_MQ:O_GATE]
    gates = jax.nn.sigmoid(proj[..., O_GATE:].reshape(B, S, N_BRANCH, D_MODEL))
    branches = (
        window_attention(q_a, k_a, v_a, rel_bias, attn_sink),
        hyena_mixer(hy, conv_w, conv_b, f_w_in, f_w_mid, f_b, f_freq, f_w_out, hyena_bias),
        memory_attention(q_m, mem, w_mem_kv),
    )
    merged = gates[:, :, 0] * (branches[0] @ w_branch[0])
    for n in range(1, N_BRANCH):
        merged = merged + gates[:, :, n] * (branches[n] @ w_branch[n])
    x = layer_norm(ALPHA * x + merged @ w_out, ln1_g, ln1_b)
    x = layer_norm(ALPHA * x + peer_ffn(x, w_query, sub_keys, pack_expert_table(expert_u), pack_expert_table(expert_v)), ln2_g, ln2_b)
    return x


def kernel(x_prompt, x_sample, mem_prompt, mem_sample, rel_bias, w_in, b_in, conv_w, conv_b, attn_sink,
           f_w_in, f_w_mid, f_b, f_freq, f_w_out, hyena_bias, w_mem_kv, w_branch, w_out, ln1_g, ln1_b,
           w_query, sub_keys, expert_u, expert_v, ln2_g, ln2_b):
    nb = x_prompt.shape[0]
    x = jnp.concatenate([x_prompt, x_sample], axis=0)
    mem = jnp.concatenate([mem_prompt, mem_sample], axis=0)
    for l in range(DEPTH):
        x = encoder_layer(x, mem, rel_bias, w_in[l], b_in[l], conv_w[l], conv_b[l], attn_sink[l],
                          f_w_in[l], f_w_mid[l], f_b[l], f_freq[l], f_w_out[l], hyena_bias[l],
                          w_mem_kv[l], w_branch[l], w_out[l], ln1_g[l], ln1_b[l], w_query[l],
                          sub_keys[l], expert_u[l], expert_v[l], ln2_g[l], ln2_b[l])
    return (x[:nb], x[nb:])
```

```python
import functools
import math

import jax
import jax.numpy as jnp
from jax import lax
from jax.experimental import pallas as pl
from jax.experimental.pallas import tpu as pltpu

D_MODEL = 1024
DEPTH = 2
N_MEM = 256
ATT_HEADS = 8
ATT_KV_HEADS = 2
ATT_HEAD_DIM = 64
ATT_WIDTH = ATT_HEADS * ATT_HEAD_DIM
ATT_KV_WIDTH = ATT_KV_HEADS * ATT_HEAD_DIM
WINDOW = 128
BLOCK = 128
N_BUCKETS = 32
MAX_DISTANCE = 128
HYENA_WIDTH = 512
HYENA_ORDER = 2
SHORT_CONV = 3
FILTER_EMB = 33
FILTER_BANDS = (FILTER_EMB - 1) // 2
FILTER_HIDDEN = 64
FILTER_INNER = 2
FAST_DECAY_PCT = 0.3
SLOW_DECAY_PCT = 1.5
DECAY_TARGET = 1e-2
MEM_HEADS = 4
MEM_HEAD_DIM = 128
MEM_WIDTH = MEM_HEADS * MEM_HEAD_DIM
N_BRANCH = 3
BRANCH_WIDTH = 512
PEER_HEADS = 8
N_KEYS = 128
N_EXPERTS = N_KEYS * N_KEYS
PEER_TOPK = 16
PEER_HALF = 128
PEER_QDIM = 2 * PEER_HALF
PEER_BLOCK = 128
O_K = ATT_WIDTH
O_V = O_K + ATT_KV_WIDTH
O_HY = O_V + ATT_KV_WIDTH
O_MQ = O_HY + (HYENA_ORDER + 1) * HYENA_WIDTH
O_GATE = O_MQ + MEM_WIDTH
IN_WIDTH = O_GATE + N_BRANCH * D_MODEL
ALPHA = (2 * DEPTH) ** 0.25
BETA = (8 * DEPTH) ** -0.25
LN_EPS = 1e-5
NEG_INF = -1e30

VMEM_LIMIT_BYTES = 56 * 1024 * 1024


def _linear_kernel(x_ref, w_ref, b_ref, o_ref):
    x = x_ref[...].astype(jnp.bfloat16)
    acc = jnp.dot(x, w_ref[...], preferred_element_type=jnp.float32)
    o_ref[...] = acc + b_ref[...]


def linear(x, w, b, *, tm=512, tn=None):
    T, K = x.shape
    N = w.shape[1]
    tn = N if tn is None else tn
    wb = w.astype(jnp.bfloat16)
    return pl.pallas_call(
        _linear_kernel,
        grid=(N // tn, T // tm),
        in_specs=[
            pl.BlockSpec((tm, K), lambda j, i: (i, 0)),
            pl.BlockSpec((K, tn), lambda j, i: (0, j)),
            pl.BlockSpec((1, tn), lambda j, i: (0, j)),
        ],
        out_specs=pl.BlockSpec((tm, tn), lambda j, i: (i, j)),
        out_shape=jax.ShapeDtypeStruct((T, N), jnp.float32),
        compiler_params=pltpu.CompilerParams(
            dimension_semantics=("arbitrary", "arbitrary"),
            vmem_limit_bytes=VMEM_LIMIT_BYTES),
        name="linear",
    )(x, wb, b.reshape(1, N))


def layer_norm(x, g, b):
    mu = jnp.mean(x, axis=-1, keepdims=True)
    var = jnp.mean(jnp.square(x - mu), axis=-1, keepdims=True)
    return (x - mu) * lax.rsqrt(var + LN_EPS) * g + b


def t5_bucket(rel):
    nb = N_BUCKETS // 2
    max_exact = nb // 2
    ret = jnp.where(rel > 0, nb, 0)
    n = jnp.abs(rel)
    nf = jnp.maximum(n, 1).astype(jnp.float32)
    large = max_exact + (jnp.log(nf / max_exact) / math.log(MAX_DISTANCE / max_exact)
                         * (nb - max_exact)).astype(jnp.int32)
    large = jnp.minimum(large, nb - 1)
    return ret + jnp.where(n < max_exact, n, large)


def window_attention(q, k, v, rel_bias, sink):
    B, S = q.shape[0], q.shape[1]
    nb = S // BLOCK
    G = ATT_HEADS // ATT_KV_HEADS
    qb = q.reshape(B, nb, BLOCK, ATT_KV_HEADS, G, ATT_HEAD_DIM)
    pad = ((0, 0), (BLOCK, BLOCK), (0, 0), (0, 0))
    kp = jnp.pad(k, pad).reshape(B, nb + 2, BLOCK, ATT_KV_HEADS, ATT_HEAD_DIM)
    vp = jnp.pad(v, pad).reshape(B, nb + 2, BLOCK, ATT_KV_HEADS, ATT_HEAD_DIM)
    kb = jnp.concatenate([kp[:, :-2], kp[:, 1:-1], kp[:, 2:]], axis=2)
    vb = jnp.concatenate([vp[:, :-2], vp[:, 1:-1], vp[:, 2:]], axis=2)
    qpos = jnp.arange(BLOCK)
    kpos = jnp.arange(3 * BLOCK) - BLOCK
    rel = kpos[None, :] - qpos[:, None]
    bias = rel_bias[t5_bucket(rel)].astype(jnp.float32)
    bias = bias.transpose(2, 0, 1).reshape(ATT_KV_HEADS, G, BLOCK, 3 * BLOCK)
    abs_k = (jnp.arange(nb) * BLOCK)[:, None] + kpos[None, :]
    valid = (abs_k >= 0) & (abs_k < S)
    mask = valid[:, None, :] & (jnp.abs(rel) <= WINDOW)[None]
    scale = ATT_HEAD_DIM ** -0.5
    s = jnp.einsum('bnqhgd,bnkhd->bnhgqk', qb, kb).astype(jnp.float32) * scale + bias
    s = jnp.where(mask[None, :, None, None], s, NEG_INF)
    sink_l = sink.astype(jnp.float32).reshape(ATT_KV_HEADS, G)[None, None, :, :, None, None]
    m = jnp.maximum(jnp.max(s, axis=-1, keepdims=True), sink_l)
    p = jnp.exp(s - m)
    denom = jnp.sum(p, axis=-1, keepdims=True) + jnp.exp(sink_l - m)
    o = jnp.einsum('bnhgqk,bnkhd->bnqhgd', (p / denom).astype(vb.dtype), vb)
    return o.reshape(B, S, ATT_WIDTH)


FFT_N1 = 256
FFT_N2 = 128
FFT_N = FFT_N1 * FFT_N2
HY_COLS = FFT_N2 * HYENA_WIDTH
LMUL_TN = 2048
KB_K1 = 8
FILT_TM = 512


def _dft_constants():
    f32 = jnp.float32
    n1 = jnp.arange(FFT_N1, dtype=jnp.int32)
    ang1 = ((n1[:, None] * n1[None, :]) % FFT_N1).astype(f32) * f32(2.0 * math.pi / FFT_N1)
    fr, fi = jnp.cos(ang1), -jnp.sin(ang1)
    h = FFT_N1 // 2
    f_fwd = jnp.concatenate([jnp.concatenate([fr[:, :h], -fi[:, :h]], axis=1),
                             jnp.concatenate([fi[:, :h], fr[:, :h]], axis=1)], axis=0)
    f_inv = f_fwd.T * f32(1.0 / FFT_N)
    f_real = jnp.concatenate([fr, fi], axis=0)
    k2 = jnp.arange(FFT_N2, dtype=jnp.int32)
    k = n1[:, None, None] + FFT_N1 * k2[None, :, None]
    ang2 = ((k * k2[None, None, :]) % FFT_N).astype(f32) * f32(2.0 * math.pi / FFT_N)
    gr, gi = jnp.cos(ang2), -jnp.sin(ang2)
    g = jnp.concatenate([jnp.concatenate([gr, -gi], axis=2), jnp.concatenate([gi, gr], axis=2)], axis=1)
    bf = jnp.bfloat16
    return f_fwd.astype(bf), f_inv.astype(bf), f_real.astype(bf), g.astype(bf), g.transpose(0, 2, 1).astype(bf)


def _lmul_pair_kernel(l_ref, xr_ref, xi_ref, o_ref):
    x = jnp.concatenate([xr_ref[0], xi_ref[0]], axis=0).astype(jnp.bfloat16)
    o_ref[0] = jnp.dot(l_ref[...], x, preferred_element_type=jnp.float32).astype(o_ref.dtype)


def _lmul_kernel(l_ref, x_ref, o_ref):
    o_ref[0] = jnp.dot(l_ref[...], x_ref[0].astype(jnp.bfloat16),
                       preferred_element_type=jnp.float32).astype(o_ref.dtype)


def _lmul_gate_kernel(l_ref, a_ref, zr_ref, zi_ref, pr_ref, pi_ref, b_ref, o_ref):
    y = jnp.dot(l_ref[...], a_ref[0], preferred_element_type=jnp.float32)
    h = FFT_N1 // 2
    o_ref[0, :h] = pr_ref[0] * (y[:h] + zr_ref[0] * b_ref[...])
    o_ref[0, h:] = pi_ref[0] * (y[h:] + zi_ref[0] * b_ref[...])


def _col_params():
    return pltpu.CompilerParams(dimension_semantics=("arbitrary", "arbitrary"), vmem_limit_bytes=VMEM_LIMIT_BYTES)


def dft_stage1_pairs(f_fwd, z):
    nb, h, cols = z.shape
    npair = (nb + 1) // 2
    last = nb - 1
    return pl.pallas_call(
        _lmul_pair_kernel,
        grid=(npair, cols // LMUL_TN),
        in_specs=[
            pl.BlockSpec(f_fwd.shape, lambda p, j: (0, 0)),
            pl.BlockSpec((1, h, LMUL_TN), lambda p, j: (jnp.minimum(2 * p, last), 0, j)),
            pl.BlockSpec((1, h, LMUL_TN), lambda p, j: (jnp.minimum(2 * p + 1, last), 0, j)),
        ],
        out_specs=pl.BlockSpec((1, 2 * FFT_N1, LMUL_TN), lambda p, j: (p, 0, j)),
        out_shape=jax.ShapeDtypeStruct((npair, 2 * FFT_N1, cols), jnp.bfloat16),
        compiler_params=_col_params(),
        name="dft_stage1",
    )(f_fwd, z, z)


def dft_stage1_real(f_real, x):
    _, h, cols = x.shape
    return pl.pallas_call(
        _lmul_kernel,
        grid=(1, cols // LMUL_TN),
        in_specs=[pl.BlockSpec(f_real.shape, lambda p, j: (0, 0)),
                  pl.BlockSpec((1, h, LMUL_TN), lambda p, j: (p, 0, j))],
        out_specs=pl.BlockSpec((1, 2 * FFT_N1, LMUL_TN), lambda p, j: (p, 0, j)),
        out_shape=jax.ShapeDtypeStruct((1, 2 * FFT_N1, cols), jnp.bfloat16),
        compiler_params=_col_params(),
        name="dft_stage1_real",
    )(f_real, x)


def idft_stage1_gate(f_inv, a, z, p, bias_row):
    npair, _, cols = a.shape
    half = FFT_N1 // 2

    def half_blk(x, r):
        last = x.shape[0] - 1
        return pl.BlockSpec((1, half, LMUL_TN), lambda q, j: (jnp.minimum(2 * q + r, last), 0, j))

    out = pl.pallas_call(
        _lmul_gate_kernel,
        grid=(npair, cols // LMUL_TN),
        in_specs=[pl.BlockSpec(f_inv.shape, lambda q, j: (0, 0)),
                  pl.BlockSpec((1, 2 * FFT_N1, LMUL_TN), lambda q, j: (q, 0, j)),
                  half_blk(z, 0), half_blk(z, 1), half_blk(p, 0), half_blk(p, 1),
                  pl.BlockSpec((1, LMUL_TN), lambda q, j: (0, 0))],
        out_specs=pl.BlockSpec((1, FFT_N1, LMUL_TN), lambda q, j: (q, 0, j)),
        out_shape=jax.ShapeDtypeStruct((npair, FFT_N1, cols), jnp.float32),
        compiler_params=_col_params(),
        name="idft_stage1_gate",
    )(f_inv, a, z, z, p, p, bias_row)
    return out.reshape(2 * npair, half, cols)


def _stage2_conv_kernel(a_ref, g_ref, gt_ref, h_ref, o_ref):
    for j in range(KB_K1):
        x = jnp.concatenate([a_ref[0, 0, j], a_ref[0, 1, j]], axis=0)
        z = jnp.dot(g_ref[j], x, preferred_element_type=jnp.float32)
        zr, zi = z[:FFT_N2], z[FFT_N2:]
        hr, hi = h_ref[0, j], h_ref[1, j]
        w = jnp.concatenate([zr * hr - zi * hi, zr * hi + zi * hr], axis=0).astype(jnp.bfloat16)
        y = jnp.dot(gt_ref[j], w, preferred_element_type=jnp.float32)
        o_ref[0, 0, j] = y[:FFT_N2].astype(o_ref.dtype)
        o_ref[0, 1, j] = y[FFT_N2:].astype(o_ref.dtype)


def stage2_conv(a, g, gt, hf, order):
    npair = a.shape[0]
    C = HYENA_WIDTH
    ablk = pl.BlockSpec((1, 2, KB_K1, FFT_N2, C), lambda p, i: (p, 0, i, 0, 0))
    gblk = pl.BlockSpec((KB_K1, 2 * FFT_N2, 2 * FFT_N2), lambda p, i: (i, 0, 0))
    return pl.pallas_call(
        _stage2_conv_kernel,
        grid=(npair, FFT_N1 // KB_K1),
        in_specs=[ablk, gblk, gblk,
                  pl.BlockSpec((2, KB_K1, FFT_N2, C), lambda p, i: (0, i, 0, order))],
        out_specs=ablk,
        out_shape=jax.ShapeDtypeStruct(a.shape, a.dtype),
        compiler_params=_col_params(),
        name="stage2_conv",
    )(a, g, gt, hf)


def _stage2_filter_kernel(a_ref, g_ref, s_ref, o_ref):
    for j in range(KB_K1):
        x = jnp.concatenate([a_ref[0, j], a_ref[1, j]], axis=0)
        z = jnp.dot(g_ref[j], x, preferred_element_type=jnp.float32) * s_ref[...]
        o_ref[0, j] = z[:FFT_N2]
        o_ref[1, j] = z[FFT_N2:]


def stage2_filter(a, g, inv_norm):
    W = a.shape[-1]
    blk = pl.BlockSpec((2, KB_K1, FFT_N2, W), lambda i: (0, i, 0, 0))
    return pl.pallas_call(
        _stage2_filter_kernel,
        grid=(FFT_N1 // KB_K1,),
        in_specs=[blk, pl.BlockSpec((KB_K1, 2 * FFT_N2, 2 * FFT_N2), lambda i: (i, 0, 0)),
                  pl.BlockSpec((1, W), lambda i: (0, 0))],
        out_specs=blk,
        out_shape=jax.ShapeDtypeStruct(a.shape, jnp.float32),
        compiler_params=pltpu.CompilerParams(dimension_semantics=("arbitrary",), vmem_limit_bytes=VMEM_LIMIT_BYTES),
        name="stage2_filter",
    )(a, g, inv_norm)


def _filter_gen_kernel(bands_ref, w0_ref, wc_ref, ws_ref, wmid_ref, fb_ref, ff_ref, wdir_ref, wbwd_ref,
                       adelta_ref, two_ref, norm_ref, *, seq):
    i = pl.program_id(0)
    bf = jnp.bfloat16
    row = i * FILT_TM + lax.broadcasted_iota(jnp.int32, (FILT_TM, 1), 0)
    pos_i = jnp.where(row < seq, row, 2 * seq - row)
    pos = pos_i.astype(jnp.float32)
    t = pos / max(seq - 1, 1)
    w = (jnp.float32(2.0 * math.pi) * pos) / seq
    ang = w * bands_ref[...]
    pre = (jnp.dot(jnp.cos(ang).astype(bf), wc_ref[...], preferred_element_type=jnp.float32)
           + jnp.dot((-jnp.sin(ang)).astype(bf), ws_ref[...], preferred_element_type=jnp.float32)
           + t.astype(bf).astype(jnp.float32) * w0_ref[...])
    h = jnp.sin(ff_ref[0:1, :] * (pre + fb_ref[0:1, :]))
    for m in range(FILTER_INNER):
        pre = jnp.dot(h.astype(bf), wmid_ref[m], preferred_element_type=jnp.float32)
        h = jnp.sin(ff_ref[m + 1:m + 2, :] * (pre + fb_ref[m + 1:m + 2, :]))
    hb = h.astype(bf)
    decay = jnp.exp(-t * adelta_ref[...])
    out = jnp.dot(hb, wdir_ref[0], preferred_element_type=jnp.float32) * decay
    out = jnp.where(pos_i < seq, out, 0.0)

    @pl.when(i == 0)
    def _():
        extra = jnp.dot(hb, wbwd_ref[...], preferred_element_type=jnp.float32) * decay
        first = jnp.where(row == 0, out + extra, out)
        two_ref[...] = first
        norm_ref[...] = jnp.sum(jnp.abs(first), axis=0, keepdims=True)

    @pl.when(i != 0)
    def _():
        two_ref[...] = out
        norm_ref[...] += jnp.sum(jnp.abs(out), axis=0, keepdims=True)


def hyena_filter_rows(seq, f_w_in, f_w_mid, f_b, f_freq, f_w_out):
    f32, bf = jnp.float32, jnp.bfloat16
    C, H = HYENA_WIDTH, FILTER_HIDDEN
    bands = jnp.linspace(1e-4, FILTER_BANDS - 1, FILTER_BANDS, dtype=f32)
    bands = jnp.pad(bands, (0, VREG_LANES - FILTER_BANDS)).reshape(1, VREG_LANES)
    w_in = f_w_in.astype(bf)
    w0 = w_in[0:1].astype(f32)
    pad_rows = ((0, VREG_LANES - FILTER_BANDS), (0, 0))
    wc = jnp.pad(w_in[1:1 + FILTER_BANDS], pad_rows)
    ws = jnp.pad(w_in[1 + FILTER_BANDS:], pad_rows)
    w_out = f_w_out.astype(bf).reshape(H, HYENA_ORDER, 2, C)
    wdir = jnp.stack([w_out[:, :, 0].reshape(H, HYENA_ORDER * C), w_out[:, :, 1].reshape(H, HYENA_ORDER * C)])
    max_decay = math.log(DECAY_TARGET) / FAST_DECAY_PCT
    min_decay = math.log(DECAY_TARGET) / SLOW_DECAY_PCT
    adelta = jnp.abs(jnp.linspace(min_decay, max_decay, C, dtype=f32))
    adelta = jnp.tile(adelta, HYENA_ORDER).reshape(1, HYENA_ORDER * C)
    n_tiles = 2 * seq // FILT_TM
    half_tiles = seq // FILT_TM
    whole = lambda a: pl.BlockSpec(a.shape, lambda i: (0,) * a.ndim)
    wmid = f_w_mid.astype(bf)
    fb, ff = f_b.astype(f32), f_freq.astype(f32)
    return pl.pallas_call(
        functools.partial(_filter_gen_kernel, seq=seq),
        grid=(n_tiles,),
        in_specs=[whole(bands), whole(w0), whole(wc), whole(ws), whole(wmid), whole(fb), whole(ff),
                  pl.BlockSpec((1, H, HYENA_ORDER * C), lambda i: (i // half_tiles, 0, 0)),
                  pl.BlockSpec((None, H, HYENA_ORDER * C), lambda i: (1, 0, 0)),
                  whole(adelta)],
        out_specs=[pl.BlockSpec((FILT_TM, HYENA_ORDER * C), lambda i: (i, 0)),
                   pl.BlockSpec((1, HYENA_ORDER * C), lambda i: (0, 0))],
        out_shape=[jax.ShapeDtypeStruct((2 * seq, HYENA_ORDER * C), f32),
                   jax.ShapeDtypeStruct((1, HYENA_ORDER * C), f32)],
        compiler_params=pltpu.CompilerParams(dimension_semantics=("arbitrary",), vmem_limit_bytes=VMEM_LIMIT_BYTES),
        name="hyena_filter_gen",
    )(bands, w0, wc, ws, wmid, fb, ff, wdir, wdir, adelta)


def _short_conv_kernel(x_ref, w_ref, b_ref, o_ref):
    x = x_ref[0]
    n = x.shape[0]
    t = lax.broadcasted_iota(jnp.int32, x.shape, 0)
    prev = jnp.where(t == 0, 0.0, pltpu.roll(x, 1, 0))
    nxt = jnp.where(t == n - 1, 0.0, pltpu.roll(x, n - 1, 0))
    o_ref[0] = ((b_ref[...] + prev * w_ref[0:1, :]) + x * w_ref[1:2, :]) + nxt * w_ref[2:3, :]


def short_conv_pallas(u, w, b):
    B, L, W = u.shape
    tc = VREG_LANES
    return pl.pallas_call(
        _short_conv_kernel,
        grid=(B, W // tc),
        in_specs=[pl.BlockSpec((1, L, tc), lambda i, j: (i, 0, j)),
                  pl.BlockSpec((SHORT_CONV, tc), lambda i, j: (0, j)),
                  pl.BlockSpec((1, tc), lambda i, j: (0, j))],
        out_specs=pl.BlockSpec((1, L, tc), lambda i, j: (i, 0, j)),
        out_shape=jax.ShapeDtypeStruct((B, L, W), jnp.float32),
        compiler_params=_col_params(),
        name="short_conv",
    )(u, w, b.reshape(1, W))


def hyena_mixer_pallas(u, conv_w, conv_b, f_w_in, f_w_mid, f_b, f_freq, f_w_out, hyena_bias):
    B, L, _ = u.shape
    assert 2 * L == FFT_N
    C = HYENA_WIDTH
    f_fwd, f_inv, f_real, g, gt = _dft_constants()
    two, norm = hyena_filter_rows(L, f_w_in, f_w_mid, f_b, f_freq, f_w_out)
    af = dft_stage1_real(f_real, two.reshape(1, FFT_N1, FFT_N2 * HYENA_ORDER * C))
    af = af.reshape(2, FFT_N1, FFT_N2, HYENA_ORDER * C)
    hf = stage2_filter(af, g, 1.0 / norm)
    uc = short_conv_pallas(u, conv_w, conv_b)
    half = FFT_N1 // 2
    parts = [uc[..., o * C:(o + 1) * C].reshape(B, half, HY_COLS) for o in range(HYENA_ORDER + 1)]
    z = parts[0]
    for o in range(HYENA_ORDER):
        a = dft_stage1_pairs(f_fwd, z)
        npair = a.shape[0]
        a = stage2_conv(a.reshape(npair, 2, FFT_N1, FFT_N2, C), g, gt, hf, o)
        bias_row = jnp.tile(hyena_bias[o], LMUL_TN // C).reshape(1, LMUL_TN)
        z = idft_stage1_gate(f_inv, a.reshape(npair, 2 * FFT_N1, HY_COLS), z, parts[o + 1], bias_row)
    return z[:B].reshape(B, L, C)


def memory_attention(q, mem, w_mem_kv):
    B, S = q.shape[0], q.shape[1]
    M = mem.shape[1]
    kv = mem @ w_mem_kv
    k = kv[..., :MEM_WIDTH].reshape(B, M, MEM_HEADS, MEM_HEAD_DIM)
    v = kv[..., MEM_WIDTH:].reshape(B, M, MEM_HEADS, MEM_HEAD_DIM)
    qh = q.reshape(B, S, MEM_HEADS, MEM_HEAD_DIM)
    s = jnp.einsum('bshd,bmhd->bhsm', qh, k).astype(jnp.float32) * (MEM_HEAD_DIM ** -0.5)
    p = jax.nn.softmax(s, axis=-1).astype(v.dtype)
    o = jnp.einsum('bhsm,bmhd->bshd', p, v)
    return o.reshape(B, S, MEM_WIDTH)


PEER_SEL = PEER_HEADS * PEER_TOPK
PEER_TB = 64
VREG_SUBLANES = 8
VREG_LANES = 128
EXPERT_ROWS = 4
_COMBINE_POS = (3, 7, 1, 5, 2, 6, 0, 4)


def pack_expert_table(tab):
    e = tab.shape[0]
    b = lax.bitcast_convert_type(tab.astype(jnp.bfloat16), jnp.uint16).astype(jnp.uint32)
    b = b.reshape(e, 2, EXPERT_ROWS, VREG_LANES)
    w = (b[:, 0] | (b[:, 1] << 16)).reshape(e * EXPERT_ROWS, VREG_LANES)
    return jnp.pad(w, ((0, VREG_SUBLANES), (0, 0)))


def _expert_halves(tab_ref, row):
    w = tab_ref[pl.ds(row, VREG_SUBLANES), :]
    lo = lax.bitcast_convert_type(w << 16, jnp.float32)
    hi = lax.bitcast_convert_type(w & jnp.uint32(0xFFFF0000), jnp.float32)
    return lo, hi


def _sublane_sums(ps, sub):
    lo4 = (sub & 4) == 0
    c = []
    for a, b in ((ps[0], ps[1]), (ps[2], ps[3]), (ps[4], ps[5]), (ps[6], ps[7])):
        c.append(jnp.where(lo4, a, pltpu.roll(b, 4, 0)))
    mv = (sub & 2) != 0
    e = []
    for c1, c2 in ((c[0], c[1]), (c[2], c[3])):
        e.append(jnp.where(mv, c1 + pltpu.roll(c1, 2, 0), c2 + pltpu.roll(c2, 6, 0)))
    mo = (sub & 1) != 0
    return jnp.where(mo, e[0] + pltpu.roll(e[0], 1, 0), e[1] + pltpu.roll(e[1], 7, 0))


def _gelu_exact(x):
    return 0.5 * x * (1.0 + lax.erf(x * (2.0 ** -0.5)))


def _peer_u_kernel(idx_ref, x_ref, g_ref, tab_ref, o_ref, act_ref, rhi_ref, rlo_ref):
    sub = lax.broadcasted_iota(jnp.int32, (VREG_SUBLANES, VREG_LANES), 0)
    ones = jnp.ones((VREG_SUBLANES, VREG_LANES), jnp.bfloat16)
    nt = (((1,), (1,)), ((), ()))

    def token(t, carry):
        x_lo = x_ref[t]
        x_hi = pltpu.roll(x_lo, EXPERT_ROWS, 0)
        rs = []
        for grp in range(PEER_SEL // VREG_SUBLANES):
            ps = [None] * VREG_SUBLANES
            for q in range(VREG_SUBLANES):
                lo, hi = _expert_halves(tab_ref, idx_ref[t, grp * VREG_SUBLANES + _COMBINE_POS[q]])
                ps[q] = lo * x_lo + hi * x_hi
            rs.append(_sublane_sums(ps, sub))
        r = jnp.concatenate(rs, axis=0)
        r_hi = r.astype(jnp.bfloat16)
        r_lo = (r - r_hi.astype(jnp.float32)).astype(jnp.bfloat16)
        r0 = pl.multiple_of(t * PEER_SEL, PEER_SEL)
        rhi_ref[pl.ds(r0, PEER_SEL), :] = r_hi
        rlo_ref[pl.ds(r0, PEER_SEL), :] = r_lo
        return carry

    lax.fori_loop(0, PEER_TB, token, 0)
    chunk = VREG_SUBLANES * PEER_SEL
    for c in range(PEER_TB // VREG_SUBLANES):
        s = (lax.dot_general(ones, rhi_ref[c * chunk:(c + 1) * chunk, :], nt, preferred_element_type=jnp.float32)
             + lax.dot_general(ones, rlo_ref[c * chunk:(c + 1) * chunk, :], nt, preferred_element_type=jnp.float32))
        for j in range(VREG_SUBLANES):
            act_ref[c * VREG_SUBLANES + j:c * VREG_SUBLANES + j + 1, :] = s[0:1, j * PEER_SEL:(j + 1) * PEER_SEL]
    o_ref[...] = g_ref[...] * _gelu_exact(act_ref[...])


def _peer_v_kernel(idx_ref, coef_ref, tab_ref, o_ref):
    n_acc = 2
    sub = lax.broadcasted_iota(jnp.int32, (VREG_SUBLANES, VREG_LANES), 0)

    def token(t, carry):
        zero = jnp.zeros((VREG_SUBLANES, VREG_LANES), jnp.float32)
        acc_lo = [zero] * n_acc
        acc_hi = [zero] * n_acc
        for k in range(PEER_SEL):
            lo, hi = _expert_halves(tab_ref, idx_ref[t, k])
            c = coef_ref[t, k]
            acc_lo[k % n_acc] = acc_lo[k % n_acc] + lo * c
            acc_hi[k % n_acc] = acc_hi[k % n_acc] + hi * c
        o_ref[t] = jnp.where(sub < EXPERT_ROWS, acc_lo[0] + acc_lo[1],
                             pltpu.roll(acc_hi[0] + acc_hi[1], EXPERT_ROWS, 0))
        return carry

    lax.fori_loop(0, PEER_TB, token, 0)


def peer_experts(x, eidx, g, tab_u, tab_v):
    T = x.shape[0]
    grid = (T // PEER_TB,)
    smem_blk = pl.BlockSpec((PEER_TB, PEER_SEL), lambda i: (i, 0), memory_space=pltpu.SMEM)
    vec_blk = pl.BlockSpec((PEER_TB, PEER_SEL), lambda i: (i, 0))
    row_blk = pl.BlockSpec((PEER_TB, VREG_SUBLANES, VREG_LANES), lambda i: (i, 0, 0))
    tab_spec = pl.BlockSpec(memory_space=pltpu.VMEM)
    params = pltpu.CompilerParams(dimension_semantics=("arbitrary",), vmem_limit_bytes=VMEM_LIMIT_BYTES)
    coef = pl.pallas_call(
        _peer_u_kernel,
        grid=grid,
        in_specs=[smem_blk, row_blk, vec_blk, tab_spec],
        out_specs=vec_blk,
        out_shape=jax.ShapeDtypeStruct((T, PEER_SEL), jnp.float32),
        scratch_shapes=[pltpu.VMEM((PEER_TB, PEER_SEL), jnp.float32),
                        pltpu.VMEM((PEER_TB * PEER_SEL, VREG_LANES), jnp.bfloat16),
                        pltpu.VMEM((PEER_TB * PEER_SEL, VREG_LANES), jnp.bfloat16)],
        compiler_params=params,
        name="peer_u",
    )(eidx, x.reshape(T, VREG_SUBLANES, VREG_LANES), g, tab_u)
    out = pl.pallas_call(
        _peer_v_kernel,
        grid=grid,
        in_specs=[smem_blk, smem_blk, tab_spec],
        out_specs=row_blk,
        out_shape=jax.ShapeDtypeStruct((T, VREG_SUBLANES, VREG_LANES), jnp.float32),
        compiler_params=params,
        name="peer_v",
    )(eidx, coef, tab_v)
    return out.reshape(T, D_MODEL)


ROUTE_TM = 256
_NT_DIMS = (((1,), (1,)), ((), ()))


def _top16_rows(s, key_id):
    row16 = lax.broadcasted_iota(jnp.int32, (PEER_TOPK, VREG_LANES), 0)
    vals = jnp.zeros((PEER_TOPK, VREG_LANES), jnp.float32)
    ids = jnp.zeros((PEER_TOPK, VREG_LANES), jnp.int32)
    big = jnp.int32(2 ** 30)
    for j in range(PEER_TOPK):
        m = jnp.max(s, axis=0, keepdims=True)
        am = jnp.min(jnp.where(s == m, key_id, big), axis=0, keepdims=True)
        vals = jnp.where(row16 == j, m, vals)
        ids = jnp.where(row16 == j, am, ids)
        s = jnp.where(key_id == am, -jnp.inf, s)
    return vals, ids


_PAIR_GROUPS = ((0, 0, 8), (0, 8, 8), (1, 0, 8), (2, 0, 5), (3, 0, 4), (4, 0, 3), (5, 0, 2), (6, 0, 2), (7, 0, 2))


def _route_head(s0, i0, s1, i1):
    sub = lax.broadcasted_iota(jnp.int32, (VREG_SUBLANES, VREG_LANES), 0)
    cands, flat, eids = [], [], []
    for a, b0, nb in _PAIR_GROUPS:
        c = s0[a:a + 1] + s1[b0:b0 + VREG_SUBLANES]
        cands.append(jnp.where(sub < nb, c, -jnp.inf) if nb < VREG_SUBLANES else c)
        flat.append(a * PEER_TOPK + b0 + sub)
        eids.append(i0[a:a + 1] * N_KEYS + i1[b0:b0 + VREG_SUBLANES])
    cands.append(s0[VREG_SUBLANES:] + s1[0:1])
    flat.append((sub + VREG_SUBLANES) * PEER_TOPK)
    eids.append(i0[VREG_SUBLANES:] * N_KEYS + i1[0:1])
    cand = jnp.concatenate(cands, axis=0)
    flat = jnp.concatenate(flat, axis=0)
    eid = jnp.concatenate(eids, axis=0)
    row16 = lax.broadcasted_iota(jnp.int32, (PEER_TOPK, VREG_LANES), 0)
    sc = jnp.zeros((PEER_TOPK, VREG_LANES), jnp.float32)
    sel = jnp.zeros((PEER_TOPK, VREG_LANES), jnp.int32)
    big = jnp.int32(2 ** 30)
    for j in range(PEER_TOPK):
        m = jnp.max(cand, axis=0, keepdims=True)
        am = jnp.min(jnp.where(cand == m, flat, big), axis=0, keepdims=True)
        hit = flat == am
        e = jnp.max(jnp.where(hit, eid, -1), axis=0, keepdims=True)
        sc = jnp.where(row16 == j, m, sc)
        sel = jnp.where(row16 == j, e, sel)
        cand = jnp.where(hit, -jnp.inf, cand)
    p = jnp.exp(sc - sc[0:1])
    return sel, p / jnp.sum(p, axis=0, keepdims=True)


def _route_kernel(x_ref, wq_ref, sk_ref, rows_ref, g_ref, q_ref):
    q = jnp.dot(x_ref[...].astype(jnp.bfloat16), wq_ref[...], preferred_element_type=jnp.float32)
    q_ref[...] = q.astype(jnp.bfloat16)
    key_id = lax.broadcasted_iota(jnp.int32, (N_KEYS, VREG_LANES), 0)

    def head(h, carry):
        tops = []
        for c in range(2):
            hc = h * 2 + c
            qhc = q_ref[:, pl.ds(pl.multiple_of(hc * PEER_HALF, PEER_HALF), PEER_HALF)]
            s = lax.dot_general(sk_ref[hc], qhc, _NT_DIMS, preferred_element_type=jnp.float32)
            tops.append([_top16_rows(s[:, j * VREG_LANES:(j + 1) * VREG_LANES], key_id)
                         for j in range(ROUTE_TM // VREG_LANES)])
        r0 = pl.multiple_of(h * PEER_TOPK, PEER_TOPK)
        for j in range(ROUTE_TM // VREG_LANES):
            (s0, i0), (s1, i1) = tops[0][j], tops[1][j]
            sel, g = _route_head(s0, i0, s1, i1)
            rows_ref[pl.ds(r0, PEER_TOPK), j * VREG_LANES:(j + 1) * VREG_LANES] = sel * EXPERT_ROWS
            g_ref[pl.ds(r0, PEER_TOPK), j * VREG_LANES:(j + 1) * VREG_LANES] = g
        return carry

    lax.fori_loop(0, PEER_HEADS, head, 0)


def peer_route(x, w_query, sub_keys):
    T = x.shape[0]
    wq = w_query.astype(jnp.bfloat16)
    sk = sub_keys.reshape(PEER_HEADS * 2, N_KEYS, PEER_HALF).astype(jnp.bfloat16)
    out_blk = pl.BlockSpec((PEER_SEL, ROUTE_TM), lambda i: (0, i))
    rows, g = pl.pallas_call(
        _route_kernel,
        grid=(T // ROUTE_TM,),
        in_specs=[
            pl.BlockSpec((ROUTE_TM, D_MODEL), lambda i: (i, 0)),
            pl.BlockSpec(wq.shape, lambda i: (0, 0)),
            pl.BlockSpec(sk.shape, lambda i: (0, 0, 0)),
        ],
        out_specs=[out_blk, out_blk],
        out_shape=[jax.ShapeDtypeStruct((PEER_SEL, T), jnp.int32),
                   jax.ShapeDtypeStruct((PEER_SEL, T), jnp.float32)],
        scratch_shapes=[pltpu.VMEM((ROUTE_TM, PEER_HEADS * PEER_QDIM), jnp.bfloat16)],
        compiler_params=pltpu.CompilerParams(dimension_semantics=("arbitrary",),
                                             vmem_limit_bytes=VMEM_LIMIT_BYTES),
        name="peer_route",
    )(x, wq, sk)
    return rows.T, g.T


def peer_ffn(x, w_query, sub_keys, tab_u, tab_v):
    B, S, D = x.shape
    xf = x.reshape(B * S, D)
    eidx, g = peer_route(xf, w_query, sub_keys)
    return peer_experts(xf, eidx, g, tab_u, tab_v).reshape(B, S, D)


def encoder_layer(x, mem, rel_bias, w_in, b_in, conv_w, conv_b, attn_sink, f_w_in, f_w_mid, f_b, f_freq,
                  f_w_out, hyena_bias, w_mem_kv, w_branch, w_out, ln1_g, ln1_b, w_query, sub_keys,
                  expert_u, expert_v, ln2_g, ln2_b):
    B, S, _ = x.shape
    proj = linear(x.reshape(B * S, D_MODEL), w_in, b_in, tm=512, tn=IN_WIDTH // 2).reshape(B, S, IN_WIDTH)
    q_a = proj[..., :O_K].reshape(B, S, ATT_HEADS, ATT_HEAD_DIM)
    k_a = proj[..., O_K:O_V].reshape(B, S, ATT_KV_HEADS, ATT_HEAD_DIM)
    v_a = proj[..., O_V:O_HY].reshape(B, S, ATT_KV_HEADS, ATT_HEAD_DIM)
    hy = proj[..., O_HY:O_MQ]
    q_m = proj[..., O_MQ:O_GATE]
    gates = jax.nn.sigmoid(proj[..., O_GATE:].reshape(B, S, N_BRANCH, D_MODEL))
    branches = (
        window_attention(q_a, k_a, v_a, rel_bias, attn_sink),
        hyena_mixer_pallas(hy, conv_w, conv_b, f_w_in, f_w_mid, f_b, f_freq, f_w_out, hyena_bias),
        memory_attention(q_m, mem, w_mem_kv),
    )
    merged = gates[:, :, 0] * (branches[0] @ w_branch[0])
    for n in range(1, N_BRANCH):
        merged = merged + gates[:, :, n] * (branches[n] @ w_branch[n])
    x = layer_norm(ALPHA * x + merged @ w_out, ln1_g, ln1_b)
    x = layer_norm(ALPHA * x + peer_ffn(x, w_query, sub_keys, pack_expert_table(expert_u), pack_expert_table(expert_v)), ln2_g, ln2_b)
    return x


def kernel(x_prompt, x_sample, mem_prompt, mem_sample, rel_bias, w_in, b_in, conv_w, conv_b, attn_sink,
           f_w_in, f_w_mid, f_b, f_freq, f_w_out, hyena_bias, w_mem_kv, w_branch, w_out, ln1_g, ln1_b,
           w_query, sub_keys, expert_u, expert_v, ln2_g, ln2_b):
    nb = x_prompt.shape[0]
    x = jnp.concatenate([x_prompt, x_sample], axis=0)
    mem = jnp.concatenate([mem_prompt, mem_sample], axis=0)
    for l in range(DEPTH):
        x = encoder_layer(x, mem, rel_bias, w_in[l], b_in[l], conv_w[l], conv_b[l], attn_sink[l],
                          f_w_in[l], f_w_mid[l], f_b[l], f_freq[l], f_w_out[l], hyena_bias[l],
                          w_mem_kv[l], w_branch[l], w_out[l], ln1_g[l], ln1_b[l], w_query[l],
                          sub_keys[l], expert_u[l], expert_v[l], ln2_g[l], ln2_b[l])
    return (x[:nb], x[nb:])
```

```python
import functools
import math

import jax
import jax.numpy as jnp
from jax import lax
from jax.experimental import pallas as pl
from jax.experimental.pallas import tpu as pltpu

D_MODEL = 1024
DEPTH = 2
N_MEM = 256
ATT_HEADS = 8
ATT_KV_HEADS = 2
ATT_HEAD_DIM = 64
ATT_WIDTH = ATT_HEADS * ATT_HEAD_DIM
ATT_KV_WIDTH = ATT_KV_HEADS * ATT_HEAD_DIM
WINDOW = 128
BLOCK = 128
N_BUCKETS = 32
MAX_DISTANCE = 128
HYENA_WIDTH = 512
HYENA_ORDER = 2
SHORT_CONV = 3
FILTER_EMB = 33
FILTER_BANDS = (FILTER_EMB - 1) // 2
FILTER_HIDDEN = 64
FILTER_INNER = 2
FAST_DECAY_PCT = 0.3
SLOW_DECAY_PCT = 1.5
DECAY_TARGET = 1e-2
MEM_HEADS = 4
MEM_HEAD_DIM = 128
MEM_WIDTH = MEM_HEADS * MEM_HEAD_DIM
N_BRANCH = 3
BRANCH_WIDTH = 512
PEER_HEADS = 8
N_KEYS = 128
N_EXPERTS = N_KEYS * N_KEYS
PEER_TOPK = 16
PEER_HALF = 128
PEER_QDIM = 2 * PEER_HALF
PEER_BLOCK = 128
O_K = ATT_WIDTH
O_V = O_K + ATT_KV_WIDTH
O_HY = O_V + ATT_KV_WIDTH
O_MQ = O_HY + (HYENA_ORDER + 1) * HYENA_WIDTH
O_GATE = O_MQ + MEM_WIDTH
IN_WIDTH = O_GATE + N_BRANCH * D_MODEL
ALPHA = (2 * DEPTH) ** 0.25
BETA = (8 * DEPTH) ** -0.25
LN_EPS = 1e-5
NEG_INF = -1e30

VMEM_LIMIT_BYTES = 56 * 1024 * 1024


def _linear_kernel(x_ref, w_ref, b_ref, o_ref, *, sigmoid):
    x = x_ref[...].astype(jnp.bfloat16)
    acc = jnp.dot(x, w_ref[...], preferred_element_type=jnp.float32) + b_ref[...]
    o_ref[...] = jax.nn.sigmoid(acc) if sigmoid else acc


def linear(x, w, b, *, tm=512, tn=None, sigmoid=False):
    T, K = x.shape
    N = w.shape[1]
    tn = N if tn is None else tn
    wb = w.astype(jnp.bfloat16)
    return pl.pallas_call(
        functools.partial(_linear_kernel, sigmoid=sigmoid),
        grid=(N // tn, T // tm),
        in_specs=[
            pl.BlockSpec((tm, K), lambda j, i: (i, 0)),
            pl.BlockSpec((K, tn), lambda j, i: (0, j)),
            pl.BlockSpec((1, tn), lambda j, i: (0, j)),
        ],
        out_specs=pl.BlockSpec((tm, tn), lambda j, i: (i, j)),
        out_shape=jax.ShapeDtypeStruct((T, N), jnp.float32),
        compiler_params=pltpu.CompilerParams(
            dimension_semantics=("arbitrary", "arbitrary"),
            vmem_limit_bytes=VMEM_LIMIT_BYTES),
        name="linear",
    )(x, wb, b.reshape(1, N))


ATT_TQ = 512
ATT_KEYS = 3 * BLOCK
_NT_DIMS = (((1,), (1,)), ((), ()))


def t5_bucket(rel):
    nb = N_BUCKETS // 2
    max_exact = nb // 2
    ret = jnp.where(rel > 0, nb, 0)
    n = jnp.abs(rel)
    nf = jnp.maximum(n, 1).astype(jnp.float32)
    large = max_exact + (jnp.log(nf / max_exact) / math.log(MAX_DISTANCE / max_exact)
                         * (nb - max_exact)).astype(jnp.int32)
    large = jnp.minimum(large, nb - 1)
    return ret + jnp.where(n < max_exact, n, large)


def window_bias_table(rel_bias):
    rel = (jnp.arange(ATT_KEYS) - BLOCK)[None, :] - jnp.arange(BLOCK)[:, None]
    bias = rel_bias[t5_bucket(rel)].astype(jnp.float32).transpose(2, 0, 1)
    return jnp.where((jnp.abs(rel) <= WINDOW)[None], bias, NEG_INF)


def _col_params():
    return pltpu.CompilerParams(dimension_semantics=("arbitrary", "arbitrary"), vmem_limit_bytes=VMEM_LIMIT_BYTES)


def _window_attn_kernel(sink_ref, q_ref, kp_ref, kc_ref, kn_ref, vp_ref, vc_ref, vn_ref, bias_ref, o_ref, *, n_steps):
    i = pl.program_id(1)
    bf = jnp.bfloat16
    q = q_ref[0].astype(bf)
    k_all = jnp.concatenate([kp_ref[0], kc_ref[0], kn_ref[0]], axis=0).astype(bf)
    v_all = jnp.concatenate([vp_ref[0], vc_ref[0], vn_ref[0]], axis=0).astype(bf)
    col = lax.broadcasted_iota(jnp.int32, (BLOCK, ATT_KEYS), 1)
    n_blk = ATT_TQ // BLOCK
    scale = ATT_HEAD_DIM ** -0.5
    group = ATT_HEADS // ATT_KV_HEADS
    for j in range(n_blk):
        kj = k_all[j * BLOCK:j * BLOCK + ATT_KEYS]
        vj = v_all[j * BLOCK:j * BLOCK + ATT_KEYS]
        off_seq = None
        if j == 0:
            off_seq = (i == 0) & (col < BLOCK)
        if j == n_blk - 1:
            last = (i == n_steps - 1) & (col >= 2 * BLOCK)
            off_seq = last if off_seq is None else off_seq | last
        for h in range(ATT_HEADS):
            hk = h // group
            qh = q[j * BLOCK:(j + 1) * BLOCK, h * ATT_HEAD_DIM:(h + 1) * ATT_HEAD_DIM]
            kh = kj[:, hk * ATT_HEAD_DIM:(hk + 1) * ATT_HEAD_DIM]
            s = lax.dot_general(qh, kh, _NT_DIMS, preferred_element_type=jnp.float32) * scale + bias_ref[h]
            if off_seq is not None:
                s = jnp.where(off_seq, NEG_INF, s)
            sink = sink_ref[h]
            m = jnp.maximum(jnp.max(s, axis=-1, keepdims=True), sink)
            p = jnp.exp(s - m)
            denom = jnp.sum(p, axis=-1, keepdims=True) + jnp.exp(sink - m)
            oh = jnp.dot((p / denom).astype(bf), vj[:, hk * ATT_HEAD_DIM:(hk + 1) * ATT_HEAD_DIM],
                         preferred_element_type=jnp.float32)
            o_ref[0, j * BLOCK:(j + 1) * BLOCK, h * ATT_HEAD_DIM:(h + 1) * ATT_HEAD_DIM] = oh.astype(o_ref.dtype)


def window_attention_pallas(qkv, bias, sink):
    B, S, _ = qkv.shape
    n_steps = S // ATT_TQ
    per = ATT_TQ // BLOCK
    last_blk = S // BLOCK - 1
    k_col, v_col = O_K // ATT_KV_WIDTH, O_V // ATT_KV_WIDTH

    def edge(col, nxt):
        if nxt:
            return pl.BlockSpec((1, BLOCK, ATT_KV_WIDTH), lambda b, i: (b, jnp.minimum((i + 1) * per, last_blk), col))
        return pl.BlockSpec((1, BLOCK, ATT_KV_WIDTH), lambda b, i: (b, jnp.maximum(i * per - 1, 0), col))

    cur = lambda col: pl.BlockSpec((1, ATT_TQ, ATT_KV_WIDTH), lambda b, i: (b, i, col))
    return pl.pallas_call(
        functools.partial(_window_attn_kernel, n_steps=n_steps),
        grid=(B, n_steps),
        in_specs=[pl.BlockSpec(memory_space=pltpu.SMEM),
                  pl.BlockSpec((1, ATT_TQ, ATT_WIDTH), lambda b, i: (b, i, 0)),
                  edge(k_col, False), cur(k_col), edge(k_col, True),
                  edge(v_col, False), cur(v_col), edge(v_col, True),
                  pl.BlockSpec(bias.shape, lambda b, i: (0, 0, 0))],
        out_specs=pl.BlockSpec((1, ATT_TQ, ATT_WIDTH), lambda b, i: (b, i, 0)),
        out_shape=jax.ShapeDtypeStruct((B, S, ATT_WIDTH), jnp.bfloat16),
        compiler_params=_col_params(),
        name="window_attn",
    )(sink.astype(jnp.float32), qkv, qkv, qkv, qkv, qkv, qkv, qkv, bias)


MEM_TQ = 512


def _mem_attn_kernel(q_ref, kv_ref, o_ref):
    bf = jnp.bfloat16
    q = q_ref[0].astype(bf)
    kv = kv_ref[0].astype(bf)
    scale = MEM_HEAD_DIM ** -0.5
    for h in range(MEM_HEADS):
        sl = slice(h * MEM_HEAD_DIM, (h + 1) * MEM_HEAD_DIM)
        s = lax.dot_general(q[:, sl], kv[:, sl], _NT_DIMS, preferred_element_type=jnp.float32) * scale
        p = jnp.exp(s - jnp.max(s, axis=-1, keepdims=True))
        p = (p / jnp.sum(p, axis=-1, keepdims=True)).astype(bf)
        vh = kv[:, MEM_WIDTH + h * MEM_HEAD_DIM:MEM_WIDTH + (h + 1) * MEM_HEAD_DIM]
        o_ref[0, :, sl] = jnp.dot(p, vh, preferred_element_type=jnp.float32).astype(o_ref.dtype)


def memory_attention_pallas(q, kv):
    B, S, _ = q.shape
    M = kv.shape[1]
    return pl.pallas_call(
        _mem_attn_kernel,
        grid=(B, S // MEM_TQ),
        in_specs=[pl.BlockSpec((1, MEM_TQ, MEM_WIDTH), lambda b, i: (b, i, 0)),
                  pl.BlockSpec((1, M, 2 * MEM_WIDTH), lambda b, i: (b, 0, 0))],
        out_specs=pl.BlockSpec((1, MEM_TQ, MEM_WIDTH), lambda b, i: (b, i, 0)),
        out_shape=jax.ShapeDtypeStruct((B, S, MEM_WIDTH), jnp.bfloat16),
        compiler_params=_col_params(),
        name="mem_attn",
    )(q, kv)


MERGE_TM = 256


def _layer_norm_rows(y, g, b):
    mu = jnp.mean(y, axis=-1, keepdims=True)
    d = y - mu
    var = jnp.mean(d * d, axis=-1, keepdims=True)
    return d * lax.rsqrt(var + LN_EPS) * g + b


def _merge_kernel(a_ref, h_ref, m_ref, g_ref, x_ref, wb_ref, wo_ref, lg_ref, lb_ref, o_ref):
    bf = jnp.bfloat16
    f32 = jnp.float32
    merged = g_ref[:, 0:D_MODEL] * jnp.dot(a_ref[...].astype(bf), wb_ref[0], preferred_element_type=f32)
    merged = merged + g_ref[:, D_MODEL:2 * D_MODEL] * jnp.dot(h_ref[...].astype(bf), wb_ref[1], preferred_element_type=f32)
    merged = merged + g_ref[:, 2 * D_MODEL:] * jnp.dot(m_ref[...].astype(bf), wb_ref[2], preferred_element_type=f32)
    y = ALPHA * x_ref[...] + jnp.dot(merged.astype(bf), wo_ref[...], preferred_element_type=f32)
    o_ref[...] = _layer_norm_rows(y, lg_ref[...], lb_ref[...])


def merge_norm(att, hy, mem, gates, x, w_branch, w_out, ln_g, ln_b):
    T = x.shape[0]
    bf = jnp.bfloat16
    rows = lambda w: pl.BlockSpec((MERGE_TM, w), lambda i: (i, 0))
    whole = lambda a: pl.BlockSpec(a.shape, lambda i: (0,) * a.ndim)
    wb, wo = w_branch.astype(bf), w_out.astype(bf)
    lg, lb = ln_g.reshape(1, D_MODEL), ln_b.reshape(1, D_MODEL)
    return pl.pallas_call(
        _merge_kernel,
        grid=(T // MERGE_TM,),
        in_specs=[rows(BRANCH_WIDTH), rows(BRANCH_WIDTH), rows(BRANCH_WIDTH), rows(N_BRANCH * D_MODEL), rows(D_MODEL),
                  whole(wb), whole(wo), whole(lg), whole(lb)],
        out_specs=rows(D_MODEL),
        out_shape=jax.ShapeDtypeStruct((T, D_MODEL), jnp.float32),
        compiler_params=pltpu.CompilerParams(dimension_semantics=("arbitrary",), vmem_limit_bytes=VMEM_LIMIT_BYTES),
        name="merge_norm",
    )(att, hy, mem, gates, x, wb, wo, lg, lb)


def _residual_norm_kernel(x_ref, r_ref, lg_ref, lb_ref, o_ref):
    o_ref[...] = _layer_norm_rows(ALPHA * x_ref[...] + r_ref[...], lg_ref[...], lb_ref[...])


def residual_norm(x, r, ln_g, ln_b):
    T = x.shape[0]
    tm = 512
    rows = pl.BlockSpec((tm, D_MODEL), lambda i: (i, 0))
    one = pl.BlockSpec((1, D_MODEL), lambda i: (0, 0))
    return pl.pallas_call(
        _residual_norm_kernel,
        grid=(T // tm,),
        in_specs=[rows, rows, one, one],
        out_specs=rows,
        out_shape=jax.ShapeDtypeStruct((T, D_MODEL), jnp.float32),
        compiler_params=pltpu.CompilerParams(dimension_semantics=("arbitrary",), vmem_limit_bytes=VMEM_LIMIT_BYTES),
        name="residual_norm",
    )(x, r, ln_g.reshape(1, D_MODEL), ln_b.reshape(1, D_MODEL))


FFT_N1 = 256
FFT_N2 = 128
FFT_N = FFT_N1 * FFT_N2
HY_COLS = FFT_N2 * HYENA_WIDTH
LMUL_TN = 2048
KB_K1 = 8
FILT_TM = 512


def _dft_constants():
    f32 = jnp.float32
    n1 = jnp.arange(FFT_N1, dtype=jnp.int32)
    ang1 = ((n1[:, None] * n1[None, :]) % FFT_N1).astype(f32) * f32(2.0 * math.pi / FFT_N1)
    fr, fi = jnp.cos(ang1), -jnp.sin(ang1)
    h = FFT_N1 // 2
    f_fwd = jnp.concatenate([jnp.concatenate([fr[:, :h], -fi[:, :h]], axis=1),
                             jnp.concatenate([fi[:, :h], fr[:, :h]], axis=1)], axis=0)
    f_inv = f_fwd.T * f32(1.0 / FFT_N)
    f_real = jnp.concatenate([fr, fi], axis=0)
    k2 = jnp.arange(FFT_N2, dtype=jnp.int32)
    k = n1[:, None, None] + FFT_N1 * k2[None, :, None]
    ang2 = ((k * k2[None, None, :]) % FFT_N).astype(f32) * f32(2.0 * math.pi / FFT_N)
    gr, gi = jnp.cos(ang2), -jnp.sin(ang2)
    g = jnp.concatenate([jnp.concatenate([gr, -gi], axis=2), jnp.concatenate([gi, gr], axis=2)], axis=1)
    bf = jnp.bfloat16
    return f_fwd.astype(bf), f_inv.astype(bf), f_real.astype(bf), g.astype(bf), g.transpose(0, 2, 1).astype(bf)


def _lmul_pair_kernel(l_ref, xr_ref, xi_ref, o_ref):
    x = jnp.concatenate([xr_ref[0], xi_ref[0]], axis=0).astype(jnp.bfloat16)
    o_ref[0] = jnp.dot(l_ref[...], x, preferred_element_type=jnp.float32).astype(o_ref.dtype)


def _lmul_kernel(l_ref, x_ref, o_ref):
    o_ref[0] = jnp.dot(l_ref[...], x_ref[0].astype(jnp.bfloat16),
                       preferred_element_type=jnp.float32).astype(o_ref.dtype)


def _lmul_gate_kernel(l_ref, a_ref, zr_ref, zi_ref, pr_ref, pi_ref, b_ref, o_ref):
    y = jnp.dot(l_ref[...], a_ref[0], preferred_element_type=jnp.float32)
    h = FFT_N1 // 2
    o_ref[0, :h] = pr_ref[0] * (y[:h] + zr_ref[0] * b_ref[...])
    o_ref[0, h:] = pi_ref[0] * (y[h:] + zi_ref[0] * b_ref[...])


def dft_stage1_pairs(f_fwd, z):
    nb, h, cols = z.shape
    npair = (nb + 1) // 2
    last = nb - 1
    return pl.pallas_call(
        _lmul_pair_kernel,
        grid=(npair, cols // LMUL_TN),
        in_specs=[
            pl.BlockSpec(f_fwd.shape, lambda p, j: (0, 0)),
            pl.BlockSpec((1, h, LMUL_TN), lambda p, j: (jnp.minimum(2 * p, last), 0, j)),
            pl.BlockSpec((1, h, LMUL_TN), lambda p, j: (jnp.minimum(2 * p + 1, last), 0, j)),
        ],
        out_specs=pl.BlockSpec((1, 2 * FFT_N1, LMUL_TN), lambda p, j: (p, 0, j)),
        out_shape=jax.ShapeDtypeStruct((npair, 2 * FFT_N1, cols), jnp.bfloat16),
        compiler_params=_col_params(),
        name="dft_stage1",
    )(f_fwd, z, z)


def dft_stage1_real(f_real, x):
    _, h, cols = x.shape
    return pl.pallas_call(
        _lmul_kernel,
        grid=(1, cols // LMUL_TN),
        in_specs=[pl.BlockSpec(f_real.shape, lambda p, j: (0, 0)),
                  pl.BlockSpec((1, h, LMUL_TN), lambda p, j: (p, 0, j))],
        out_specs=pl.BlockSpec((1, 2 * FFT_N1, LMUL_TN), lambda p, j: (p, 0, j)),
        out_shape=jax.ShapeDtypeStruct((1, 2 * FFT_N1, cols), jnp.bfloat16),
        compiler_params=_col_params(),
        name="dft_stage1_real",
    )(f_real, x)


def idft_stage1_gate(f_inv, a, z, p, bias_row):
    npair, _, cols = a.shape
    half = FFT_N1 // 2

    def half_blk(x, r):
        last = x.shape[0] - 1
        return pl.BlockSpec((1, half, LMUL_TN), lambda q, j: (jnp.minimum(2 * q + r, last), 0, j))

    out = pl.pallas_call(
        _lmul_gate_kernel,
        grid=(npair, cols // LMUL_TN),
        in_specs=[pl.BlockSpec(f_inv.shape, lambda q, j: (0, 0)),
                  pl.BlockSpec((1, 2 * FFT_N1, LMUL_TN), lambda q, j: (q, 0, j)),
                  half_blk(z, 0), half_blk(z, 1), half_blk(p, 0), half_blk(p, 1),
                  pl.BlockSpec((1, LMUL_TN), lambda q, j: (0, 0))],
        out_specs=pl.BlockSpec((1, FFT_N1, LMUL_TN), lambda q, j: (q, 0, j)),
        out_shape=jax.ShapeDtypeStruct((npair, FFT_N1, cols), jnp.float32),
        compiler_params=_col_params(),
        name="idft_stage1_gate",
    )(f_inv, a, z, z, p, p, bias_row)
    return out.reshape(2 * npair, half, cols)


def _stage2_conv_kernel(a_ref, g_ref, gt_ref, h_ref, o_ref):
    for j in range(KB_K1):
        x = jnp.concatenate([a_ref[0, 0, j], a_ref[0, 1, j]], axis=0)
        z = jnp.dot(g_ref[j], x, preferred_element_type=jnp.float32)
        zr, zi = z[:FFT_N2], z[FFT_N2:]
        hr, hi = h_ref[0, j], h_ref[1, j]
        w = jnp.concatenate([zr * hr - zi * hi, zr * hi + zi * hr], axis=0).astype(jnp.bfloat16)
        y = jnp.dot(gt_ref[j], w, preferred_element_type=jnp.float32)
        o_ref[0, 0, j] = y[:FFT_N2].astype(o_ref.dtype)
        o_ref[0, 1, j] = y[FFT_N2:].astype(o_ref.dtype)


def stage2_conv(a, g, gt, hf, order):
    npair = a.shape[0]
    C = HYENA_WIDTH
    ablk = pl.BlockSpec((1, 2, KB_K1, FFT_N2, C), lambda p, i: (p, 0, i, 0, 0))
    gblk = pl.BlockSpec((KB_K1, 2 * FFT_N2, 2 * FFT_N2), lambda p, i: (i, 0, 0))
    return pl.pallas_call(
        _stage2_conv_kernel,
        grid=(npair, FFT_N1 // KB_K1),
        in_specs=[ablk, gblk, gblk,
                  pl.BlockSpec((2, KB_K1, FFT_N2, C), lambda p, i: (0, i, 0, order))],
        out_specs=ablk,
        out_shape=jax.ShapeDtypeStruct(a.shape, a.dtype),
        compiler_params=_col_params(),
        name="stage2_conv",
    )(a, g, gt, hf)


def _stage2_filter_kernel(a_ref, g_ref, s_ref, o_ref):
    for j in range(KB_K1):
        x = jnp.concatenate([a_ref[0, j], a_ref[1, j]], axis=0)
        z = jnp.dot(g_ref[j], x, preferred_element_type=jnp.float32) * s_ref[...]
        o_ref[0, j] = z[:FFT_N2]
        o_ref[1, j] = z[FFT_N2:]


def stage2_filter(a, g, inv_norm):
    W = a.shape[-1]
    blk = pl.BlockSpec((2, KB_K1, FFT_N2, W), lambda i: (0, i, 0, 0))
    return pl.pallas_call(
        _stage2_filter_kernel,
        grid=(FFT_N1 // KB_K1,),
        in_specs=[blk, pl.BlockSpec((KB_K1, 2 * FFT_N2, 2 * FFT_N2), lambda i: (i, 0, 0)),
                  pl.BlockSpec((1, W), lambda i: (0, 0))],
        out_specs=blk,
        out_shape=jax.ShapeDtypeStruct(a.shape, jnp.float32),
        compiler_params=pltpu.CompilerParams(dimension_semantics=("arbitrary",), vmem_limit_bytes=VMEM_LIMIT_BYTES),
        name="stage2_filter",
    )(a, g, inv_norm)


def _filter_gen_kernel(bands_ref, w0_ref, wc_ref, ws_ref, wmid_ref, fb_ref, ff_ref, wdir_ref, wbwd_ref,
                       adelta_ref, two_ref, norm_ref, *, seq):
    i = pl.program_id(0)
    bf = jnp.bfloat16
    row = i * FILT_TM + lax.broadcasted_iota(jnp.int32, (FILT_TM, 1), 0)
    pos_i = jnp.where(row < seq, row, 2 * seq - row)
    pos = pos_i.astype(jnp.float32)
    t = pos / max(seq - 1, 1)
    w = (jnp.float32(2.0 * math.pi) * pos) / seq
    ang = w * bands_ref[...]
    pre = (jnp.dot(jnp.cos(ang).astype(bf), wc_ref[...], preferred_element_type=jnp.float32)
           + jnp.dot((-jnp.sin(ang)).astype(bf), ws_ref[...], preferred_element_type=jnp.float32)
           + t.astype(bf).astype(jnp.float32) * w0_ref[...])
    h = jnp.sin(ff_ref[0:1, :] * (pre + fb_ref[0:1, :]))
    for m in range(FILTER_INNER):
        pre = jnp.dot(h.astype(bf), wmid_ref[m], preferred_element_type=jnp.float32)
        h = jnp.sin(ff_ref[m + 1:m + 2, :] * (pre + fb_ref[m + 1:m + 2, :]))
    hb = h.astype(bf)
    decay = jnp.exp(-t * adelta_ref[...])
    out = jnp.dot(hb, wdir_ref[0], preferred_element_type=jnp.float32) * decay
    out = jnp.where(pos_i < seq, out, 0.0)

    @pl.when(i == 0)
    def _():
        extra = jnp.dot(hb, wbwd_ref[...], preferred_element_type=jnp.float32) * decay
        first = jnp.where(row == 0, out + extra, out)
        two_ref[...] = first
        norm_ref[...] = jnp.sum(jnp.abs(first), axis=0, keepdims=True)

    @pl.when(i != 0)
    def _():
        two_ref[...] = out
        norm_ref[...] += jnp.sum(jnp.abs(out), axis=0, keepdims=True)


def hyena_filter_rows(seq, f_w_in, f_w_mid, f_b, f_freq, f_w_out):
    f32, bf = jnp.float32, jnp.bfloat16
    C, H = HYENA_WIDTH, FILTER_HIDDEN
    bands = jnp.linspace(1e-4, FILTER_BANDS - 1, FILTER_BANDS, dtype=f32)
    bands = jnp.pad(bands, (0, VREG_LANES - FILTER_BANDS)).reshape(1, VREG_LANES)
    w_in = f_w_in.astype(bf)
    w0 = w_in[0:1].astype(f32)
    pad_rows = ((0, VREG_LANES - FILTER_BANDS), (0, 0))
    wc = jnp.pad(w_in[1:1 + FILTER_BANDS], pad_rows)
    ws = jnp.pad(w_in[1 + FILTER_BANDS:], pad_rows)
    w_out = f_w_out.astype(bf).reshape(H, HYENA_ORDER, 2, C)
    wdir = jnp.stack([w_out[:, :, 0].reshape(H, HYENA_ORDER * C), w_out[:, :, 1].reshape(H, HYENA_ORDER * C)])
    max_decay = math.log(DECAY_TARGET) / FAST_DECAY_PCT
    min_decay = math.log(DECAY_TARGET) / SLOW_DECAY_PCT
    adelta = jnp.abs(jnp.linspace(min_decay, max_decay, C, dtype=f32))
    adelta = jnp.tile(adelta, HYENA_ORDER).reshape(1, HYENA_ORDER * C)
    n_tiles = 2 * seq // FILT_TM
    half_tiles = seq // FILT_TM
    whole = lambda a: pl.BlockSpec(a.shape, lambda i: (0,) * a.ndim)
    wmid = f_w_mid.astype(bf)
    fb, ff = f_b.astype(f32), f_freq.astype(f32)
    return pl.pallas_call(
        functools.partial(_filter_gen_kernel, seq=seq),
        grid=(n_tiles,),
        in_specs=[whole(bands), whole(w0), whole(wc), whole(ws), whole(wmid), whole(fb), whole(ff),
                  pl.BlockSpec((1, H, HYENA_ORDER * C), lambda i: (i // half_tiles, 0, 0)),
                  pl.BlockSpec((None, H, HYENA_ORDER * C), lambda i: (1, 0, 0)),
                  whole(adelta)],
        out_specs=[pl.BlockSpec((FILT_TM, HYENA_ORDER * C), lambda i: (i, 0)),
                   pl.BlockSpec((1, HYENA_ORDER * C), lambda i: (0, 0))],
        out_shape=[jax.ShapeDtypeStruct((2 * seq, HYENA_ORDER * C), f32),
                   jax.ShapeDtypeStruct((1, HYENA_ORDER * C), f32)],
        compiler_params=pltpu.CompilerParams(dimension_semantics=("arbitrary",), vmem_limit_bytes=VMEM_LIMIT_BYTES),
        name="hyena_filter_gen",
    )(bands, w0, wc, ws, wmid, fb, ff, wdir, wdir, adelta)


def _short_conv_kernel(x_ref, w_ref, b_ref, o_ref):
    x = x_ref[0]
    n = x.shape[0]
    t = lax.broadcasted_iota(jnp.int32, x.shape, 0)
    prev = jnp.where(t == 0, 0.0, pltpu.roll(x, 1, 0))
    nxt = jnp.where(t == n - 1, 0.0, pltpu.roll(x, n - 1, 0))
    o_ref[0] = ((b_ref[...] + prev * w_ref[0:1, :]) + x * w_ref[1:2, :]) + nxt * w_ref[2:3, :]


def short_conv_pallas(u, w, b):
    B, L, W = u.shape
    tc = VREG_LANES
    return pl.pallas_call(
        _short_conv_kernel,
        grid=(B, W // tc),
        in_specs=[pl.BlockSpec((1, L, tc), lambda i, j: (i, 0, j)),
                  pl.BlockSpec((SHORT_CONV, tc), lambda i, j: (0, j)),
                  pl.BlockSpec((1, tc), lambda i, j: (0, j))],
        out_specs=pl.BlockSpec((1, L, tc), lambda i, j: (i, 0, j)),
        out_shape=jax.ShapeDtypeStruct((B, L, W), jnp.float32),
        compiler_params=_col_params(),
        name="short_conv",
    )(u, w, b.reshape(1, W))


def hyena_mixer_pallas(u, conv_w, conv_b, f_w_in, f_w_mid, f_b, f_freq, f_w_out, hyena_bias):
    B, L, _ = u.shape
    assert 2 * L == FFT_N
    C = HYENA_WIDTH
    f_fwd, f_inv, f_real, g, gt = _dft_constants()
    two, norm = hyena_filter_rows(L, f_w_in, f_w_mid, f_b, f_freq, f_w_out)
    af = dft_stage1_real(f_real, two.reshape(1, FFT_N1, FFT_N2 * HYENA_ORDER * C))
    af = af.reshape(2, FFT_N1, FFT_N2, HYENA_ORDER * C)
    hf = stage2_filter(af, g, 1.0 / norm)
    uc = short_conv_pallas(u, conv_w, conv_b)
    half = FFT_N1 // 2
    parts = [uc[..., o * C:(o + 1) * C].reshape(B, half, HY_COLS) for o in range(HYENA_ORDER + 1)]
    z = parts[0]
    for o in range(HYENA_ORDER):
        a = dft_stage1_pairs(f_fwd, z)
        npair = a.shape[0]
        a = stage2_conv(a.reshape(npair, 2, FFT_N1, FFT_N2, C), g, gt, hf, o)
        bias_row = jnp.tile(hyena_bias[o], LMUL_TN // C).reshape(1, LMUL_TN)
        z = idft_stage1_gate(f_inv, a.reshape(npair, 2 * FFT_N1, HY_COLS), z, parts[o + 1], bias_row)
    return z[:B].reshape(B, L, C)


PEER_SEL = PEER_HEADS * PEER_TOPK
PEER_TB = 64
VREG_SUBLANES = 8
VREG_LANES = 128
EXPERT_ROWS = 4
_COMBINE_POS = (3, 7, 1, 5, 2, 6, 0, 4)


def pack_expert_table(tab):
    e = tab.shape[0]
    b = lax.bitcast_convert_type(tab.astype(jnp.bfloat16), jnp.uint16).astype(jnp.uint32)
    b = b.reshape(e, 2, EXPERT_ROWS, VREG_LANES)
    w = (b[:, 0] | (b[:, 1] << 16)).reshape(e * EXPERT_ROWS, VREG_LANES)
    return jnp.pad(w, ((0, VREG_SUBLANES), (0, 0)))


def _expert_halves(tab_ref, row):
    w = tab_ref[pl.ds(row, VREG_SUBLANES), :]
    lo = lax.bitcast_convert_type(w << 16, jnp.float32)
    hi = lax.bitcast_convert_type(w & jnp.uint32(0xFFFF0000), jnp.float32)
    return lo, hi


def _sublane_sums(ps, sub):
    lo4 = (sub & 4) == 0
    c = []
    for a, b in ((ps[0], ps[1]), (ps[2], ps[3]), (ps[4], ps[5]), (ps[6], ps[7])):
        c.append(jnp.where(lo4, a, pltpu.roll(b, 4, 0)))
    mv = (sub & 2) != 0
    e = []
    for c1, c2 in ((c[0], c[1]), (c[2], c[3])):
        e.append(jnp.where(mv, c1 + pltpu.roll(c1, 2, 0), c2 + pltpu.roll(c2, 6, 0)))
    mo = (sub & 1) != 0
    return jnp.where(mo, e[0] + pltpu.roll(e[0], 1, 0), e[1] + pltpu.roll(e[1], 7, 0))


def _gelu_exact(x):
    return 0.5 * x * (1.0 + lax.erf(x * (2.0 ** -0.5)))


def _peer_u_kernel(idx_ref, x_ref, g_ref, tab_ref, o_ref, act_ref, rhi_ref, rlo_ref):
    sub = lax.broadcasted_iota(jnp.int32, (VREG_SUBLANES, VREG_LANES), 0)
    ones = jnp.ones((VREG_SUBLANES, VREG_LANES), jnp.bfloat16)
    nt = (((1,), (1,)), ((), ()))

    def token(t, carry):
        x_lo = x_ref[t]
        x_hi = pltpu.roll(x_lo, EXPERT_ROWS, 0)
        rs = []
        for grp in range(PEER_SEL // VREG_SUBLANES):
            ps = [None] * VREG_SUBLANES
            for q in range(VREG_SUBLANES):
                lo, hi = _expert_halves(tab_ref, idx_ref[t, grp * VREG_SUBLANES + _COMBINE_POS[q]])
                ps[q] = lo * x_lo + hi * x_hi
            rs.append(_sublane_sums(ps, sub))
        r = jnp.concatenate(rs, axis=0)
        r_hi = r.astype(jnp.bfloat16)
        r_lo = (r - r_hi.astype(jnp.float32)).astype(jnp.bfloat16)
        r0 = pl.multiple_of(t * PEER_SEL, PEER_SEL)
        rhi_ref[pl.ds(r0, PEER_SEL), :] = r_hi
        rlo_ref[pl.ds(r0, PEER_SEL), :] = r_lo
        return carry

    lax.fori_loop(0, PEER_TB, token, 0)
    chunk = VREG_SUBLANES * PEER_SEL
    for c in range(PEER_TB // VREG_SUBLANES):
        s = (lax.dot_general(ones, rhi_ref[c * chunk:(c + 1) * chunk, :], nt, preferred_element_type=jnp.float32)
             + lax.dot_general(ones, rlo_ref[c * chunk:(c + 1) * chunk, :], nt, preferred_element_type=jnp.float32))
        for j in range(VREG_SUBLANES):
            act_ref[c * VREG_SUBLANES + j:c * VREG_SUBLANES + j + 1, :] = s[0:1, j * PEER_SEL:(j + 1) * PEER_SEL]
    o_ref[...] = g_ref[...] * _gelu_exact(act_ref[...])


def _peer_v_kernel(idx_ref, coef_ref, tab_ref, o_ref):
    n_acc = 2
    sub = lax.broadcasted_iota(jnp.int32, (VREG_SUBLANES, VREG_LANES), 0)

    def token(t, carry):
        zero = jnp.zeros((VREG_SUBLANES, VREG_LANES), jnp.float32)
        acc_lo = [zero] * n_acc
        acc_hi = [zero] * n_acc
        for k in range(PEER_SEL):
            lo, hi = _expert_halves(tab_ref, idx_ref[t, k])
            c = coef_ref[t, k]
            acc_lo[k % n_acc] = acc_lo[k % n_acc] + lo * c
            acc_hi[k % n_acc] = acc_hi[k % n_acc] + hi * c
        o_ref[t] = jnp.where(sub < EXPERT_ROWS, acc_lo[0] + acc_lo[1],
                             pltpu.roll(acc_hi[0] + acc_hi[1], EXPERT_ROWS, 0))
        return carry

    lax.fori_loop(0, PEER_TB, token, 0)


def peer_experts(x, eidx, g, tab_u, tab_v):
    T = x.shape[0]
    grid = (T // PEER_TB,)
    smem_blk = pl.BlockSpec((PEER_TB, PEER_SEL), lambda i: (i, 0), memory_space=pltpu.SMEM)
    vec_blk = pl.BlockSpec((PEER_TB, PEER_SEL), lambda i: (i, 0))
    row_blk = pl.BlockSpec((PEER_TB, VREG_SUBLANES, VREG_LANES), lambda i: (i, 0, 0))
    tab_spec = pl.BlockSpec(memory_space=pltpu.VMEM)
    params = pltpu.CompilerParams(dimension_semantics=("arbitrary",), vmem_limit_bytes=VMEM_LIMIT_BYTES)
    coef = pl.pallas_call(
        _peer_u_kernel,
        grid=grid,
        in_specs=[smem_blk, row_blk, vec_blk, tab_spec],
        out_specs=vec_blk,
        out_shape=jax.ShapeDtypeStruct((T, PEER_SEL), jnp.float32),
        scratch_shapes=[pltpu.VMEM((PEER_TB, PEER_SEL), jnp.float32),
                        pltpu.VMEM((PEER_TB * PEER_SEL, VREG_LANES), jnp.bfloat16),
                        pltpu.VMEM((PEER_TB * PEER_SEL, VREG_LANES), jnp.bfloat16)],
        compiler_params=params,
        name="peer_u",
    )(eidx, x.reshape(T, VREG_SUBLANES, VREG_LANES), g, tab_u)
    out = pl.pallas_call(
        _peer_v_kernel,
        grid=grid,
        in_specs=[smem_blk, smem_blk, tab_spec],
        out_specs=row_blk,
        out_shape=jax.ShapeDtypeStruct((T, VREG_SUBLANES, VREG_LANES), jnp.float32),
        compiler_params=params,
        name="peer_v",
    )(eidx, coef, tab_v)
    return out.reshape(T, D_MODEL)


ROUTE_TM = 256


def _top16_rows(s, key_id):
    row16 = lax.broadcasted_iota(jnp.int32, (PEER_TOPK, VREG_LANES), 0)
    vals = jnp.zeros((PEER_TOPK, VREG_LANES), jnp.float32)
    ids = jnp.zeros((PEER_TOPK, VREG_LANES), jnp.int32)
    big = jnp.int32(2 ** 30)
    for j in range(PEER_TOPK):
        m = jnp.max(s, axis=0, keepdims=True)
        am = jnp.min(jnp.where(s == m, key_id, big), axis=0, keepdims=True)
        vals = jnp.where(row16 == j, m, vals)
        ids = jnp.where(row16 == j, am, ids)
        s = jnp.where(key_id == am, -jnp.inf, s)
    return vals, ids


_PAIR_GROUPS = ((0, 0, 8), (0, 8, 8), (1, 0, 8), (2, 0, 5), (3, 0, 4), (4, 0, 3), (5, 0, 2), (6, 0, 2), (7, 0, 2))


def _route_head(s0, i0, s1, i1):
    sub = lax.broadcasted_iota(jnp.int32, (VREG_SUBLANES, VREG_LANES), 0)
    cands, flat, eids = [], [], []
    for a, b0, nb in _PAIR_GROUPS:
        c = s0[a:a + 1] + s1[b0:b0 + VREG_SUBLANES]
        cands.append(jnp.where(sub < nb, c, -jnp.inf) if nb < VREG_SUBLANES else c)
        flat.append(a * PEER_TOPK + b0 + sub)
        eids.append(i0[a:a + 1] * N_KEYS + i1[b0:b0 + VREG_SUBLANES])
    cands.append(s0[VREG_SUBLANES:] + s1[0:1])
    flat.append((sub + VREG_SUBLANES) * PEER_TOPK)
    eids.append(i0[VREG_SUBLANES:] * N_KEYS + i1[0:1])
    cand = jnp.concatenate(cands, axis=0)
    flat = jnp.concatenate(flat, axis=0)
    eid = jnp.concatenate(eids, axis=0)
    row16 = lax.broadcasted_iota(jnp.int32, (PEER_TOPK, VREG_LANES), 0)
    sc = jnp.zeros((PEER_TOPK, VREG_LANES), jnp.float32)
    sel = jnp.zeros((PEER_TOPK, VREG_LANES), jnp.int32)
    big = jnp.int32(2 ** 30)
    for j in range(PEER_TOPK):
        m = jnp.max(cand, axis=0, keepdims=True)
        am = jnp.min(jnp.where(cand == m, flat, big), axis=0, keepdims=True)
        hit = flat == am
        e = jnp.max(jnp.where(hit, eid, -1), axis=0, keepdims=True)
        sc = jnp.where(row16 == j, m, sc)
        sel = jnp.where(row16 == j, e, sel)
        cand = jnp.where(hit, -jnp.inf, cand)
    p = jnp.exp(sc - sc[0:1])
    return sel, p / jnp.sum(p, axis=0, keepdims=True)


def _route_kernel(x_ref, wq_ref, sk_ref, rows_ref, g_ref, q_ref):
    q = jnp.dot(x_ref[...].astype(jnp.bfloat16), wq_ref[...], preferred_element_type=jnp.float32)
    q_ref[...] = q.astype(jnp.bfloat16)
    key_id = lax.broadcasted_iota(jnp.int32, (N_KEYS, VREG_LANES), 0)

    def head(h, carry):
        tops = []
        for c in range(2):
            hc = h * 2 + c
            qhc = q_ref[:, pl.ds(pl.multiple_of(hc * PEER_HALF, PEER_HALF), PEER_HALF)]
            s = lax.dot_general(sk_ref[hc], qhc, _NT_DIMS, preferred_element_type=jnp.float32)
            tops.append([_top16_rows(s[:, j * VREG_LANES:(j + 1) * VREG_LANES], key_id)
                         for j in range(ROUTE_TM // VREG_LANES)])
        r0 = pl.multiple_of(h * PEER_TOPK, PEER_TOPK)
        for j in range(ROUTE_TM // VREG_LANES):
            (s0, i0), (s1, i1) = tops[0][j], tops[1][j]
            sel, g = _route_head(s0, i0, s1, i1)
            rows_ref[pl.ds(r0, PEER_TOPK), j * VREG_LANES:(j + 1) * VREG_LANES] = sel * EXPERT_ROWS
            g_ref[pl.ds(r0, PEER_TOPK), j * VREG_LANES:(j + 1) * VREG_LANES] = g
        return carry

    lax.fori_loop(0, PEER_HEADS, head, 0)


def peer_route(x, w_query, sub_keys):
    T = x.shape[0]
    wq = w_query.astype(jnp.bfloat16)
    sk = sub_keys.reshape(PEER_HEADS * 2, N_KEYS, PEER_HALF).astype(jnp.bfloat16)
    out_blk = pl.BlockSpec((PEER_SEL, ROUTE_TM), lambda i: (0, i))
    rows, g = pl.pallas_call(
        _route_kernel,
        grid=(T // ROUTE_TM,),
        in_specs=[
            pl.BlockSpec((ROUTE_TM, D_MODEL), lambda i: (i, 0)),
            pl.BlockSpec(wq.shape, lambda i: (0, 0)),
            pl.BlockSpec(sk.shape, lambda i: (0, 0, 0)),
        ],
        out_specs=[out_blk, out_blk],
        out_shape=[jax.ShapeDtypeStruct((PEER_SEL, T), jnp.int32),
                   jax.ShapeDtypeStruct((PEER_SEL, T), jnp.float32)],
        scratch_shapes=[pltpu.VMEM((ROUTE_TM, PEER_HEADS * PEER_QDIM), jnp.bfloat16)],
        compiler_params=pltpu.CompilerParams(dimension_semantics=("arbitrary",),
                                             vmem_limit_bytes=VMEM_LIMIT_BYTES),
        name="peer_route",
    )(x, wq, sk)
    return rows.T, g.T


def encoder_layer(x, mem, rel_bias, w_in, b_in, conv_w, conv_b, attn_sink, f_w_in, f_w_mid, f_b, f_freq,
                  f_w_out, hyena_bias, w_mem_kv, w_branch, w_out, ln1_g, ln1_b, w_query, sub_keys,
                  expert_u, expert_v, ln2_g, ln2_b):
    B, S, _ = x.shape
    T = B * S
    xf = x.reshape(T, D_MODEL)
    proj = lambda lo, hi, **kw: linear(xf, w_in[:, lo:hi], b_in[lo:hi], **kw)
    qkv = proj(0, O_HY).reshape(B, S, O_HY)
    hy = proj(O_HY, O_MQ).reshape(B, S, O_MQ - O_HY)
    q_m = proj(O_MQ, O_GATE).reshape(B, S, MEM_WIDTH)
    gates = proj(O_GATE, IN_WIDTH, tn=(IN_WIDTH - O_GATE) // 2, sigmoid=True)
    M = mem.shape[1]
    kv = linear(mem.reshape(B * M, D_MODEL), w_mem_kv, jnp.zeros((2 * MEM_WIDTH,), jnp.float32), tm=B * M)
    att = window_attention_pallas(qkv, window_bias_table(rel_bias), attn_sink)
    hyo = hyena_mixer_pallas(hy, conv_w, conv_b, f_w_in, f_w_mid, f_b, f_freq, f_w_out, hyena_bias)
    mat = memory_attention_pallas(q_m, kv.reshape(B, M, 2 * MEM_WIDTH))
    x1 = merge_norm(att.reshape(T, ATT_WIDTH), hyo.reshape(T, HYENA_WIDTH), mat.reshape(T, MEM_WIDTH), gates, xf,
                    w_branch, w_out, ln1_g, ln1_b)
    rows, g = peer_route(x1, w_query, sub_keys)
    peer = peer_experts(x1, rows, g, pack_expert_table(expert_u), pack_expert_table(expert_v))
    return residual_norm(x1, peer, ln2_g, ln2_b).reshape(B, S, D_MODEL)


def kernel(x_prompt, x_sample, mem_prompt, mem_sample, rel_bias, w_in, b_in, conv_w, conv_b, attn_sink,
           f_w_in, f_w_mid, f_b, f_freq, f_w_out, hyena_bias, w_mem_kv, w_branch, w_out, ln1_g, ln1_b,
           w_query, sub_keys, expert_u, expert_v, ln2_g, ln2_b):
    nb = x_prompt.shape[0]
    x = jnp.concatenate([x_prompt, x_sample], axis=0)
    mem = jnp.concatenate([mem_prompt, mem_sample], axis=0)
    for l in range(DEPTH):
        x = encoder_layer(x, mem, rel_bias, w_in[l], b_in[l], conv_w[l], conv_b[l], attn_sink[l],
                          f_w_in[l], f_w_mid[l], f_b[l], f_freq[l], f_w_out[l], hyena_bias[l],
                          w_mem_kv[l], w_branch[l], w_out[l], ln1_g[l], ln1_b[l], w_query[l],
                          sub_keys[l], expert_u[l], expert_v[l], ln2_g[l], ln2_b[l])
    return (x[:nb], x[nb:])
```

```python
import functools
import math

import jax
import jax.numpy as jnp
from jax import lax
from jax.experimental import pallas as pl
from jax.experimental.pallas import tpu as pltpu

D_MODEL = 1024
DEPTH = 2
N_MEM = 256
ATT_HEADS = 8
ATT_KV_HEADS = 2
ATT_HEAD_DIM = 64
ATT_WIDTH = ATT_HEADS * ATT_HEAD_DIM
ATT_KV_WIDTH = ATT_KV_HEADS * ATT_HEAD_DIM
WINDOW = 128
BLOCK = 128
N_BUCKETS = 32
MAX_DISTANCE = 128
HYENA_WIDTH = 512
HYENA_ORDER = 2
SHORT_CONV = 3
FILTER_EMB = 33
FILTER_BANDS = (FILTER_EMB - 1) // 2
FILTER_HIDDEN = 64
FILTER_INNER = 2
FAST_DECAY_PCT = 0.3
SLOW_DECAY_PCT = 1.5
DECAY_TARGET = 1e-2
MEM_HEADS = 4
MEM_HEAD_DIM = 128
MEM_WIDTH = MEM_HEADS * MEM_HEAD_DIM
N_BRANCH = 3
BRANCH_WIDTH = 512
PEER_HEADS = 8
N_KEYS = 128
N_EXPERTS = N_KEYS * N_KEYS
PEER_TOPK = 16
PEER_HALF = 128
PEER_QDIM = 2 * PEER_HALF
PEER_BLOCK = 128
O_K = ATT_WIDTH
O_V = O_K + ATT_KV_WIDTH
O_HY = O_V + ATT_KV_WIDTH
O_MQ = O_HY + (HYENA_ORDER + 1) * HYENA_WIDTH
O_GATE = O_MQ + MEM_WIDTH
IN_WIDTH = O_GATE + N_BRANCH * D_MODEL
ALPHA = (2 * DEPTH) ** 0.25
BETA = (8 * DEPTH) ** -0.25
LN_EPS = 1e-5
NEG_INF = -1e30

VMEM_LIMIT_BYTES = 56 * 1024 * 1024


def _linear_kernel(x_ref, w_ref, b_ref, o_ref, *, sigmoid):
    x = x_ref[...].astype(jnp.bfloat16)
    acc = jnp.dot(x, w_ref[...], preferred_element_type=jnp.float32) + b_ref[...]
    o_ref[...] = jax.nn.sigmoid(acc) if sigmoid else acc


def linear(x, w, b, *, tm=512, tn=None, sigmoid=False):
    T, K = x.shape
    N = w.shape[1]
    tn = N if tn is None else tn
    wb = w.astype(jnp.bfloat16)
    return pl.pallas_call(
        functools.partial(_linear_kernel, sigmoid=sigmoid),
        grid=(N // tn, T // tm),
        in_specs=[
            pl.BlockSpec((tm, K), lambda j, i: (i, 0)),
            pl.BlockSpec((K, tn), lambda j, i: (0, j)),
            pl.BlockSpec((1, tn), lambda j, i: (0, j)),
        ],
        out_specs=pl.BlockSpec((tm, tn), lambda j, i: (i, j)),
        out_shape=jax.ShapeDtypeStruct((T, N), jnp.float32),
        compiler_params=pltpu.CompilerParams(
            dimension_semantics=("arbitrary", "arbitrary"),
            vmem_limit_bytes=VMEM_LIMIT_BYTES),
        name="linear",
    )(x, wb, b.reshape(1, N))


ATT_TQ = 512
ATT_KEYS = 3 * BLOCK
_NT_DIMS = (((1,), (1,)), ((), ()))


def t5_bucket(rel):
    nb = N_BUCKETS // 2
    max_exact = nb // 2
    ret = jnp.where(rel > 0, nb, 0)
    n = jnp.abs(rel)
    nf = jnp.maximum(n, 1).astype(jnp.float32)
    large = max_exact + (jnp.log(nf / max_exact) / math.log(MAX_DISTANCE / max_exact)
                         * (nb - max_exact)).astype(jnp.int32)
    large = jnp.minimum(large, nb - 1)
    return ret + jnp.where(n < max_exact, n, large)


def window_bias_table(rel_bias):
    rel = (jnp.arange(ATT_KEYS) - BLOCK)[None, :] - jnp.arange(BLOCK)[:, None]
    bias = rel_bias[t5_bucket(rel)].astype(jnp.float32).transpose(2, 0, 1)
    return jnp.where((jnp.abs(rel) <= WINDOW)[None], bias, NEG_INF)


def _col_params():
    return pltpu.CompilerParams(dimension_semantics=("arbitrary", "arbitrary"), vmem_limit_bytes=VMEM_LIMIT_BYTES)


def _window_attn_kernel(sink_ref, q_ref, kp_ref, kc_ref, kn_ref, vp_ref, vc_ref, vn_ref, bias_ref, o_ref, *, n_steps):
    i = pl.program_id(1)
    bf = jnp.bfloat16
    q = q_ref[0].astype(bf)
    k_all = jnp.concatenate([kp_ref[0], kc_ref[0], kn_ref[0]], axis=0).astype(bf)
    v_all = jnp.concatenate([vp_ref[0], vc_ref[0], vn_ref[0]], axis=0).astype(bf)
    col = lax.broadcasted_iota(jnp.int32, (BLOCK, ATT_KEYS), 1)
    n_blk = ATT_TQ // BLOCK
    scale = ATT_HEAD_DIM ** -0.5
    group = ATT_HEADS // ATT_KV_HEADS
    for j in range(n_blk):
        kj = k_all[j * BLOCK:j * BLOCK + ATT_KEYS]
        vj = v_all[j * BLOCK:j * BLOCK + ATT_KEYS]
        off_seq = None
        if j == 0:
            off_seq = (i == 0) & (col < BLOCK)
        if j == n_blk - 1:
            last = (i == n_steps - 1) & (col >= 2 * BLOCK)
            off_seq = last if off_seq is None else off_seq | last
        for h in range(ATT_HEADS):
            hk = h // group
            qh = q[j * BLOCK:(j + 1) * BLOCK, h * ATT_HEAD_DIM:(h + 1) * ATT_HEAD_DIM]
            kh = kj[:, hk * ATT_HEAD_DIM:(hk + 1) * ATT_HEAD_DIM]
            s = lax.dot_general(qh, kh, _NT_DIMS, preferred_element_type=jnp.float32) * scale + bias_ref[h]
            if off_seq is not None:
                s = jnp.where(off_seq, NEG_INF, s)
            sink = sink_ref[h]
            m = jnp.maximum(jnp.max(s, axis=-1, keepdims=True), sink)
            p = jnp.exp(s - m)
            denom = jnp.sum(p, axis=-1, keepdims=True) + jnp.exp(sink - m)
            oh = jnp.dot((p / denom).astype(bf), vj[:, hk * ATT_HEAD_DIM:(hk + 1) * ATT_HEAD_DIM],
                         preferred_element_type=jnp.float32)
            o_ref[0, j * BLOCK:(j + 1) * BLOCK, h * ATT_HEAD_DIM:(h + 1) * ATT_HEAD_DIM] = oh.astype(o_ref.dtype)


def window_attention_pallas(qkv, bias, sink):
    B, S, _ = qkv.shape
    n_steps = S // ATT_TQ
    per = ATT_TQ // BLOCK
    last_blk = S // BLOCK - 1
    k_col, v_col = O_K // ATT_KV_WIDTH, O_V // ATT_KV_WIDTH

    def edge(col, nxt):
        if nxt:
            return pl.BlockSpec((1, BLOCK, ATT_KV_WIDTH), lambda b, i: (b, jnp.minimum((i + 1) * per, last_blk), col))
        return pl.BlockSpec((1, BLOCK, ATT_KV_WIDTH), lambda b, i: (b, jnp.maximum(i * per - 1, 0), col))

    cur = lambda col: pl.BlockSpec((1, ATT_TQ, ATT_KV_WIDTH), lambda b, i: (b, i, col))
    return pl.pallas_call(
        functools.partial(_window_attn_kernel, n_steps=n_steps),
        grid=(B, n_steps),
        in_specs=[pl.BlockSpec(memory_space=pltpu.SMEM),
                  pl.BlockSpec((1, ATT_TQ, ATT_WIDTH), lambda b, i: (b, i, 0)),
                  edge(k_col, False), cur(k_col), edge(k_col, True),
                  edge(v_col, False), cur(v_col), edge(v_col, True),
                  pl.BlockSpec(bias.shape, lambda b, i: (0, 0, 0))],
        out_specs=pl.BlockSpec((1, ATT_TQ, ATT_WIDTH), lambda b, i: (b, i, 0)),
        out_shape=jax.ShapeDtypeStruct((B, S, ATT_WIDTH), jnp.bfloat16),
        compiler_params=_col_params(),
        name="window_attn",
    )(sink.astype(jnp.float32), qkv, qkv, qkv, qkv, qkv, qkv, qkv, bias)


MEM_TQ = 512


def _mem_attn_kernel(q_ref, kv_ref, o_ref):
    bf = jnp.bfloat16
    q = q_ref[0].astype(bf)
    kv = kv_ref[0].astype(bf)
    scale = MEM_HEAD_DIM ** -0.5
    for h in range(MEM_HEADS):
        sl = slice(h * MEM_HEAD_DIM, (h + 1) * MEM_HEAD_DIM)
        s = lax.dot_general(q[:, sl], kv[:, sl], _NT_DIMS, preferred_element_type=jnp.float32) * scale
        p = jnp.exp(s - jnp.max(s, axis=-1, keepdims=True))
        p = (p / jnp.sum(p, axis=-1, keepdims=True)).astype(bf)
        vh = kv[:, MEM_WIDTH + h * MEM_HEAD_DIM:MEM_WIDTH + (h + 1) * MEM_HEAD_DIM]
        o_ref[0, :, sl] = jnp.dot(p, vh, preferred_element_type=jnp.float32).astype(o_ref.dtype)


def memory_attention_pallas(q, kv):
    B, S, _ = q.shape
    M = kv.shape[1]
    return pl.pallas_call(
        _mem_attn_kernel,
        grid=(B, S // MEM_TQ),
        in_specs=[pl.BlockSpec((1, MEM_TQ, MEM_WIDTH), lambda b, i: (b, i, 0)),
                  pl.BlockSpec((1, M, 2 * MEM_WIDTH), lambda b, i: (b, 0, 0))],
        out_specs=pl.BlockSpec((1, MEM_TQ, MEM_WIDTH), lambda b, i: (b, i, 0)),
        out_shape=jax.ShapeDtypeStruct((B, S, MEM_WIDTH), jnp.bfloat16),
        compiler_params=_col_params(),
        name="mem_attn",
    )(q, kv)


MERGE_TM = 256


def _layer_norm_rows(y, g, b):
    mu = jnp.mean(y, axis=-1, keepdims=True)
    d = y - mu
    var = jnp.mean(d * d, axis=-1, keepdims=True)
    return d * lax.rsqrt(var + LN_EPS) * g + b


def _merge_kernel(a_ref, h_ref, m_ref, g_ref, x_ref, wb_ref, wo_ref, lg_ref, lb_ref, o_ref):
    bf = jnp.bfloat16
    f32 = jnp.float32
    merged = g_ref[:, 0:D_MODEL] * jnp.dot(a_ref[...].astype(bf), wb_ref[0], preferred_element_type=f32)
    merged = merged + g_ref[:, D_MODEL:2 * D_MODEL] * jnp.dot(h_ref[...].astype(bf), wb_ref[1], preferred_element_type=f32)
    merged = merged + g_ref[:, 2 * D_MODEL:] * jnp.dot(m_ref[...].astype(bf), wb_ref[2], preferred_element_type=f32)
    y = ALPHA * x_ref[...] + jnp.dot(merged.astype(bf), wo_ref[...], preferred_element_type=f32)
    o_ref[...] = _layer_norm_rows(y, lg_ref[...], lb_ref[...])


def merge_norm(att, hy, mem, gates, x, w_branch, w_out, ln_g, ln_b):
    T = x.shape[0]
    bf = jnp.bfloat16
    rows = lambda w: pl.BlockSpec((MERGE_TM, w), lambda i: (i, 0))
    whole = lambda a: pl.BlockSpec(a.shape, lambda i: (0,) * a.ndim)
    wb, wo = w_branch.astype(bf), w_out.astype(bf)
    lg, lb = ln_g.reshape(1, D_MODEL), ln_b.reshape(1, D_MODEL)
    return pl.pallas_call(
        _merge_kernel,
        grid=(T // MERGE_TM,),
        in_specs=[rows(BRANCH_WIDTH), rows(BRANCH_WIDTH), rows(BRANCH_WIDTH), rows(N_BRANCH * D_MODEL), rows(D_MODEL),
                  whole(wb), whole(wo), whole(lg), whole(lb)],
        out_specs=rows(D_MODEL),
        out_shape=jax.ShapeDtypeStruct((T, D_MODEL), jnp.float32),
        compiler_params=pltpu.CompilerParams(dimension_semantics=("arbitrary",), vmem_limit_bytes=VMEM_LIMIT_BYTES),
        name="merge_norm",
    )(att, hy, mem, gates, x, wb, wo, lg, lb)


def _residual_norm_kernel(x_ref, r_ref, lg_ref, lb_ref, o_ref):
    o_ref[...] = _layer_norm_rows(ALPHA * x_ref[...] + r_ref[...], lg_ref[...], lb_ref[...])


def residual_norm(x, r, ln_g, ln_b):
    T = x.shape[0]
    tm = 512
    rows = pl.BlockSpec((tm, D_MODEL), lambda i: (i, 0))
    one = pl.BlockSpec((1, D_MODEL), lambda i: (0, 0))
    return pl.pallas_call(
        _residual_norm_kernel,
        grid=(T // tm,),
        in_specs=[rows, rows, one, one],
        out_specs=rows,
        out_shape=jax.ShapeDtypeStruct((T, D_MODEL), jnp.float32),
        compiler_params=pltpu.CompilerParams(dimension_semantics=("arbitrary",), vmem_limit_bytes=VMEM_LIMIT_BYTES),
        name="residual_norm",
    )(x, r, ln_g.reshape(1, D_MODEL), ln_b.reshape(1, D_MODEL))


FFT_N1 = 256
FFT_N2 = 128
FFT_N = FFT_N1 * FFT_N2
HY_COLS = FFT_N2 * HYENA_WIDTH
LMUL_TN = 2048
KB_K1 = 8
FILT_TM = 512


def _dft_constants():
    f32 = jnp.float32
    n1 = jnp.arange(FFT_N1, dtype=jnp.int32)
    ang1 = ((n1[:, None] * n1[None, :]) % FFT_N1).astype(f32) * f32(2.0 * math.pi / FFT_N1)
    fr, fi = jnp.cos(ang1), -jnp.sin(ang1)
    h = FFT_N1 // 2
    f_fwd = jnp.concatenate([jnp.concatenate([fr[:, :h], -fi[:, :h]], axis=1),
                             jnp.concatenate([fi[:, :h], fr[:, :h]], axis=1)], axis=0)
    f_inv = f_fwd.T * f32(1.0 / FFT_N)
    f_real = jnp.concatenate([fr, fi], axis=0)
    k2 = jnp.arange(FFT_N2, dtype=jnp.int32)
    k = n1[:, None, None] + FFT_N1 * k2[None, :, None]
    ang2 = ((k * k2[None, None, :]) % FFT_N).astype(f32) * f32(2.0 * math.pi / FFT_N)
    gr, gi = jnp.cos(ang2), -jnp.sin(ang2)
    g = jnp.concatenate([jnp.concatenate([gr, -gi], axis=2), jnp.concatenate([gi, gr], axis=2)], axis=1)
    bf = jnp.bfloat16
    return f_fwd.astype(bf), f_inv.astype(bf), f_real.astype(bf), g.astype(bf), g.transpose(0, 2, 1).astype(bf)


def _lmul_pair_kernel(l_ref, xr_ref, xi_ref, o_ref):
    x = jnp.concatenate([xr_ref[0], xi_ref[0]], axis=0).astype(jnp.bfloat16)
    o_ref[0] = jnp.dot(l_ref[...], x, preferred_element_type=jnp.float32).astype(o_ref.dtype)


def _lmul_kernel(l_ref, x_ref, o_ref):
    o_ref[0] = jnp.dot(l_ref[...], x_ref[0].astype(jnp.bfloat16),
                       preferred_element_type=jnp.float32).astype(o_ref.dtype)


def _lmul_gate_kernel(l_ref, a_ref, zr_ref, zi_ref, pr_ref, pi_ref, b_ref, o_ref):
    y = jnp.dot(l_ref[...], a_ref[0], preferred_element_type=jnp.float32)
    h = FFT_N1 // 2
    o_ref[0, :h] = pr_ref[0] * (y[:h] + zr_ref[0] * b_ref[...])
    o_ref[0, h:] = pi_ref[0] * (y[h:] + zi_ref[0] * b_ref[...])


def dft_stage1_pairs(f_fwd, z):
    nb, h, cols = z.shape
    npair = (nb + 1) // 2
    last = nb - 1
    return pl.pallas_call(
        _lmul_pair_kernel,
        grid=(npair, cols // LMUL_TN),
        in_specs=[
            pl.BlockSpec(f_fwd.shape, lambda p, j: (0, 0)),
            pl.BlockSpec((1, h, LMUL_TN), lambda p, j: (jnp.minimum(2 * p, last), 0, j)),
            pl.BlockSpec((1, h, LMUL_TN), lambda p, j: (jnp.minimum(2 * p + 1, last), 0, j)),
        ],
        out_specs=pl.BlockSpec((1, 2 * FFT_N1, LMUL_TN), lambda p, j: (p, 0, j)),
        out_shape=jax.ShapeDtypeStruct((npair, 2 * FFT_N1, cols), jnp.bfloat16),
        compiler_params=_col_params(),
        name="dft_stage1",
    )(f_fwd, z, z)


def dft_stage1_real(f_real, x):
    _, h, cols = x.shape
    return pl.pallas_call(
        _lmul_kernel,
        grid=(1, cols // LMUL_TN),
        in_specs=[pl.BlockSpec(f_real.shape, lambda p, j: (0, 0)),
                  pl.BlockSpec((1, h, LMUL_TN), lambda p, j: (p, 0, j))],
        out_specs=pl.BlockSpec((1, 2 * FFT_N1, LMUL_TN), lambda p, j: (p, 0, j)),
        out_shape=jax.ShapeDtypeStruct((1, 2 * FFT_N1, cols), jnp.bfloat16),
        compiler_params=_col_params(),
        name="dft_stage1_real",
    )(f_real, x)


def idft_stage1_gate(f_inv, a, z, p, bias_row):
    npair, _, cols = a.shape
    half = FFT_N1 // 2

    def half_blk(x, r):
        last = x.shape[0] - 1
        return pl.BlockSpec((1, half, LMUL_TN), lambda q, j: (jnp.minimum(2 * q + r, last), 0, j))

    out = pl.pallas_call(
        _lmul_gate_kernel,
        grid=(npair, cols // LMUL_TN),
        in_specs=[pl.BlockSpec(f_inv.shape, lambda q, j: (0, 0)),
                  pl.BlockSpec((1, 2 * FFT_N1, LMUL_TN), lambda q, j: (q, 0, j)),
                  half_blk(z, 0), half_blk(z, 1), half_blk(p, 0), half_blk(p, 1),
                  pl.BlockSpec((1, LMUL_TN), lambda q, j: (0, 0))],
        out_specs=pl.BlockSpec((1, FFT_N1, LMUL_TN), lambda q, j: (q, 0, j)),
        out_shape=jax.ShapeDtypeStruct((npair, FFT_N1, cols), jnp.float32),
        compiler_params=_col_params(),
        name="idft_stage1_gate",
    )(f_inv, a, z, z, p, p, bias_row)
    return out.reshape(2 * npair, half, cols)


def _stage2_conv_kernel(a_ref, g_ref, gt_ref, h_ref, o_ref):
    for j in range(KB_K1):
        x = jnp.concatenate([a_ref[0, 0, j], a_ref[0, 1, j]], axis=0)
        z = jnp.dot(g_ref[j], x, preferred_element_type=jnp.float32)
        zr, zi = z[:FFT_N2], z[FFT_N2:]
        hr, hi = h_ref[0, j], h_ref[1, j]
        w = jnp.concatenate([zr * hr - zi * hi, zr * hi + zi * hr], axis=0).astype(jnp.bfloat16)
        y = jnp.dot(gt_ref[j], w, preferred_element_type=jnp.float32)
        o_ref[0, 0, j] = y[:FFT_N2].astype(o_ref.dtype)
        o_ref[0, 1, j] = y[FFT_N2:].astype(o_ref.dtype)


def stage2_conv(a, g, gt, hf, order):
    npair = a.shape[0]
    C = HYENA_WIDTH
    ablk = pl.BlockSpec((1, 2, KB_K1, FFT_N2, C), lambda p, i: (p, 0, i, 0, 0))
    gblk = pl.BlockSpec((KB_K1, 2 * FFT_N2, 2 * FFT_N2), lambda p, i: (i, 0, 0))
    return pl.pallas_call(
        _stage2_conv_kernel,
        grid=(npair, FFT_N1 // KB_K1),
        in_specs=[ablk, gblk, gblk,
                  pl.BlockSpec((2, KB_K1, FFT_N2, C), lambda p, i: (0, i, 0, order))],
        out_specs=ablk,
        out_shape=jax.ShapeDtypeStruct(a.shape, a.dtype),
        compiler_params=_col_params(),
        name="stage2_conv",
    )(a, g, gt, hf)


def _stage2_filter_kernel(a_ref, g_ref, s_ref, o_ref):
    for j in range(KB_K1):
        x = jnp.concatenate([a_ref[0, j], a_ref[1, j]], axis=0)
        z = jnp.dot(g_ref[j], x, preferred_element_type=jnp.float32) * s_ref[...]
        o_ref[0, j] = z[:FFT_N2]
        o_ref[1, j] = z[FFT_N2:]


def stage2_filter(a, g, inv_norm):
    W = a.shape[-1]
    blk = pl.BlockSpec((2, KB_K1, FFT_N2, W), lambda i: (0, i, 0, 0))
    return pl.pallas_call(
        _stage2_filter_kernel,
        grid=(FFT_N1 // KB_K1,),
        in_specs=[blk, pl.BlockSpec((KB_K1, 2 * FFT_N2, 2 * FFT_N2), lambda i: (i, 0, 0)),
                  pl.BlockSpec((1, W), lambda i: (0, 0))],
        out_specs=blk,
        out_shape=jax.ShapeDtypeStruct(a.shape, jnp.float32),
        compiler_params=pltpu.CompilerParams(dimension_semantics=("arbitrary",), vmem_limit_bytes=VMEM_LIMIT_BYTES),
        name="stage2_filter",
    )(a, g, inv_norm)


def _filter_gen_kernel(bands_ref, w0_ref, wc_ref, ws_ref, wmid_ref, fb_ref, ff_ref, wdir_ref, wbwd_ref,
                       adelta_ref, two_ref, norm_ref, *, seq):
    i = pl.program_id(0)
    bf = jnp.bfloat16
    row = i * FILT_TM + lax.broadcasted_iota(jnp.int32, (FILT_TM, 1), 0)
    pos_i = jnp.where(row < seq, row, 2 * seq - row)
    pos = pos_i.astype(jnp.float32)
    t = pos / max(seq - 1, 1)
    w = (jnp.float32(2.0 * math.pi) * pos) / seq
    ang = w * bands_ref[...]
    pre = (jnp.dot(jnp.cos(ang).astype(bf), wc_ref[...], preferred_element_type=jnp.float32)
           + jnp.dot((-jnp.sin(ang)).astype(bf), ws_ref[...], preferred_element_type=jnp.float32)
           + t.astype(bf).astype(jnp.float32) * w0_ref[...])
    h = jnp.sin(ff_ref[0:1, :] * (pre + fb_ref[0:1, :]))
    for m in range(FILTER_INNER):
        pre = jnp.dot(h.astype(bf), wmid_ref[m], preferred_element_type=jnp.float32)
        h = jnp.sin(ff_ref[m + 1:m + 2, :] * (pre + fb_ref[m + 1:m + 2, :]))
    hb = h.astype(bf)
    decay = jnp.exp(-t * adelta_ref[...])
    out = jnp.dot(hb, wdir_ref[0], preferred_element_type=jnp.float32) * decay
    out = jnp.where(pos_i < seq, out, 0.0)

    @pl.when(i == 0)
    def _():
        extra = jnp.dot(hb, wbwd_ref[...], preferred_element_type=jnp.float32) * decay
        first = jnp.where(row == 0, out + extra, out)
        two_ref[...] = first
        norm_ref[...] = jnp.sum(jnp.abs(first), axis=0, keepdims=True)

    @pl.when(i != 0)
    def _():
        two_ref[...] = out
        norm_ref[...] += jnp.sum(jnp.abs(out), axis=0, keepdims=True)


def hyena_filter_rows(seq, f_w_in, f_w_mid, f_b, f_freq, f_w_out):
    f32, bf = jnp.float32, jnp.bfloat16
    C, H = HYENA_WIDTH, FILTER_HIDDEN
    bands = jnp.linspace(1e-4, FILTER_BANDS - 1, FILTER_BANDS, dtype=f32)
    bands = jnp.pad(bands, (0, VREG_LANES - FILTER_BANDS)).reshape(1, VREG_LANES)
    w_in = f_w_in.astype(bf)
    w0 = w_in[0:1].astype(f32)
    pad_rows = ((0, VREG_LANES - FILTER_BANDS), (0, 0))
    wc = jnp.pad(w_in[1:1 + FILTER_BANDS], pad_rows)
    ws = jnp.pad(w_in[1 + FILTER_BANDS:], pad_rows)
    w_out = f_w_out.astype(bf).reshape(H, HYENA_ORDER, 2, C)
    wdir = jnp.stack([w_out[:, :, 0].reshape(H, HYENA_ORDER * C), w_out[:, :, 1].reshape(H, HYENA_ORDER * C)])
    max_decay = math.log(DECAY_TARGET) / FAST_DECAY_PCT
    min_decay = math.log(DECAY_TARGET) / SLOW_DECAY_PCT
    adelta = jnp.abs(jnp.linspace(min_decay, max_decay, C, dtype=f32))
    adelta = jnp.tile(adelta, HYENA_ORDER).reshape(1, HYENA_ORDER * C)
    n_tiles = 2 * seq // FILT_TM
    half_tiles = seq // FILT_TM
    whole = lambda a: pl.BlockSpec(a.shape, lambda i: (0,) * a.ndim)
    wmid = f_w_mid.astype(bf)
    fb, ff = f_b.astype(f32), f_freq.astype(f32)
    return pl.pallas_call(
        functools.partial(_filter_gen_kernel, seq=seq),
        grid=(n_tiles,),
        in_specs=[whole(bands), whole(w0), whole(wc), whole(ws), whole(wmid), whole(fb), whole(ff),
                  pl.BlockSpec((1, H, HYENA_ORDER * C), lambda i: (i // half_tiles, 0, 0)),
                  pl.BlockSpec((None, H, HYENA_ORDER * C), lambda i: (1, 0, 0)),
                  whole(adelta)],
        out_specs=[pl.BlockSpec((FILT_TM, HYENA_ORDER * C), lambda i: (i, 0)),
                   pl.BlockSpec((1, HYENA_ORDER * C), lambda i: (0, 0))],
        out_shape=[jax.ShapeDtypeStruct((2 * seq, HYENA_ORDER * C), f32),
                   jax.ShapeDtypeStruct((1, HYENA_ORDER * C), f32)],
        compiler_params=pltpu.CompilerParams(dimension_semantics=("arbitrary",), vmem_limit_bytes=VMEM_LIMIT_BYTES),
        name="hyena_filter_gen",
    )(bands, w0, wc, ws, wmid, fb, ff, wdir, wdir, adelta)


def _short_conv_kernel(x_ref, w_ref, b_ref, o_ref):
    x = x_ref[0]
    n = x.shape[0]
    t = lax.broadcasted_iota(jnp.int32, x.shape, 0)
    prev = jnp.where(t == 0, 0.0, pltpu.roll(x, 1, 0))
    nxt = jnp.where(t == n - 1, 0.0, pltpu.roll(x, n - 1, 0))
    o_ref[0] = ((b_ref[...] + prev * w_ref[0:1, :]) + x * w_ref[1:2, :]) + nxt * w_ref[2:3, :]


def short_conv_pallas(u, w, b):
    B, L, W = u.shape
    tc = VREG_LANES
    return pl.pallas_call(
        _short_conv_kernel,
        grid=(B, W // tc),
        in_specs=[pl.BlockSpec((1, L, tc), lambda i, j: (i, 0, j)),
                  pl.BlockSpec((SHORT_CONV, tc), lambda i, j: (0, j)),
                  pl.BlockSpec((1, tc), lambda i, j: (0, j))],
        out_specs=pl.BlockSpec((1, L, tc), lambda i, j: (i, 0, j)),
        out_shape=jax.ShapeDtypeStruct((B, L, W), jnp.float32),
        compiler_params=_col_params(),
        name="short_conv",
    )(u, w, b.reshape(1, W))


def hyena_mixer_pallas(u, conv_w, conv_b, f_w_in, f_w_mid, f_b, f_freq, f_w_out, hyena_bias):
    B, L, _ = u.shape
    assert 2 * L == FFT_N
    C = HYENA_WIDTH
    f_fwd, f_inv, f_real, g, gt = _dft_constants()
    two, norm = hyena_filter_rows(L, f_w_in, f_w_mid, f_b, f_freq, f_w_out)
    af = dft_stage1_real(f_real, two.reshape(1, FFT_N1, FFT_N2 * HYENA_ORDER * C))
    af = af.reshape(2, FFT_N1, FFT_N2, HYENA_ORDER * C)
    hf = stage2_filter(af, g, 1.0 / norm)
    uc = short_conv_pallas(u, conv_w, conv_b)
    half = FFT_N1 // 2
    parts = [uc[..., o * C:(o + 1) * C].reshape(B, half, HY_COLS) for o in range(HYENA_ORDER + 1)]
    z = parts[0]
    for o in range(HYENA_ORDER):
        a = dft_stage1_pairs(f_fwd, z)
        npair = a.shape[0]
        a = stage2_conv(a.reshape(npair, 2, FFT_N1, FFT_N2, C), g, gt, hf, o)
        bias_row = jnp.tile(hyena_bias[o], LMUL_TN // C).reshape(1, LMUL_TN)
        z = idft_stage1_gate(f_inv, a.reshape(npair, 2 * FFT_N1, HY_COLS), z, parts[o + 1], bias_row)
    return z[:B].reshape(B, L, C)


PEER_SEL = PEER_HEADS * PEER_TOPK
PEER_TB = 64
VREG_SUBLANES = 8
VREG_LANES = 128
EXPERT_ROWS = 4
TILE_ROWS = 16
V_UNROLL = 8


def pack_expert_table(tab):
    e = tab.shape[0]
    b = lax.bitcast_convert_type(tab.astype(jnp.bfloat16), jnp.uint16).astype(jnp.uint32)
    b = b.reshape(e, 2, EXPERT_ROWS, VREG_LANES)
    w = (b[:, 0] | (b[:, 1] << 16)).reshape(e * EXPERT_ROWS, VREG_LANES)
    return jnp.pad(w, ((0, VREG_SUBLANES), (0, 0)))


def _expert_halves(tab_ref, row):
    w = tab_ref[pl.ds(row, VREG_SUBLANES), :]
    lo = lax.bitcast_convert_type(w << 16, jnp.float32)
    hi = lax.bitcast_convert_type(w & jnp.uint32(0xFFFF0000), jnp.float32)
    return lo, hi


def from_tile_rows(y):
    T = y.shape[0]
    return y.reshape(T, EXPERT_ROWS, 2, VREG_LANES).transpose(0, 2, 1, 3).reshape(T, D_MODEL)


def _split_bf16(v):
    hi = v.astype(jnp.bfloat16)
    return hi, (v - hi.astype(jnp.float32)).astype(jnp.bfloat16)


def _gelu_exact(x):
    return 0.5 * x * (1.0 + lax.erf(x * (2.0 ** -0.5)))


_COMBINE_POS = (3, 7, 1, 5, 2, 6, 0, 4)


def _sublane_sums(ps, sub):
    lo4 = (sub & 4) == 0
    c = []
    for a, b in ((ps[0], ps[1]), (ps[2], ps[3]), (ps[4], ps[5]), (ps[6], ps[7])):
        c.append(jnp.where(lo4, a, pltpu.roll(b, 4, 0)))
    mv = (sub & 2) != 0
    e = []
    for c1, c2 in ((c[0], c[1]), (c[2], c[3])):
        e.append(jnp.where(mv, c1 + pltpu.roll(c1, 2, 0), c2 + pltpu.roll(c2, 6, 0)))
    mo = (sub & 1) != 0
    return jnp.where(mo, e[0] + pltpu.roll(e[0], 1, 0), e[1] + pltpu.roll(e[1], 7, 0))


def _peer_u_kernel(idx_ref, x_ref, g_ref, tab_ref, o_ref, act_ref, r_ref):
    sub = lax.broadcasted_iota(jnp.int32, (VREG_SUBLANES, VREG_LANES), 0)

    def token(t, carry):
        x_lo = x_ref[t]
        x_hi = pltpu.roll(x_lo, EXPERT_ROWS, 0)
        rs = []
        for grp in range(PEER_SEL // VREG_SUBLANES):
            ps = [None] * VREG_SUBLANES
            for q in range(VREG_SUBLANES):
                lo, hi = _expert_halves(tab_ref, idx_ref[t, grp * VREG_SUBLANES + _COMBINE_POS[q]])
                ps[q] = lo * x_lo + hi * x_hi
            rs.append(_sublane_sums(ps, sub))
        r0 = pl.multiple_of(t * PEER_SEL, PEER_SEL)
        r_ref[pl.ds(r0, PEER_SEL), :] = jnp.concatenate(rs, axis=0).astype(jnp.bfloat16)
        return carry

    lax.fori_loop(0, PEER_TB, token, 0)
    ones = jnp.ones((VREG_LANES, VREG_LANES), jnp.bfloat16)
    keep = (lax.broadcasted_iota(jnp.int32, (PEER_SEL, VREG_LANES), 0)
            == lax.broadcasted_iota(jnp.int32, (PEER_SEL, VREG_LANES), 1))
    chunk = VREG_SUBLANES * PEER_SEL
    for c in range(PEER_TB // VREG_SUBLANES):
        s = jnp.dot(r_ref[c * chunk:(c + 1) * chunk, :], ones, preferred_element_type=jnp.float32)
        for j in range(VREG_SUBLANES):
            blk = jnp.where(keep, s[j * PEER_SEL:(j + 1) * PEER_SEL], 0.0)
            act_ref[c * VREG_SUBLANES + j:c * VREG_SUBLANES + j + 1, :] = jnp.sum(blk, axis=0, keepdims=True)
    o_ref[...] = g_ref[...] * _gelu_exact(act_ref[...])


def _peer_v_kernel(idx_ref, coef_ref, e_ref, d_ref, tab_ref, o_ref, chi_ref, clo_ref):
    c_hi, c_lo = _split_bf16(coef_ref[...])
    chi_ref[...] = jnp.dot(c_hi, e_ref[...], preferred_element_type=jnp.float32)
    clo_ref[...] = jnp.dot(c_lo, e_ref[...], preferred_element_type=jnp.float32)

    def token(t, carry):
        cmat = jnp.concatenate([chi_ref[pl.ds(t, 1), :] * d_ref[...], clo_ref[pl.ds(t, 1), :] * d_ref[...]],
                               axis=0).astype(jnp.bfloat16)
        w = jnp.concatenate([pltpu.bitcast(tab_ref[pl.ds(idx_ref[t, k], VREG_SUBLANES), :], jnp.bfloat16)
                             for k in range(PEER_SEL)], axis=0)
        acc = jnp.dot(cmat, w, preferred_element_type=jnp.float32)
        o_ref[t] = acc[:VREG_SUBLANES] + acc[VREG_SUBLANES:]
        return carry

    lax.fori_loop(0, PEER_TB, token, 0, unroll=V_UNROLL)


def peer_experts(x, eidx, g, tab_u, tab_v):
    T = x.shape[0]
    grid = (T // PEER_TB,)
    f32, bf = jnp.float32, jnp.bfloat16
    sel = jnp.arange(PEER_SEL, dtype=jnp.int32)
    col = jnp.arange(PEER_SEL * TILE_ROWS, dtype=jnp.int32)
    expand = (col[None, :] // TILE_ROWS == sel[:, None]).astype(bf)
    diag = (col[None, :] % TILE_ROWS == jnp.arange(VREG_SUBLANES, dtype=jnp.int32)[:, None]).astype(f32)
    smem_blk = pl.BlockSpec((PEER_TB, PEER_SEL), lambda i: (i, 0), memory_space=pltpu.SMEM)
    vec_blk = pl.BlockSpec((PEER_TB, PEER_SEL), lambda i: (i, 0))
    row_blk = pl.BlockSpec((PEER_TB, VREG_SUBLANES, VREG_LANES), lambda i: (i, 0, 0))
    whole = lambda a: pl.BlockSpec(a.shape, lambda i: (0,) * a.ndim)
    tab_spec = pl.BlockSpec(memory_space=pltpu.VMEM)
    params = pltpu.CompilerParams(dimension_semantics=("arbitrary",), vmem_limit_bytes=VMEM_LIMIT_BYTES)
    coef = pl.pallas_call(
        _peer_u_kernel,
        grid=grid,
        in_specs=[smem_blk, row_blk, vec_blk, tab_spec],
        out_specs=vec_blk,
        out_shape=jax.ShapeDtypeStruct((T, PEER_SEL), f32),
        scratch_shapes=[pltpu.VMEM((PEER_TB, PEER_SEL), f32),
                        pltpu.VMEM((PEER_TB * PEER_SEL, VREG_LANES), bf)],
        compiler_params=params,
        name="peer_u",
    )(eidx, x.reshape(T, VREG_SUBLANES, VREG_LANES), g, tab_u)
    out = pl.pallas_call(
        _peer_v_kernel,
        grid=grid,
        in_specs=[smem_blk, vec_blk, whole(expand), whole(diag), tab_spec],
        out_specs=row_blk,
        out_shape=jax.ShapeDtypeStruct((T, VREG_SUBLANES, VREG_LANES), f32),
        scratch_shapes=[pltpu.VMEM((PEER_TB, PEER_SEL * TILE_ROWS), f32),
                        pltpu.VMEM((PEER_TB, PEER_SEL * TILE_ROWS), f32)],
        compiler_params=params,
        name="peer_v",
    )(eidx, coef, expand, diag, tab_v)
    return from_tile_rows(out)


ROUTE_TM = 256


def _top16_rows(s, key_id):
    row16 = lax.broadcasted_iota(jnp.int32, (PEER_TOPK, VREG_LANES), 0)
    vals = jnp.zeros((PEER_TOPK, VREG_LANES), jnp.float32)
    ids = jnp.zeros((PEER_TOPK, VREG_LANES), jnp.int32)
    big = jnp.int32(2 ** 30)
    for j in range(PEER_TOPK):
        m = jnp.max(s, axis=0, keepdims=True)
        am = jnp.min(jnp.where(s == m, key_id, big), axis=0, keepdims=True)
        vals = jnp.where(row16 == j, m, vals)
        ids = jnp.where(row16 == j, am, ids)
        s = jnp.where(key_id == am, -jnp.inf, s)
    return vals, ids


_PAIR_GROUPS = ((0, 0, 8), (0, 8, 8), (1, 0, 8), (2, 0, 5), (3, 0, 4), (4, 0, 3), (5, 0, 2), (6, 0, 2), (7, 0, 2))


def _route_head(s0, i0, s1, i1):
    sub = lax.broadcasted_iota(jnp.int32, (VREG_SUBLANES, VREG_LANES), 0)
    cands, flat, eids = [], [], []
    for a, b0, nb in _PAIR_GROUPS:
        c = s0[a:a + 1] + s1[b0:b0 + VREG_SUBLANES]
        cands.append(jnp.where(sub < nb, c, -jnp.inf) if nb < VREG_SUBLANES else c)
        flat.append(a * PEER_TOPK + b0 + sub)
        eids.append(i0[a:a + 1] * N_KEYS + i1[b0:b0 + VREG_SUBLANES])
    cands.append(s0[VREG_SUBLANES:] + s1[0:1])
    flat.append((sub + VREG_SUBLANES) * PEER_TOPK)
    eids.append(i0[VREG_SUBLANES:] * N_KEYS + i1[0:1])
    cand = jnp.concatenate(cands, axis=0)
    flat = jnp.concatenate(flat, axis=0)
    eid = jnp.concatenate(eids, axis=0)
    row16 = lax.broadcasted_iota(jnp.int32, (PEER_TOPK, VREG_LANES), 0)
    sc = jnp.zeros((PEER_TOPK, VREG_LANES), jnp.float32)
    sel = jnp.zeros((PEER_TOPK, VREG_LANES), jnp.int32)
    big = jnp.int32(2 ** 30)
    for j in range(PEER_TOPK):
        m = jnp.max(cand, axis=0, keepdims=True)
        am = jnp.min(jnp.where(cand == m, flat, big), axis=0, keepdims=True)
        hit = flat == am
        e = jnp.max(jnp.where(hit, eid, -1), axis=0, keepdims=True)
        sc = jnp.where(row16 == j, m, sc)
        sel = jnp.where(row16 == j, e, sel)
        cand = jnp.where(hit, -jnp.inf, cand)
    p = jnp.exp(sc - sc[0:1])
    return sel, p / jnp.sum(p, axis=0, keepdims=True)


def _route_kernel(x_ref, wq_ref, sk_ref, rows_ref, g_ref, q_ref):
    q = jnp.dot(x_ref[...].astype(jnp.bfloat16), wq_ref[...], preferred_element_type=jnp.float32)
    q_ref[...] = q.astype(jnp.bfloat16)
    key_id = lax.broadcasted_iota(jnp.int32, (N_KEYS, VREG_LANES), 0)

    def head(h, carry):
        tops = []
        for c in range(2):
            hc = h * 2 + c
            qhc = q_ref[:, pl.ds(pl.multiple_of(hc * PEER_HALF, PEER_HALF), PEER_HALF)]
            s = lax.dot_general(sk_ref[hc], qhc, _NT_DIMS, preferred_element_type=jnp.float32)
            tops.append([_top16_rows(s[:, j * VREG_LANES:(j + 1) * VREG_LANES], key_id)
                         for j in range(ROUTE_TM // VREG_LANES)])
        r0 = pl.multiple_of(h * PEER_TOPK, PEER_TOPK)
        for j in range(ROUTE_TM // VREG_LANES):
            (s0, i0), (s1, i1) = tops[0][j], tops[1][j]
            sel, g = _route_head(s0, i0, s1, i1)
            rows_ref[pl.ds(r0, PEER_TOPK), j * VREG_LANES:(j + 1) * VREG_LANES] = sel * EXPERT_ROWS
            g_ref[pl.ds(r0, PEER_TOPK), j * VREG_LANES:(j + 1) * VREG_LANES] = g
        return carry

    lax.fori_loop(0, PEER_HEADS, head, 0)


def peer_route(x, w_query, sub_keys):
    T = x.shape[0]
    wq = w_query.astype(jnp.bfloat16)
    sk = sub_keys.reshape(PEER_HEADS * 2, N_KEYS, PEER_HALF).astype(jnp.bfloat16)
    out_blk = pl.BlockSpec((PEER_SEL, ROUTE_TM), lambda i: (0, i))
    rows, g = pl.pallas_call(
        _route_kernel,
        grid=(T // ROUTE_TM,),
        in_specs=[
            pl.BlockSpec((ROUTE_TM, D_MODEL), lambda i: (i, 0)),
            pl.BlockSpec(wq.shape, lambda i: (0, 0)),
            pl.BlockSpec(sk.shape, lambda i: (0, 0, 0)),
        ],
        out_specs=[out_blk, out_blk],
        out_shape=[jax.ShapeDtypeStruct((PEER_SEL, T), jnp.int32),
                   jax.ShapeDtypeStruct((PEER_SEL, T), jnp.float32)],
        scratch_shapes=[pltpu.VMEM((ROUTE_TM, PEER_HEADS * PEER_QDIM), jnp.bfloat16)],
        compiler_params=pltpu.CompilerParams(dimension_semantics=("arbitrary",),
                                             vmem_limit_bytes=VMEM_LIMIT_BYTES),
        name="peer_route",
    )(x, wq, sk)
    return rows.T, g.T


def encoder_layer(x, mem, rel_bias, w_in, b_in, conv_w, conv_b, attn_sink, f_w_in, f_w_mid, f_b, f_freq,
                  f_w_out, hyena_bias, w_mem_kv, w_branch, w_out, ln1_g, ln1_b, w_query, sub_keys,
                  expert_u, expert_v, ln2_g, ln2_b):
    B, S, _ = x.shape
    T = B * S
    xf = x.reshape(T, D_MODEL)
    proj = lambda lo, hi, **kw: linear(xf, w_in[:, lo:hi], b_in[lo:hi], **kw)
    qkv = proj(0, O_HY).reshape(B, S, O_HY)
    hy = proj(O_HY, O_MQ).reshape(B, S, O_MQ - O_HY)
    q_m = proj(O_MQ, O_GATE).reshape(B, S, MEM_WIDTH)
    gates = proj(O_GATE, IN_WIDTH, tn=(IN_WIDTH - O_GATE) // 2, sigmoid=True)
    M = mem.shape[1]
    kv = linear(mem.reshape(B * M, D_MODEL), w_mem_kv, jnp.zeros((2 * MEM_WIDTH,), jnp.float32), tm=B * M)
    att = window_attention_pallas(qkv, window_bias_table(rel_bias), attn_sink)
    hyo = hyena_mixer_pallas(hy, conv_w, conv_b, f_w_in, f_w_mid, f_b, f_freq, f_w_out, hyena_bias)
    mat = memory_attention_pallas(q_m, kv.reshape(B, M, 2 * MEM_WIDTH))
    x1 = merge_norm(att.reshape(T, ATT_WIDTH), hyo.reshape(T, HYENA_WIDTH), mat.reshape(T, MEM_WIDTH), gates, xf,
                    w_branch, w_out, ln1_g, ln1_b)
    rows, g = peer_route(x1, w_query, sub_keys)
    peer = peer_experts(x1, rows, g, pack_expert_table(expert_u), pack_expert_table(expert_v))
    return residual_norm(x1, peer, ln2_g, ln2_b).reshape(B, S, D_MODEL)


def kernel(x_prompt, x_sample, mem_prompt, mem_sample, rel_bias, w_in, b_in, conv_w, conv_b, attn_sink,
           f_w_in, f_w_mid, f_b, f_freq, f_w_out, hyena_bias, w_mem_kv, w_branch, w_out, ln1_g, ln1_b,
           w_query, sub_keys, expert_u, expert_v, ln2_g, ln2_b):
    nb = x_prompt.shape[0]
    x = jnp.concatenate([x_prompt, x_sample], axis=0)
    mem = jnp.concatenate([mem_prompt, mem_sample], axis=0)
    for l in range(DEPTH):
        x = encoder_layer(x, mem, rel_bias, w_in[l], b_in[l], conv_w[l], conv_b[l], attn_sink[l],
                          f_w_in[l], f_w_mid[l], f_b[l], f_freq[l], f_w_out[l], hyena_bias[l],
                          w_mem_kv[l], w_branch[l], w_out[l], ln1_g[l], ln1_b[l], w_query[l],
                          sub_keys[l], expert_u[l], expert_v[l], ln2_g[l], ln2_b[l])
    return (x[:nb], x[nb:])
```

```python
import functools
import math

import jax
import jax.numpy as jnp
from jax import lax
from jax.experimental import pallas as pl
from jax.experimental.pallas import tpu as pltpu

D_MODEL = 1024
DEPTH = 2
N_MEM = 256
ATT_HEADS = 8
ATT_KV_HEADS = 2
ATT_HEAD_DIM = 64
ATT_WIDTH = ATT_HEADS * ATT_HEAD_DIM
ATT_KV_WIDTH = ATT_KV_HEADS * ATT_HEAD_DIM
WINDOW = 128
BLOCK = 128
N_BUCKETS = 32
MAX_DISTANCE = 128
HYENA_WIDTH = 512
HYENA_ORDER = 2
SHORT_CONV = 3
FILTER_EMB = 33
FILTER_BANDS = (FILTER_EMB - 1) // 2
FILTER_HIDDEN = 64
FILTER_INNER = 2
FAST_DECAY_PCT = 0.3
SLOW_DECAY_PCT = 1.5
DECAY_TARGET = 1e-2
MEM_HEADS = 4
MEM_HEAD_DIM = 128
MEM_WIDTH = MEM_HEADS * MEM_HEAD_DIM
N_BRANCH = 3
BRANCH_WIDTH = 512
PEER_HEADS = 8
N_KEYS = 128
N_EXPERTS = N_KEYS * N_KEYS
PEER_TOPK = 16
PEER_HALF = 128
PEER_QDIM = 2 * PEER_HALF
PEER_BLOCK = 128
O_K = ATT_WIDTH
O_V = O_K + ATT_KV_WIDTH
O_HY = O_V + ATT_KV_WIDTH
O_MQ = O_HY + (HYENA_ORDER + 1) * HYENA_WIDTH
O_GATE = O_MQ + MEM_WIDTH
IN_WIDTH = O_GATE + N_BRANCH * D_MODEL
ALPHA = (2 * DEPTH) ** 0.25
BETA = (8 * DEPTH) ** -0.25
LN_EPS = 1e-5
NEG_INF = -1e30

VMEM_LIMIT_BYTES = 56 * 1024 * 1024


def _linear_kernel(x_ref, w_ref, b_ref, o_ref, *, sigmoid):
    x = x_ref[...].astype(jnp.bfloat16)
    acc = jnp.dot(x, w_ref[...], preferred_element_type=jnp.float32) + b_ref[...]
    o_ref[...] = jax.nn.sigmoid(acc) if sigmoid else acc


def linear(x, w, b, *, tm=512, tn=None, sigmoid=False):
    T, K = x.shape
    N = w.shape[1]
    tn = N if tn is None else tn
    wb = w.astype(jnp.bfloat16)
    return pl.pallas_call(
        functools.partial(_linear_kernel, sigmoid=sigmoid),
        grid=(N // tn, T // tm),
        in_specs=[
            pl.BlockSpec((tm, K), lambda j, i: (i, 0)),
            pl.BlockSpec((K, tn), lambda j, i: (0, j)),
            pl.BlockSpec((1, tn), lambda j, i: (0, j)),
        ],
        out_specs=pl.BlockSpec((tm, tn), lambda j, i: (i, j)),
        out_shape=jax.ShapeDtypeStruct((T, N), jnp.float32),
        compiler_params=pltpu.CompilerParams(
            dimension_semantics=("arbitrary", "arbitrary"),
            vmem_limit_bytes=VMEM_LIMIT_BYTES),
        name="linear",
    )(x, wb, b.reshape(1, N))


ATT_TQ = 512
ATT_KEYS = 3 * BLOCK
_NT_DIMS = (((1,), (1,)), ((), ()))


def t5_bucket(rel):
    nb = N_BUCKETS // 2
    max_exact = nb // 2
    ret = jnp.where(rel > 0, nb, 0)
    n = jnp.abs(rel)
    nf = jnp.maximum(n, 1).astype(jnp.float32)
    large = max_exact + (jnp.log(nf / max_exact) / math.log(MAX_DISTANCE / max_exact)
                         * (nb - max_exact)).astype(jnp.int32)
    large = jnp.minimum(large, nb - 1)
    return ret + jnp.where(n < max_exact, n, large)


def window_bias_table(rel_bias):
    rel = (jnp.arange(ATT_KEYS) - BLOCK)[None, :] - jnp.arange(BLOCK)[:, None]
    bias = rel_bias[t5_bucket(rel)].astype(jnp.float32).transpose(2, 0, 1)
    return jnp.where((jnp.abs(rel) <= WINDOW)[None], bias, NEG_INF)


def _col_params():
    return pltpu.CompilerParams(dimension_semantics=("arbitrary", "arbitrary"), vmem_limit_bytes=VMEM_LIMIT_BYTES)


def _window_attn_kernel(sink_ref, q_ref, kp_ref, kc_ref, kn_ref, vp_ref, vc_ref, vn_ref, bias_ref, o_ref, *, n_steps):
    i = pl.program_id(1)
    bf = jnp.bfloat16
    q = q_ref[0].astype(bf)
    k_all = jnp.concatenate([kp_ref[0], kc_ref[0], kn_ref[0]], axis=0).astype(bf)
    v_all = jnp.concatenate([vp_ref[0], vc_ref[0], vn_ref[0]], axis=0).astype(bf)
    col = lax.broadcasted_iota(jnp.int32, (BLOCK, ATT_KEYS), 1)
    n_blk = ATT_TQ // BLOCK
    scale = ATT_HEAD_DIM ** -0.5
    group = ATT_HEADS // ATT_KV_HEADS
    for j in range(n_blk):
        kj = k_all[j * BLOCK:j * BLOCK + ATT_KEYS]
        vj = v_all[j * BLOCK:j * BLOCK + ATT_KEYS]
        off_seq = None
        if j == 0:
            off_seq = (i == 0) & (col < BLOCK)
        if j == n_blk - 1:
            last = (i == n_steps - 1) & (col >= 2 * BLOCK)
            off_seq = last if off_seq is None else off_seq | last
        for h in range(ATT_HEADS):
            hk = h // group
            qh = q[j * BLOCK:(j + 1) * BLOCK, h * ATT_HEAD_DIM:(h + 1) * ATT_HEAD_DIM]
            kh = kj[:, hk * ATT_HEAD_DIM:(hk + 1) * ATT_HEAD_DIM]
            s = lax.dot_general(qh, kh, _NT_DIMS, preferred_element_type=jnp.float32) * scale + bias_ref[h]
            if off_seq is not None:
                s = jnp.where(off_seq, NEG_INF, s)
            sink = sink_ref[h]
            m = jnp.maximum(jnp.max(s, axis=-1, keepdims=True), sink)
            p = jnp.exp(s - m)
            denom = jnp.sum(p, axis=-1, keepdims=True) + jnp.exp(sink - m)
            oh = jnp.dot((p / denom).astype(bf), vj[:, hk * ATT_HEAD_DIM:(hk + 1) * ATT_HEAD_DIM],
                         preferred_element_type=jnp.float32)
            o_ref[0, j * BLOCK:(j + 1) * BLOCK, h * ATT_HEAD_DIM:(h + 1) * ATT_HEAD_DIM] = oh.astype(o_ref.dtype)


def window_attention_pallas(qkv, bias, sink):
    B, S, _ = qkv.shape
    n_steps = S // ATT_TQ
    per = ATT_TQ // BLOCK
    last_blk = S // BLOCK - 1
    k_col, v_col = O_K // ATT_KV_WIDTH, O_V // ATT_KV_WIDTH

    def edge(col, nxt):
        if nxt:
            return pl.BlockSpec((1, BLOCK, ATT_KV_WIDTH), lambda b, i: (b, jnp.minimum((i + 1) * per, last_blk), col))
        return pl.BlockSpec((1, BLOCK, ATT_KV_WIDTH), lambda b, i: (b, jnp.maximum(i * per - 1, 0), col))

    cur = lambda col: pl.BlockSpec((1, ATT_TQ, ATT_KV_WIDTH), lambda b, i: (b, i, col))
    return pl.pallas_call(
        functools.partial(_window_attn_kernel, n_steps=n_steps),
        grid=(B, n_steps),
        in_specs=[pl.BlockSpec(memory_space=pltpu.SMEM),
                  pl.BlockSpec((1, ATT_TQ, ATT_WIDTH), lambda b, i: (b, i, 0)),
                  edge(k_col, False), cur(k_col), edge(k_col, True),
                  edge(v_col, False), cur(v_col), edge(v_col, True),
                  pl.BlockSpec(bias.shape, lambda b, i: (0, 0, 0))],
        out_specs=pl.BlockSpec((1, ATT_TQ, ATT_WIDTH), lambda b, i: (b, i, 0)),
        out_shape=jax.ShapeDtypeStruct((B, S, ATT_WIDTH), jnp.bfloat16),
        compiler_params=_col_params(),
        name="window_attn",
    )(sink.astype(jnp.float32), qkv, qkv, qkv, qkv, qkv, qkv, qkv, bias)


MEM_TQ = 512


def _mem_attn_kernel(q_ref, kv_ref, o_ref):
    bf = jnp.bfloat16
    q = q_ref[0].astype(bf)
    kv = kv_ref[0].astype(bf)
    scale = MEM_HEAD_DIM ** -0.5
    for h in range(MEM_HEADS):
        sl = slice(h * MEM_HEAD_DIM, (h + 1) * MEM_HEAD_DIM)
        s = lax.dot_general(q[:, sl], kv[:, sl], _NT_DIMS, preferred_element_type=jnp.float32) * scale
        p = jnp.exp(s - jnp.max(s, axis=-1, keepdims=True))
        p = (p / jnp.sum(p, axis=-1, keepdims=True)).astype(bf)
        vh = kv[:, MEM_WIDTH + h * MEM_HEAD_DIM:MEM_WIDTH + (h + 1) * MEM_HEAD_DIM]
        o_ref[0, :, sl] = jnp.dot(p, vh, preferred_element_type=jnp.float32).astype(o_ref.dtype)


def memory_attention_pallas(q, kv):
    B, S, _ = q.shape
    M = kv.shape[1]
    return pl.pallas_call(
        _mem_attn_kernel,
        grid=(B, S // MEM_TQ),
        in_specs=[pl.BlockSpec((1, MEM_TQ, MEM_WIDTH), lambda b, i: (b, i, 0)),
                  pl.BlockSpec((1, M, 2 * MEM_WIDTH), lambda b, i: (b, 0, 0))],
        out_specs=pl.BlockSpec((1, MEM_TQ, MEM_WIDTH), lambda b, i: (b, i, 0)),
        out_shape=jax.ShapeDtypeStruct((B, S, MEM_WIDTH), jnp.bfloat16),
        compiler_params=_col_params(),
        name="mem_attn",
    )(q, kv)


MERGE_TM = 256


def _layer_norm_rows(y, g, b):
    mu = jnp.mean(y, axis=-1, keepdims=True)
    d = y - mu
    var = jnp.mean(d * d, axis=-1, keepdims=True)
    return d * lax.rsqrt(var + LN_EPS) * g + b


def _merge_kernel(a_ref, h_ref, m_ref, g_ref, x_ref, wb_ref, wo_ref, lg_ref, lb_ref, o_ref):
    bf = jnp.bfloat16
    f32 = jnp.float32
    merged = g_ref[:, 0:D_MODEL] * jnp.dot(a_ref[...].astype(bf), wb_ref[0], preferred_element_type=f32)
    merged = merged + g_ref[:, D_MODEL:2 * D_MODEL] * jnp.dot(h_ref[...].astype(bf), wb_ref[1], preferred_element_type=f32)
    merged = merged + g_ref[:, 2 * D_MODEL:] * jnp.dot(m_ref[...].astype(bf), wb_ref[2], preferred_element_type=f32)
    y = ALPHA * x_ref[...] + jnp.dot(merged.astype(bf), wo_ref[...], preferred_element_type=f32)
    o_ref[...] = _layer_norm_rows(y, lg_ref[...], lb_ref[...])


def merge_norm(att, hy, mem, gates, x, w_branch, w_out, ln_g, ln_b):
    T = x.shape[0]
    bf = jnp.bfloat16
    rows = lambda w: pl.BlockSpec((MERGE_TM, w), lambda i: (i, 0))
    whole = lambda a: pl.BlockSpec(a.shape, lambda i: (0,) * a.ndim)
    wb, wo = w_branch.astype(bf), w_out.astype(bf)
    lg, lb = ln_g.reshape(1, D_MODEL), ln_b.reshape(1, D_MODEL)
    return pl.pallas_call(
        _merge_kernel,
        grid=(T // MERGE_TM,),
        in_specs=[rows(BRANCH_WIDTH), rows(BRANCH_WIDTH), rows(BRANCH_WIDTH), rows(N_BRANCH * D_MODEL), rows(D_MODEL),
                  whole(wb), whole(wo), whole(lg), whole(lb)],
        out_specs=rows(D_MODEL),
        out_shape=jax.ShapeDtypeStruct((T, D_MODEL), jnp.float32),
        compiler_params=pltpu.CompilerParams(dimension_semantics=("arbitrary",), vmem_limit_bytes=VMEM_LIMIT_BYTES),
        name="merge_norm",
    )(att, hy, mem, gates, x, wb, wo, lg, lb)


def _residual_norm_kernel(x_ref, r_ref, lg_ref, lb_ref, o_ref):
    o_ref[...] = _layer_norm_rows(ALPHA * x_ref[...] + r_ref[...], lg_ref[...], lb_ref[...])


def residual_norm(x, r, ln_g, ln_b):
    T = x.shape[0]
    tm = 512
    rows = pl.BlockSpec((tm, D_MODEL), lambda i: (i, 0))
    one = pl.BlockSpec((1, D_MODEL), lambda i: (0, 0))
    return pl.pallas_call(
        _residual_norm_kernel,
        grid=(T // tm,),
        in_specs=[rows, rows, one, one],
        out_specs=rows,
        out_shape=jax.ShapeDtypeStruct((T, D_MODEL), jnp.float32),
        compiler_params=pltpu.CompilerParams(dimension_semantics=("arbitrary",), vmem_limit_bytes=VMEM_LIMIT_BYTES),
        name="residual_norm",
    )(x, r, ln_g.reshape(1, D_MODEL), ln_b.reshape(1, D_MODEL))


FFT_N1 = 256
FFT_N2 = 128
FFT_N = FFT_N1 * FFT_N2
HY_COLS = FFT_N2 * HYENA_WIDTH
LMUL_TN = 2048
KB_K1 = 8
FILT_TM = 512


def _dft_constants():
    f32 = jnp.float32
    n1 = jnp.arange(FFT_N1, dtype=jnp.int32)
    ang1 = ((n1[:, None] * n1[None, :]) % FFT_N1).astype(f32) * f32(2.0 * math.pi / FFT_N1)
    fr, fi = jnp.cos(ang1), -jnp.sin(ang1)
    h = FFT_N1 // 2
    f_fwd = jnp.concatenate([jnp.concatenate([fr[:, :h], -fi[:, :h]], axis=1),
                             jnp.concatenate([fi[:, :h], fr[:, :h]], axis=1)], axis=0)
    f_inv = f_fwd.T * f32(1.0 / FFT_N)
    f_real = jnp.concatenate([fr, fi], axis=0)
    k2 = jnp.arange(FFT_N2, dtype=jnp.int32)
    k = n1[:, None, None] + FFT_N1 * k2[None, :, None]
    ang2 = ((k * k2[None, None, :]) % FFT_N).astype(f32) * f32(2.0 * math.pi / FFT_N)
    gr, gi = jnp.cos(ang2), -jnp.sin(ang2)
    g = jnp.concatenate([jnp.concatenate([gr, -gi], axis=2), jnp.concatenate([gi, gr], axis=2)], axis=1)
    bf = jnp.bfloat16
    return f_fwd.astype(bf), f_inv.astype(bf), f_real.astype(bf), g.astype(bf), g.transpose(0, 2, 1).astype(bf)


def _lmul_pair_kernel(l_ref, xr_ref, xi_ref, o_ref):
    x = jnp.concatenate([xr_ref[0], xi_ref[0]], axis=0).astype(jnp.bfloat16)
    o_ref[0] = jnp.dot(l_ref[...], x, preferred_element_type=jnp.float32).astype(o_ref.dtype)


def _lmul_kernel(l_ref, x_ref, o_ref):
    o_ref[0] = jnp.dot(l_ref[...], x_ref[0].astype(jnp.bfloat16),
                       preferred_element_type=jnp.float32).astype(o_ref.dtype)


def _lmul_gate_kernel(l_ref, a_ref, zr_ref, zi_ref, pr_ref, pi_ref, b_ref, o_ref):
    y = jnp.dot(l_ref[...], a_ref[0], preferred_element_type=jnp.float32)
    h = FFT_N1 // 2
    o_ref[0, :h] = pr_ref[0] * (y[:h] + zr_ref[0] * b_ref[...])
    o_ref[0, h:] = pi_ref[0] * (y[h:] + zi_ref[0] * b_ref[...])


def dft_stage1_pairs(f_fwd, z):
    nb, h, cols = z.shape
    npair = (nb + 1) // 2
    last = nb - 1
    return pl.pallas_call(
        _lmul_pair_kernel,
        grid=(npair, cols // LMUL_TN),
        in_specs=[
            pl.BlockSpec(f_fwd.shape, lambda p, j: (0, 0)),
            pl.BlockSpec((1, h, LMUL_TN), lambda p, j: (jnp.minimum(2 * p, last), 0, j)),
            pl.BlockSpec((1, h, LMUL_TN), lambda p, j: (jnp.minimum(2 * p + 1, last), 0, j)),
        ],
        out_specs=pl.BlockSpec((1, 2 * FFT_N1, LMUL_TN), lambda p, j: (p, 0, j)),
        out_shape=jax.ShapeDtypeStruct((npair, 2 * FFT_N1, cols), jnp.bfloat16),
        compiler_params=_col_params(),
        name="dft_stage1",
    )(f_fwd, z, z)


def dft_stage1_real(f_real, x):
    _, h, cols = x.shape
    return pl.pallas_call(
        _lmul_kernel,
        grid=(1, cols // LMUL_TN),
        in_specs=[pl.BlockSpec(f_real.shape, lambda p, j: (0, 0)),
                  pl.BlockSpec((1, h, LMUL_TN), lambda p, j: (p, 0, j))],
        out_specs=pl.BlockSpec((1, 2 * FFT_N1, LMUL_TN), lambda p, j: (p, 0, j)),
        out_shape=jax.ShapeDtypeStruct((1, 2 * FFT_N1, cols), jnp.bfloat16),
        compiler_params=_col_params(),
        name="dft_stage1_real",
    )(f_real, x)


def idft_stage1_gate(f_inv, a, z, p, bias_row):
    npair, _, cols = a.shape
    half = FFT_N1 // 2

    def half_blk(x, r):
        last = x.shape[0] - 1
        return pl.BlockSpec((1, half, LMUL_TN), lambda q, j: (jnp.minimum(2 * q + r, last), 0, j))

    out = pl.pallas_call(
        _lmul_gate_kernel,
        grid=(npair, cols // LMUL_TN),
        in_specs=[pl.BlockSpec(f_inv.shape, lambda q, j: (0, 0)),
                  pl.BlockSpec((1, 2 * FFT_N1, LMUL_TN), lambda q, j: (q, 0, j)),
                  half_blk(z, 0), half_blk(z, 1), half_blk(p, 0), half_blk(p, 1),
                  pl.BlockSpec((1, LMUL_TN), lambda q, j: (0, 0))],
        out_specs=pl.BlockSpec((1, FFT_N1, LMUL_TN), lambda q, j: (q, 0, j)),
        out_shape=jax.ShapeDtypeStruct((npair, FFT_N1, cols), jnp.float32),
        compiler_params=_col_params(),
        name="idft_stage1_gate",
    )(f_inv, a, z, z, p, p, bias_row)
    return out.reshape(2 * npair, half, cols)


def _stage2_conv_kernel(a_ref, g_ref, gt_ref, h_ref, o_ref):
    for j in range(KB_K1):
        x = jnp.concatenate([a_ref[0, 0, j], a_ref[0, 1, j]], axis=0)
        z = jnp.dot(g_ref[j], x, preferred_element_type=jnp.float32)
        zr, zi = z[:FFT_N2], z[FFT_N2:]
        hr, hi = h_ref[0, j], h_ref[1, j]
        w = jnp.concatenate([zr * hr - zi * hi, zr * hi + zi * hr], axis=0).astype(jnp.bfloat16)
        y = jnp.dot(gt_ref[j], w, preferred_element_type=jnp.float32)
        o_ref[0, 0, j] = y[:FFT_N2].astype(o_ref.dtype)
        o_ref[0, 1, j] = y[FFT_N2:].astype(o_ref.dtype)


def stage2_conv(a, g, gt, hf, order):
    npair = a.shape[0]
    C = HYENA_WIDTH
    ablk = pl.BlockSpec((1, 2, KB_K1, FFT_N2, C), lambda p, i: (p, 0, i, 0, 0))
    gblk = pl.BlockSpec((KB_K1, 2 * FFT_N2, 2 * FFT_N2), lambda p, i: (i, 0, 0))
    return pl.pallas_call(
        _stage2_conv_kernel,
        grid=(npair, FFT_N1 // KB_K1),
        in_specs=[ablk, gblk, gblk,
                  pl.BlockSpec((2, KB_K1, FFT_N2, C), lambda p, i: (0, i, 0, order))],
        out_specs=ablk,
        out_shape=jax.ShapeDtypeStruct(a.shape, a.dtype),
        compiler_params=_col_params(),
        name="stage2_conv",
    )(a, g, gt, hf)


def _stage2_filter_kernel(a_ref, g_ref, s_ref, o_ref):
    for j in range(KB_K1):
        x = jnp.concatenate([a_ref[0, j], a_ref[1, j]], axis=0)
        z = jnp.dot(g_ref[j], x, preferred_element_type=jnp.float32) * s_ref[...]
        o_ref[0, j] = z[:FFT_N2]
        o_ref[1, j] = z[FFT_N2:]


def stage2_filter(a, g, inv_norm):
    W = a.shape[-1]
    blk = pl.BlockSpec((2, KB_K1, FFT_N2, W), lambda i: (0, i, 0, 0))
    return pl.pallas_call(
        _stage2_filter_kernel,
        grid=(FFT_N1 // KB_K1,),
        in_specs=[blk, pl.BlockSpec((KB_K1, 2 * FFT_N2, 2 * FFT_N2), lambda i: (i, 0, 0)),
                  pl.BlockSpec((1, W), lambda i: (0, 0))],
        out_specs=blk,
        out_shape=jax.ShapeDtypeStruct(a.shape, jnp.float32),
        compiler_params=pltpu.CompilerParams(dimension_semantics=("arbitrary",), vmem_limit_bytes=VMEM_LIMIT_BYTES),
        name="stage2_filter",
    )(a, g, inv_norm)


def _filter_gen_kernel(bands_ref, w0_ref, wc_ref, ws_ref, wmid_ref, fb_ref, ff_ref, wdir_ref, wbwd_ref,
                       adelta_ref, two_ref, norm_ref, *, seq):
    i = pl.program_id(0)
    bf = jnp.bfloat16
    row = i * FILT_TM + lax.broadcasted_iota(jnp.int32, (FILT_TM, 1), 0)
    pos_i = jnp.where(row < seq, row, 2 * seq - row)
    pos = pos_i.astype(jnp.float32)
    t = pos / max(seq - 1, 1)
    w = (jnp.float32(2.0 * math.pi) * pos) / seq
    ang = w * bands_ref[...]
    pre = (jnp.dot(jnp.cos(ang).astype(bf), wc_ref[...], preferred_element_type=jnp.float32)
           + jnp.dot((-jnp.sin(ang)).astype(bf), ws_ref[...], preferred_element_type=jnp.float32)
           + t.astype(bf).astype(jnp.float32) * w0_ref[...])
    h = jnp.sin(ff_ref[0:1, :] * (pre + fb_ref[0:1, :]))
    for m in range(FILTER_INNER):
        pre = jnp.dot(h.astype(bf), wmid_ref[m], preferred_element_type=jnp.float32)
        h = jnp.sin(ff_ref[m + 1:m + 2, :] * (pre + fb_ref[m + 1:m + 2, :]))
    hb = h.astype(bf)
    decay = jnp.exp(-t * adelta_ref[...])
    out = jnp.dot(hb, wdir_ref[0], preferred_element_type=jnp.float32) * decay
    out = jnp.where(pos_i < seq, out, 0.0)

    @pl.when(i == 0)
    def _():
        extra = jnp.dot(hb, wbwd_ref[...], preferred_element_type=jnp.float32) * decay
        first = jnp.where(row == 0, out + extra, out)
        two_ref[...] = first
        norm_ref[...] = jnp.sum(jnp.abs(first), axis=0, keepdims=True)

    @pl.when(i != 0)
    def _():
        two_ref[...] = out
        norm_ref[...] += jnp.sum(jnp.abs(out), axis=0, keepdims=True)


def hyena_filter_rows(seq, f_w_in, f_w_mid, f_b, f_freq, f_w_out):
    f32, bf = jnp.float32, jnp.bfloat16
    C, H = HYENA_WIDTH, FILTER_HIDDEN
    bands = jnp.linspace(1e-4, FILTER_BANDS - 1, FILTER_BANDS, dtype=f32)
    bands = jnp.pad(bands, (0, VREG_LANES - FILTER_BANDS)).reshape(1, VREG_LANES)
    w_in = f_w_in.astype(bf)
    w0 = w_in[0:1].astype(f32)
    pad_rows = ((0, VREG_LANES - FILTER_BANDS), (0, 0))
    wc = jnp.pad(w_in[1:1 + FILTER_BANDS], pad_rows)
    ws = jnp.pad(w_in[1 + FILTER_BANDS:], pad_rows)
    w_out = f_w_out.astype(bf).reshape(H, HYENA_ORDER, 2, C)
    wdir = jnp.stack([w_out[:, :, 0].reshape(H, HYENA_ORDER * C), w_out[:, :, 1].reshape(H, HYENA_ORDER * C)])
    max_decay = math.log(DECAY_TARGET) / FAST_DECAY_PCT
    min_decay = math.log(DECAY_TARGET) / SLOW_DECAY_PCT
    adelta = jnp.abs(jnp.linspace(min_decay, max_decay, C, dtype=f32))
    adelta = jnp.tile(adelta, HYENA_ORDER).reshape(1, HYENA_ORDER * C)
    n_tiles = 2 * seq // FILT_TM
    half_tiles = seq // FILT_TM
    whole = lambda a: pl.BlockSpec(a.shape, lambda i: (0,) * a.ndim)
    wmid = f_w_mid.astype(bf)
    fb, ff = f_b.astype(f32), f_freq.astype(f32)
    return pl.pallas_call(
        functools.partial(_filter_gen_kernel, seq=seq),
        grid=(n_tiles,),
        in_specs=[whole(bands), whole(w0), whole(wc), whole(ws), whole(wmid), whole(fb), whole(ff),
                  pl.BlockSpec((1, H, HYENA_ORDER * C), lambda i: (i // half_tiles, 0, 0)),
                  pl.BlockSpec((None, H, HYENA_ORDER * C), lambda i: (1, 0, 0)),
                  whole(adelta)],
        out_specs=[pl.BlockSpec((FILT_TM, HYENA_ORDER * C), lambda i: (i, 0)),
                   pl.BlockSpec((1, HYENA_ORDER * C), lambda i: (0, 0))],
        out_shape=[jax.ShapeDtypeStruct((2 * seq, HYENA_ORDER * C), f32),
                   jax.ShapeDtypeStruct((1, HYENA_ORDER * C), f32)],
        compiler_params=pltpu.CompilerParams(dimension_semantics=("arbitrary",), vmem_limit_bytes=VMEM_LIMIT_BYTES),
        name="hyena_filter_gen",
    )(bands, w0, wc, ws, wmid, fb, ff, wdir, wdir, adelta)


def _short_conv_kernel(x_ref, w_ref, b_ref, o_ref):
    x = x_ref[0]
    n = x.shape[0]
    t = lax.broadcasted_iota(jnp.int32, x.shape, 0)
    prev = jnp.where(t == 0, 0.0, pltpu.roll(x, 1, 0))
    nxt = jnp.where(t == n - 1, 0.0, pltpu.roll(x, n - 1, 0))
    o_ref[0] = ((b_ref[...] + prev * w_ref[0:1, :]) + x * w_ref[1:2, :]) + nxt * w_ref[2:3, :]


def short_conv_pallas(u, w, b):
    B, L, W = u.shape
    tc = VREG_LANES
    return pl.pallas_call(
        _short_conv_kernel,
        grid=(B, W // tc),
        in_specs=[pl.BlockSpec((1, L, tc), lambda i, j: (i, 0, j)),
                  pl.BlockSpec((SHORT_CONV, tc), lambda i, j: (0, j)),
                  pl.BlockSpec((1, tc), lambda i, j: (0, j))],
        out_specs=pl.BlockSpec((1, L, tc), lambda i, j: (i, 0, j)),
        out_shape=jax.ShapeDtypeStruct((B, L, W), jnp.float32),
        compiler_params=_col_params(),
        name="short_conv",
    )(u, w, b.reshape(1, W))


def hyena_mixer_pallas(u, conv_w, conv_b, f_w_in, f_w_mid, f_b, f_freq, f_w_out, hyena_bias):
    B, L, _ = u.shape
    assert 2 * L == FFT_N
    C = HYENA_WIDTH
    f_fwd, f_inv, f_real, g, gt = _dft_constants()
    two, norm = hyena_filter_rows(L, f_w_in, f_w_mid, f_b, f_freq, f_w_out)
    af = dft_stage1_real(f_real, two.reshape(1, FFT_N1, FFT_N2 * HYENA_ORDER * C))
    af = af.reshape(2, FFT_N1, FFT_N2, HYENA_ORDER * C)
    hf = stage2_filter(af, g, 1.0 / norm)
    uc = short_conv_pallas(u, conv_w, conv_b)
    half = FFT_N1 // 2
    parts = [uc[..., o * C:(o + 1) * C].reshape(B, half, HY_COLS) for o in range(HYENA_ORDER + 1)]
    z = parts[0]
    for o in range(HYENA_ORDER):
        a = dft_stage1_pairs(f_fwd, z)
        npair = a.shape[0]
        a = stage2_conv(a.reshape(npair, 2, FFT_N1, FFT_N2, C), g, gt, hf, o)
        bias_row = jnp.tile(hyena_bias[o], LMUL_TN // C).reshape(1, LMUL_TN)
        z = idft_stage1_gate(f_inv, a.reshape(npair, 2 * FFT_N1, HY_COLS), z, parts[o + 1], bias_row)
    return z[:B].reshape(B, L, C)


PEER_SEL = PEER_HEADS * PEER_TOPK
PEER_TB = 64
VREG_SUBLANES = 8
VREG_LANES = 128
EXPERT_ROWS = 4
TILE_ROWS = 16
V_UNROLL = 16


def pack_expert_table(tab):
    e = tab.shape[0]
    b = lax.bitcast_convert_type(tab.astype(jnp.bfloat16), jnp.uint16).astype(jnp.uint32)
    b = b.reshape(e, 2, EXPERT_ROWS, VREG_LANES)
    w = (b[:, 0] | (b[:, 1] << 16)).reshape(e * EXPERT_ROWS, VREG_LANES)
    return jnp.pad(w, ((0, VREG_SUBLANES), (0, 0)))


def from_tile_rows(y):
    T = y.shape[0]
    return y.reshape(T, EXPERT_ROWS, 2, VREG_LANES).transpose(0, 2, 1, 3).reshape(T, D_MODEL)


def _split_bf16(v):
    hi = v.astype(jnp.bfloat16)
    return hi, (v - hi.astype(jnp.float32)).astype(jnp.bfloat16)


def _gelu_exact(x):
    return 0.5 * x * (1.0 + lax.erf(x * (2.0 ** -0.5)))


_COMBINE_POS = (3, 7, 1, 5, 2, 6, 0, 4)


def _sublane_sums(c, sub):
    mv = (sub & 2) != 0
    e = []
    for c1, c2 in ((c[0], c[1]), (c[2], c[3])):
        e.append(jnp.where(mv, c1 + pltpu.roll(c1, 2, 0), c2 + pltpu.roll(c2, 6, 0)))
    mo = (sub & 1) != 0
    return jnp.where(mo, e[0] + pltpu.roll(e[0], 1, 0), e[1] + pltpu.roll(e[1], 7, 0))


def _peer_u_kernel(idx_ref, x_ref, g_ref, tab_ref, o_ref, act_ref, r_ref):
    sub = lax.broadcasted_iota(jnp.int32, (VREG_SUBLANES, VREG_LANES), 0)

    lo4 = sub < EXPERT_ROWS

    def token(t, carry):
        x8 = x_ref[t]
        xr = pltpu.roll(x8, EXPERT_ROWS, 0)
        x_lo = jnp.where(lo4, x8, xr)
        x_hi = jnp.where(lo4, xr, x8)
        rs = []
        for grp in range(PEER_SEL // VREG_SUBLANES):
            pairs = []
            for i in range(VREG_SUBLANES // 2):
                wa = tab_ref[pl.ds(idx_ref[t, grp * VREG_SUBLANES + _COMBINE_POS[2 * i]], VREG_SUBLANES), :]
                wb = tab_ref[pl.ds(idx_ref[t, grp * VREG_SUBLANES + _COMBINE_POS[2 * i + 1]], VREG_SUBLANES), :]
                w = jnp.where(lo4, wa, pltpu.roll(wb, EXPERT_ROWS, 0))
                lo = lax.bitcast_convert_type(w << 16, jnp.float32)
                hi = lax.bitcast_convert_type(w & jnp.uint32(0xFFFF0000), jnp.float32)
                pairs.append(lo * x_lo + hi * x_hi)
            rs.append(_sublane_sums(pairs, sub))
        r0 = pl.multiple_of(t * PEER_SEL, PEER_SEL)
        r_ref[pl.ds(r0, PEER_SEL), :] = jnp.concatenate(rs, axis=0).astype(jnp.bfloat16)
        return carry

    lax.fori_loop(0, PEER_TB, token, 0)
    ones = jnp.ones((VREG_LANES, VREG_LANES), jnp.bfloat16)
    keep = (lax.broadcasted_iota(jnp.int32, (PEER_SEL, VREG_LANES), 0)
            == lax.broadcasted_iota(jnp.int32, (PEER_SEL, VREG_LANES), 1))
    chunk = VREG_SUBLANES * PEER_SEL
    for c in range(PEER_TB // VREG_SUBLANES):
        s = jnp.dot(r_ref[c * chunk:(c + 1) * chunk, :], ones, preferred_element_type=jnp.float32)
        for j in range(VREG_SUBLANES):
            blk = jnp.where(keep, s[j * PEER_SEL:(j + 1) * PEER_SEL], 0.0)
            act_ref[c * VREG_SUBLANES + j:c * VREG_SUBLANES + j + 1, :] = jnp.sum(blk, axis=0, keepdims=True)
    o_ref[...] = g_ref[...] * _gelu_exact(act_ref[...])


def _peer_v_kernel(idx_ref, coef_ref, e_ref, d_ref, tab_ref, o_ref, chi_ref, clo_ref):
    c_hi, c_lo = _split_bf16(coef_ref[...])
    chi_ref[...] = jnp.dot(c_hi, e_ref[...], preferred_element_type=jnp.float32)
    clo_ref[...] = jnp.dot(c_lo, e_ref[...], preferred_element_type=jnp.float32)

    def token(t, carry):
        cmat = jnp.concatenate([chi_ref[pl.ds(t, 1), :] * d_ref[...], clo_ref[pl.ds(t, 1), :] * d_ref[...]],
                               axis=0).astype(jnp.bfloat16)
        w = jnp.concatenate([pltpu.bitcast(tab_ref[pl.ds(idx_ref[t, k], VREG_SUBLANES), :], jnp.bfloat16)
                             for k in range(PEER_SEL)], axis=0)
        acc = jnp.dot(cmat, w, preferred_element_type=jnp.float32)
        o_ref[t] = acc[:VREG_SUBLANES] + acc[VREG_SUBLANES:]
        return carry

    lax.fori_loop(0, PEER_TB, token, 0, unroll=V_UNROLL)


def peer_experts(x, eidx, g, tab_u, tab_v):
    T = x.shape[0]
    grid = (T // PEER_TB,)
    f32, bf = jnp.float32, jnp.bfloat16
    sel = jnp.arange(PEER_SEL, dtype=jnp.int32)
    col = jnp.arange(PEER_SEL * TILE_ROWS, dtype=jnp.int32)
    expand = (col[None, :] // TILE_ROWS == sel[:, None]).astype(bf)
    diag = (col[None, :] % TILE_ROWS == jnp.arange(VREG_SUBLANES, dtype=jnp.int32)[:, None]).astype(f32)
    smem_blk = pl.BlockSpec((PEER_TB, PEER_SEL), lambda i: (i, 0), memory_space=pltpu.SMEM)
    vec_blk = pl.BlockSpec((PEER_TB, PEER_SEL), lambda i: (i, 0))
    row_blk = pl.BlockSpec((PEER_TB, VREG_SUBLANES, VREG_LANES), lambda i: (i, 0, 0))
    whole = lambda a: pl.BlockSpec(a.shape, lambda i: (0,) * a.ndim)
    tab_spec = pl.BlockSpec(memory_space=pltpu.VMEM)
    params = pltpu.CompilerParams(dimension_semantics=("arbitrary",), vmem_limit_bytes=VMEM_LIMIT_BYTES)
    coef = pl.pallas_call(
        _peer_u_kernel,
        grid=grid,
        in_specs=[smem_blk, row_blk, vec_blk, tab_spec],
        out_specs=vec_blk,
        out_shape=jax.ShapeDtypeStruct((T, PEER_SEL), f32),
        scratch_shapes=[pltpu.VMEM((PEER_TB, PEER_SEL), f32),
                        pltpu.VMEM((PEER_TB * PEER_SEL, VREG_LANES), bf)],
        compiler_params=params,
        name="peer_u",
    )(eidx, x.reshape(T, VREG_SUBLANES, VREG_LANES), g, tab_u)
    out = pl.pallas_call(
        _peer_v_kernel,
        grid=grid,
        in_specs=[smem_blk, vec_blk, whole(expand), whole(diag), tab_spec],
        out_specs=row_blk,
        out_shape=jax.ShapeDtypeStruct((T, VREG_SUBLANES, VREG_LANES), f32),
        scratch_shapes=[pltpu.VMEM((PEER_TB, PEER_SEL * TILE_ROWS), f32),
                        pltpu.VMEM((PEER_TB, PEER_SEL * TILE_ROWS), f32)],
        compiler_params=params,
        name="peer_v",
    )(eidx, coef, expand, diag, tab_v)
    return from_tile_rows(out)


ROUTE_TM = 512


def _top16_rows(s, key_id):
    row16 = lax.broadcasted_iota(jnp.int32, (PEER_TOPK, VREG_LANES), 0)
    vals = jnp.zeros((PEER_TOPK, VREG_LANES), jnp.float32)
    ids = jnp.zeros((PEER_TOPK, VREG_LANES), jnp.float32)
    big = jnp.float32(2 ** 30)
    for j in range(PEER_TOPK):
        m = jnp.max(s, axis=0, keepdims=True)
        am = jnp.min(jnp.where(s == m, key_id, big), axis=0, keepdims=True)
        vals = jnp.where(row16 == j, m, vals)
        ids = jnp.where(row16 == j, am, ids)
        s = jnp.where(key_id == am, -jnp.inf, s)
    return vals, ids


_PAIR_GROUPS = ((0, 0, 8), (0, 8, 8), (1, 0, 8), (2, 0, 5), (3, 0, 4), (4, 0, 3), (5, 0, 2), (6, 0, 2), (7, 0, 2))


def _route_head(s0, i0, s1, i1):
    sub_i = lax.broadcasted_iota(jnp.int32, (VREG_SUBLANES, VREG_LANES), 0)
    sub = sub_i.astype(jnp.float32)
    cands, flat, eids = [], [], []
    for a, b0, nb in _PAIR_GROUPS:
        c = s0[a:a + 1] + s1[b0:b0 + VREG_SUBLANES]
        cands.append(jnp.where(sub_i < nb, c, -jnp.inf) if nb < VREG_SUBLANES else c)
        flat.append(a * PEER_TOPK + b0 + sub)
        eids.append(i0[a:a + 1] * N_KEYS + i1[b0:b0 + VREG_SUBLANES])
    cands.append(s0[VREG_SUBLANES:] + s1[0:1])
    flat.append((sub + VREG_SUBLANES) * PEER_TOPK)
    eids.append(i0[VREG_SUBLANES:] * N_KEYS + i1[0:1])
    cand = jnp.concatenate(cands, axis=0)
    flat = jnp.concatenate(flat, axis=0)
    eid = jnp.concatenate(eids, axis=0)
    row16 = lax.broadcasted_iota(jnp.int32, (PEER_TOPK, VREG_LANES), 0)
    sc = jnp.zeros((PEER_TOPK, VREG_LANES), jnp.float32)
    sel = jnp.zeros((PEER_TOPK, VREG_LANES), jnp.float32)
    big = jnp.float32(2 ** 30)
    for j in range(PEER_TOPK):
        m = jnp.max(cand, axis=0, keepdims=True)
        am = jnp.min(jnp.where(cand == m, flat, big), axis=0, keepdims=True)
        hit = flat == am
        e = jnp.max(jnp.where(hit, eid, -1.0), axis=0, keepdims=True)
        sc = jnp.where(row16 == j, m, sc)
        sel = jnp.where(row16 == j, e, sel)
        cand = jnp.where(hit, -jnp.inf, cand)
    p = jnp.exp(sc - sc[0:1])
    return sel, p / jnp.sum(p, axis=0, keepdims=True)


def _route_kernel(x_ref, wq_ref, sk_ref, rows_ref, g_ref, q_ref):
    q = jnp.dot(x_ref[...].astype(jnp.bfloat16), wq_ref[...], preferred_element_type=jnp.float32)
    q_ref[...] = q.astype(jnp.bfloat16)
    key_id = lax.broadcasted_iota(jnp.int32, (N_KEYS, VREG_LANES), 0).astype(jnp.float32)

    def head(h, carry):
        tops = []
        for c in range(2):
            hc = h * 2 + c
            qhc = q_ref[:, pl.ds(pl.multiple_of(hc * PEER_HALF, PEER_HALF), PEER_HALF)]
            s = lax.dot_general(sk_ref[hc], qhc, _NT_DIMS, preferred_element_type=jnp.float32)
            tops.append([_top16_rows(s[:, j * VREG_LANES:(j + 1) * VREG_LANES], key_id)
                         for j in range(ROUTE_TM // VREG_LANES)])
        r0 = pl.multiple_of(h * PEER_TOPK, PEER_TOPK)
        for j in range(ROUTE_TM // VREG_LANES):
            (s0, i0), (s1, i1) = tops[0][j], tops[1][j]
            sel, g = _route_head(s0, i0, s1, i1)
            rows_ref[pl.ds(r0, PEER_TOPK), j * VREG_LANES:(j + 1) * VREG_LANES] = (sel * EXPERT_ROWS).astype(jnp.int32)
            g_ref[pl.ds(r0, PEER_TOPK), j * VREG_LANES:(j + 1) * VREG_LANES] = g
        return carry

    lax.fori_loop(0, PEER_HEADS, head, 0)


def peer_route(x, w_query, sub_keys):
    T = x.shape[0]
    wq = w_query.astype(jnp.bfloat16)
    sk = sub_keys.reshape(PEER_HEADS * 2, N_KEYS, PEER_HALF).astype(jnp.bfloat16)
    out_blk = pl.BlockSpec((PEER_SEL, ROUTE_TM), lambda i: (0, i))
    rows, g = pl.pallas_call(
        _route_kernel,
        grid=(T // ROUTE_TM,),
        in_specs=[
            pl.BlockSpec((ROUTE_TM, D_MODEL), lambda i: (i, 0)),
            pl.BlockSpec(wq.shape, lambda i: (0, 0)),
            pl.BlockSpec(sk.shape, lambda i: (0, 0, 0)),
        ],
        out_specs=[out_blk, out_blk],
        out_shape=[jax.ShapeDtypeStruct((PEER_SEL, T), jnp.int32),
                   jax.ShapeDtypeStruct((PEER_SEL, T), jnp.float32)],
        scratch_shapes=[pltpu.VMEM((ROUTE_TM, PEER_HEADS * PEER_QDIM), jnp.bfloat16)],
        compiler_params=pltpu.CompilerParams(dimension_semantics=("arbitrary",),
                                             vmem_limit_bytes=VMEM_LIMIT_BYTES),
        name="peer_route",
    )(x, wq, sk)
    return rows.T, g.T


def encoder_layer(x, mem, rel_bias, w_in, b_in, conv_w, conv_b, attn_sink, f_w_in, f_w_mid, f_b, f_freq,
                  f_w_out, hyena_bias, w_mem_kv, w_branch, w_out, ln1_g, ln1_b, w_query, sub_keys,
                  expert_u, expert_v, ln2_g, ln2_b):
    B, S, _ = x.shape
    T = B * S
    xf = x.reshape(T, D_MODEL)
    proj = lambda lo, hi, **kw: linear(xf, w_in[:, lo:hi], b_in[lo:hi], **kw)
    qkv = proj(0, O_HY).reshape(B, S, O_HY)
    hy = proj(O_HY, O_MQ).reshape(B, S, O_MQ - O_HY)
    q_m = proj(O_MQ, O_GATE).reshape(B, S, MEM_WIDTH)
    gates = proj(O_GATE, IN_WIDTH, tn=(IN_WIDTH - O_GATE) // 2, sigmoid=True)
    M = mem.shape[1]
    kv = linear(mem.reshape(B * M, D_MODEL), w_mem_kv, jnp.zeros((2 * MEM_WIDTH,), jnp.float32), tm=B * M)
    att = window_attention_pallas(qkv, window_bias_table(rel_bias), attn_sink)
    hyo = hyena_mixer_pallas(hy, conv_w, conv_b, f_w_in, f_w_mid, f_b, f_freq, f_w_out, hyena_bias)
    mat = memory_attention_pallas(q_m, kv.reshape(B, M, 2 * MEM_WIDTH))
    x1 = merge_norm(att.reshape(T, ATT_WIDTH), hyo.reshape(T, HYENA_WIDTH), mat.reshape(T, MEM_WIDTH), gates, xf,
                    w_branch, w_out, ln1_g, ln1_b)
    rows, g = peer_route(x1, w_query, sub_keys)
    peer = peer_experts(x1, rows, g, pack_expert_table(expert_u), pack_expert_table(expert_v))
    return residual_norm(x1, peer, ln2_g, ln2_b).reshape(B, S, D_MODEL)


def kernel(x_prompt, x_sample, mem_prompt, mem_sample, rel_bias, w_in, b_in, conv_w, conv_b, attn_sink,
           f_w_in, f_w_mid, f_b, f_freq, f_w_out, hyena_bias, w_mem_kv, w_branch, w_out, ln1_g, ln1_b,
           w_query, sub_keys, expert_u, expert_v, ln2_g, ln2_b):
    nb = x_prompt.shape[0]
    x = jnp.concatenate([x_prompt, x_sample], axis=0)
    mem = jnp.concatenate([mem_prompt, mem_sample], axis=0)
    for l in range(DEPTH):
        x = encoder_layer(x, mem, rel_bias, w_in[l], b_in[l], conv_w[l], conv_b[l], attn_sink[l],
                          f_w_in[l], f_w_mid[l], f_b[l], f_freq[l], f_w_out[l], hyena_bias[l],
                          w_mem_kv[l], w_branch[l], w_out[l], ln1_g[l], ln1_b[l], w_query[l],
                          sub_keys[l], expert_u[l], expert_v[l], ln2_g[l], ln2_b[l])
    return (x[:nb], x[nb:])
```

```python
import functools
import math

import jax
import jax.numpy as jnp
from jax import lax
from jax.experimental import pallas as pl
from jax.experimental.pallas import tpu as pltpu

D_MODEL = 1024
DEPTH = 2
N_MEM = 256
ATT_HEADS = 8
ATT_KV_HEADS = 2
ATT_HEAD_DIM = 64
ATT_WIDTH = ATT_HEADS * ATT_HEAD_DIM
ATT_KV_WIDTH = ATT_KV_HEADS * ATT_HEAD_DIM
WINDOW = 128
BLOCK = 128
N_BUCKETS = 32
MAX_DISTANCE = 128
HYENA_WIDTH = 512
HYENA_ORDER = 2
SHORT_CONV = 3
FILTER_EMB = 33
FILTER_BANDS = (FILTER_EMB - 1) // 2
FILTER_HIDDEN = 64
FILTER_INNER = 2
FAST_DECAY_PCT = 0.3
SLOW_DECAY_PCT = 1.5
DECAY_TARGET = 1e-2
MEM_HEADS = 4
MEM_HEAD_DIM = 128
MEM_WIDTH = MEM_HEADS * MEM_HEAD_DIM
N_BRANCH = 3
BRANCH_WIDTH = 512
PEER_HEADS = 8
N_KEYS = 128
N_EXPERTS = N_KEYS * N_KEYS
PEER_TOPK = 16
PEER_HALF = 128
PEER_QDIM = 2 * PEER_HALF
PEER_BLOCK = 128
O_K = ATT_WIDTH
O_V = O_K + ATT_KV_WIDTH
O_HY = O_V + ATT_KV_WIDTH
O_MQ = O_HY + (HYENA_ORDER + 1) * HYENA_WIDTH
O_GATE = O_MQ + MEM_WIDTH
IN_WIDTH = O_GATE + N_BRANCH * D_MODEL
ALPHA = (2 * DEPTH) ** 0.25
BETA = (8 * DEPTH) ** -0.25
LN_EPS = 1e-5
NEG_INF = -1e30

VMEM_LIMIT_BYTES = 56 * 1024 * 1024


def _linear_kernel(x_ref, w_ref, b_ref, o_ref, *, sigmoid):
    x = x_ref[...].astype(jnp.bfloat16)
    acc = jnp.dot(x, w_ref[...], preferred_element_type=jnp.float32) + b_ref[...]
    o_ref[...] = jax.nn.sigmoid(acc) if sigmoid else acc


def linear(x, w, b, *, tm=512, tn=None, sigmoid=False):
    T, K = x.shape
    N = w.shape[1]
    tn = N if tn is None else tn
    wb = w.astype(jnp.bfloat16)
    return pl.pallas_call(
        functools.partial(_linear_kernel, sigmoid=sigmoid),
        grid=(N // tn, T // tm),
        in_specs=[
            pl.BlockSpec((tm, K), lambda j, i: (i, 0)),
            pl.BlockSpec((K, tn), lambda j, i: (0, j)),
            pl.BlockSpec((1, tn), lambda j, i: (0, j)),
        ],
        out_specs=pl.BlockSpec((tm, tn), lambda j, i: (i, j)),
        out_shape=jax.ShapeDtypeStruct((T, N), jnp.float32),
        compiler_params=pltpu.CompilerParams(
            dimension_semantics=("arbitrary", "arbitrary"),
            vmem_limit_bytes=VMEM_LIMIT_BYTES),
        name="linear",
    )(x, wb, b.reshape(1, N))


ATT_TQ = 512
ATT_KEYS = 3 * BLOCK
_NT_DIMS = (((1,), (1,)), ((), ()))


def t5_bucket(rel):
    nb = N_BUCKETS // 2
    max_exact = nb // 2
    ret = jnp.where(rel > 0, nb, 0)
    n = jnp.abs(rel)
    nf = jnp.maximum(n, 1).astype(jnp.float32)
    large = max_exact + (jnp.log(nf / max_exact) / math.log(MAX_DISTANCE / max_exact)
                         * (nb - max_exact)).astype(jnp.int32)
    large = jnp.minimum(large, nb - 1)
    return ret + jnp.where(n < max_exact, n, large)


def window_bias_table(rel_bias):
    rel = (jnp.arange(ATT_KEYS) - BLOCK)[None, :] - jnp.arange(BLOCK)[:, None]
    bias = rel_bias[t5_bucket(rel)].astype(jnp.float32).transpose(2, 0, 1)
    return jnp.where((jnp.abs(rel) <= WINDOW)[None], bias, NEG_INF)


def _col_params():
    return pltpu.CompilerParams(dimension_semantics=("arbitrary", "arbitrary"), vmem_limit_bytes=VMEM_LIMIT_BYTES)


def _window_attn_kernel(sink_ref, q_ref, kp_ref, kc_ref, kn_ref, vp_ref, vc_ref, vn_ref, bias_ref, o_ref, *, n_steps):
    i = pl.program_id(1)
    bf = jnp.bfloat16
    q = q_ref[0].astype(bf)
    k_all = jnp.concatenate([kp_ref[0], kc_ref[0], kn_ref[0]], axis=0).astype(bf)
    v_all = jnp.concatenate([vp_ref[0], vc_ref[0], vn_ref[0]], axis=0).astype(bf)
    col = lax.broadcasted_iota(jnp.int32, (BLOCK, ATT_KEYS), 1)
    n_blk = ATT_TQ // BLOCK
    scale = ATT_HEAD_DIM ** -0.5
    group = ATT_HEADS // ATT_KV_HEADS
    for j in range(n_blk):
        kj = k_all[j * BLOCK:j * BLOCK + ATT_KEYS]
        vj = v_all[j * BLOCK:j * BLOCK + ATT_KEYS]
        off_seq = None
        if j == 0:
            off_seq = (i == 0) & (col < BLOCK)
        if j == n_blk - 1:
            last = (i == n_steps - 1) & (col >= 2 * BLOCK)
            off_seq = last if off_seq is None else off_seq | last
        for h in range(ATT_HEADS):
            hk = h // group
            qh = q[j * BLOCK:(j + 1) * BLOCK, h * ATT_HEAD_DIM:(h + 1) * ATT_HEAD_DIM]
            kh = kj[:, hk * ATT_HEAD_DIM:(hk + 1) * ATT_HEAD_DIM]
            s = lax.dot_general(qh, kh, _NT_DIMS, preferred_element_type=jnp.float32) * scale + bias_ref[h]
            if off_seq is not None:
                s = jnp.where(off_seq, NEG_INF, s)
            sink = sink_ref[h]
            m = jnp.maximum(jnp.max(s, axis=-1, keepdims=True), sink)
            p = jnp.exp(s - m)
            denom = jnp.sum(p, axis=-1, keepdims=True) + jnp.exp(sink - m)
            oh = jnp.dot((p / denom).astype(bf), vj[:, hk * ATT_HEAD_DIM:(hk + 1) * ATT_HEAD_DIM],
                         preferred_element_type=jnp.float32)
            o_ref[0, j * BLOCK:(j + 1) * BLOCK, h * ATT_HEAD_DIM:(h + 1) * ATT_HEAD_DIM] = oh.astype(o_ref.dtype)


def window_attention_pallas(qkv, bias, sink):
    B, S, _ = qkv.shape
    n_steps = S // ATT_TQ
    per = ATT_TQ // BLOCK
    last_blk = S // BLOCK - 1
    k_col, v_col = O_K // ATT_KV_WIDTH, O_V // ATT_KV_WIDTH

    def edge(col, nxt):
        if nxt:
            return pl.BlockSpec((1, BLOCK, ATT_KV_WIDTH), lambda b, i: (b, jnp.minimum((i + 1) * per, last_blk), col))
        return pl.BlockSpec((1, BLOCK, ATT_KV_WIDTH), lambda b, i: (b, jnp.maximum(i * per - 1, 0), col))

    cur = lambda col: pl.BlockSpec((1, ATT_TQ, ATT_KV_WIDTH), lambda b, i: (b, i, col))
    return pl.pallas_call(
        functools.partial(_window_attn_kernel, n_steps=n_steps),
        grid=(B, n_steps),
        in_specs=[pl.BlockSpec(memory_space=pltpu.SMEM),
                  pl.BlockSpec((1, ATT_TQ, ATT_WIDTH), lambda b, i: (b, i, 0)),
                  edge(k_col, False), cur(k_col), edge(k_col, True),
                  edge(v_col, False), cur(v_col), edge(v_col, True),
                  pl.BlockSpec(bias.shape, lambda b, i: (0, 0, 0))],
        out_specs=pl.BlockSpec((1, ATT_TQ, ATT_WIDTH), lambda b, i: (b, i, 0)),
        out_shape=jax.ShapeDtypeStruct((B, S, ATT_WIDTH), jnp.bfloat16),
        compiler_params=_col_params(),
        name="window_attn",
    )(sink.astype(jnp.float32), qkv, qkv, qkv, qkv, qkv, qkv, qkv, bias)


MEM_TQ = 512


def _mem_attn_kernel(q_ref, kv_ref, o_ref):
    bf = jnp.bfloat16
    q = q_ref[0].astype(bf)
    kv = kv_ref[0].astype(bf)
    scale = MEM_HEAD_DIM ** -0.5
    for h in range(MEM_HEADS):
        sl = slice(h * MEM_HEAD_DIM, (h + 1) * MEM_HEAD_DIM)
        s = lax.dot_general(q[:, sl], kv[:, sl], _NT_DIMS, preferred_element_type=jnp.float32) * scale
        p = jnp.exp(s - jnp.max(s, axis=-1, keepdims=True))
        p = (p / jnp.sum(p, axis=-1, keepdims=True)).astype(bf)
        vh = kv[:, MEM_WIDTH + h * MEM_HEAD_DIM:MEM_WIDTH + (h + 1) * MEM_HEAD_DIM]
        o_ref[0, :, sl] = jnp.dot(p, vh, preferred_element_type=jnp.float32).astype(o_ref.dtype)


def memory_attention_pallas(q, kv):
    B, S, _ = q.shape
    M = kv.shape[1]
    return pl.pallas_call(
        _mem_attn_kernel,
        grid=(B, S // MEM_TQ),
        in_specs=[pl.BlockSpec((1, MEM_TQ, MEM_WIDTH), lambda b, i: (b, i, 0)),
                  pl.BlockSpec((1, M, 2 * MEM_WIDTH), lambda b, i: (b, 0, 0))],
        out_specs=pl.BlockSpec((1, MEM_TQ, MEM_WIDTH), lambda b, i: (b, i, 0)),
        out_shape=jax.ShapeDtypeStruct((B, S, MEM_WIDTH), jnp.bfloat16),
        compiler_params=_col_params(),
        name="mem_attn",
    )(q, kv)


MERGE_TM = 256


def _layer_norm_rows(y, g, b):
    mu = jnp.mean(y, axis=-1, keepdims=True)
    d = y - mu
    var = jnp.mean(d * d, axis=-1, keepdims=True)
    return d * lax.rsqrt(var + LN_EPS) * g + b


def _merge_kernel(a_ref, h_ref, m_ref, g_ref, x_ref, wb_ref, wo_ref, lg_ref, lb_ref, o_ref):
    bf = jnp.bfloat16
    f32 = jnp.float32
    merged = g_ref[:, 0:D_MODEL] * jnp.dot(a_ref[...].astype(bf), wb_ref[0], preferred_element_type=f32)
    merged = merged + g_ref[:, D_MODEL:2 * D_MODEL] * jnp.dot(h_ref[...].astype(bf), wb_ref[1], preferred_element_type=f32)
    merged = merged + g_ref[:, 2 * D_MODEL:] * jnp.dot(m_ref[...].astype(bf), wb_ref[2], preferred_element_type=f32)
    y = ALPHA * x_ref[...] + jnp.dot(merged.astype(bf), wo_ref[...], preferred_element_type=f32)
    o_ref[...] = _layer_norm_rows(y, lg_ref[...], lb_ref[...])


def merge_norm(att, hy, mem, gates, x, w_branch, w_out, ln_g, ln_b):
    T = x.shape[0]
    bf = jnp.bfloat16
    rows = lambda w: pl.BlockSpec((MERGE_TM, w), lambda i: (i, 0))
    whole = lambda a: pl.BlockSpec(a.shape, lambda i: (0,) * a.ndim)
    wb, wo = w_branch.astype(bf), w_out.astype(bf)
    lg, lb = ln_g.reshape(1, D_MODEL), ln_b.reshape(1, D_MODEL)
    return pl.pallas_call(
        _merge_kernel,
        grid=(T // MERGE_TM,),
        in_specs=[rows(BRANCH_WIDTH), rows(BRANCH_WIDTH), rows(BRANCH_WIDTH), rows(N_BRANCH * D_MODEL), rows(D_MODEL),
                  whole(wb), whole(wo), whole(lg), whole(lb)],
        out_specs=rows(D_MODEL),
        out_shape=jax.ShapeDtypeStruct((T, D_MODEL), jnp.float32),
        compiler_params=pltpu.CompilerParams(dimension_semantics=("arbitrary",), vmem_limit_bytes=VMEM_LIMIT_BYTES),
        name="merge_norm",
    )(att, hy, mem, gates, x, wb, wo, lg, lb)


def _residual_norm_kernel(x_ref, r_ref, lg_ref, lb_ref, o_ref):
    o_ref[...] = _layer_norm_rows(ALPHA * x_ref[...] + r_ref[...], lg_ref[...], lb_ref[...])


def residual_norm(x, r, ln_g, ln_b):
    T = x.shape[0]
    tm = 512
    rows = pl.BlockSpec((tm, D_MODEL), lambda i: (i, 0))
    one = pl.BlockSpec((1, D_MODEL), lambda i: (0, 0))
    return pl.pallas_call(
        _residual_norm_kernel,
        grid=(T // tm,),
        in_specs=[rows, rows, one, one],
        out_specs=rows,
        out_shape=jax.ShapeDtypeStruct((T, D_MODEL), jnp.float32),
        compiler_params=pltpu.CompilerParams(dimension_semantics=("arbitrary",), vmem_limit_bytes=VMEM_LIMIT_BYTES),
        name="residual_norm",
    )(x, r, ln_g.reshape(1, D_MODEL), ln_b.reshape(1, D_MODEL))


FFT_N1 = 256
FFT_N2 = 128
FFT_N = FFT_N1 * FFT_N2
HY_COLS = FFT_N2 * HYENA_WIDTH
LMUL_TN = 2048
KB_K1 = 8
FILT_TM = 512


def _dft_constants():
    f32 = jnp.float32
    n1 = jnp.arange(FFT_N1, dtype=jnp.int32)
    ang1 = ((n1[:, None] * n1[None, :]) % FFT_N1).astype(f32) * f32(2.0 * math.pi / FFT_N1)
    fr, fi = jnp.cos(ang1), -jnp.sin(ang1)
    h = FFT_N1 // 2
    f_fwd = jnp.concatenate([jnp.concatenate([fr[:, :h], -fi[:, :h]], axis=1),
                             jnp.concatenate([fi[:, :h], fr[:, :h]], axis=1)], axis=0)
    f_inv = f_fwd.T * f32(1.0 / FFT_N)
    f_real = jnp.concatenate([fr, fi], axis=0)
    k2 = jnp.arange(FFT_N2, dtype=jnp.int32)
    k = n1[:, None, None] + FFT_N1 * k2[None, :, None]
    ang2 = ((k * k2[None, None, :]) % FFT_N).astype(f32) * f32(2.0 * math.pi / FFT_N)
    gr, gi = jnp.cos(ang2), -jnp.sin(ang2)
    g = jnp.concatenate([jnp.concatenate([gr, -gi], axis=2), jnp.concatenate([gi, gr], axis=2)], axis=1)
    bf = jnp.bfloat16
    return f_fwd.astype(bf), f_inv.astype(bf), f_real.astype(bf), g.astype(bf), g.transpose(0, 2, 1).astype(bf)


def _lmul_pair_kernel(l_ref, xr_ref, xi_ref, o_ref):
    x = jnp.concatenate([xr_ref[0], xi_ref[0]], axis=0).astype(jnp.bfloat16)
    o_ref[0] = jnp.dot(l_ref[...], x, preferred_element_type=jnp.float32).astype(o_ref.dtype)


def _lmul_kernel(l_ref, x_ref, o_ref):
    o_ref[0] = jnp.dot(l_ref[...], x_ref[0].astype(jnp.bfloat16),
                       preferred_element_type=jnp.float32).astype(o_ref.dtype)


def _lmul_gate_kernel(l_ref, a_ref, zr_ref, zi_ref, pr_ref, pi_ref, b_ref, o_ref):
    y = jnp.dot(l_ref[...], a_ref[0], preferred_element_type=jnp.float32)
    h = FFT_N1 // 2
    o_ref[0, :h] = pr_ref[0] * (y[:h] + zr_ref[0] * b_ref[...])
    o_ref[0, h:] = pi_ref[0] * (y[h:] + zi_ref[0] * b_ref[...])


def dft_stage1_pairs(f_fwd, z):
    nb, h, cols = z.shape
    npair = (nb + 1) // 2
    last = nb - 1
    return pl.pallas_call(
        _lmul_pair_kernel,
        grid=(npair, cols // LMUL_TN),
        in_specs=[
            pl.BlockSpec(f_fwd.shape, lambda p, j: (0, 0)),
            pl.BlockSpec((1, h, LMUL_TN), lambda p, j: (jnp.minimum(2 * p, last), 0, j)),
            pl.BlockSpec((1, h, LMUL_TN), lambda p, j: (jnp.minimum(2 * p + 1, last), 0, j)),
        ],
        out_specs=pl.BlockSpec((1, 2 * FFT_N1, LMUL_TN), lambda p, j: (p, 0, j)),
        out_shape=jax.ShapeDtypeStruct((npair, 2 * FFT_N1, cols), jnp.bfloat16),
        compiler_params=_col_params(),
        name="dft_stage1",
    )(f_fwd, z, z)


def dft_stage1_real(f_real, x):
    _, h, cols = x.shape
    return pl.pallas_call(
        _lmul_kernel,
        grid=(1, cols // LMUL_TN),
        in_specs=[pl.BlockSpec(f_real.shape, lambda p, j: (0, 0)),
                  pl.BlockSpec((1, h, LMUL_TN), lambda p, j: (p, 0, j))],
        out_specs=pl.BlockSpec((1, 2 * FFT_N1, LMUL_TN), lambda p, j: (p, 0, j)),
        out_shape=jax.ShapeDtypeStruct((1, 2 * FFT_N1, cols), jnp.bfloat16),
        compiler_params=_col_params(),
        name="dft_stage1_real",
    )(f_real, x)


def idft_stage1_gate(f_inv, a, z, p, bias_row):
    npair, _, cols = a.shape
    half = FFT_N1 // 2

    def half_blk(x, r):
        last = x.shape[0] - 1
        return pl.BlockSpec((1, half, LMUL_TN), lambda q, j: (jnp.minimum(2 * q + r, last), 0, j))

    out = pl.pallas_call(
        _lmul_gate_kernel,
        grid=(npair, cols // LMUL_TN),
        in_specs=[pl.BlockSpec(f_inv.shape, lambda q, j: (0, 0)),
                  pl.BlockSpec((1, 2 * FFT_N1, LMUL_TN), lambda q, j: (q, 0, j)),
                  half_blk(z, 0), half_blk(z, 1), half_blk(p, 0), half_blk(p, 1),
                  pl.BlockSpec((1, LMUL_TN), lambda q, j: (0, 0))],
        out_specs=pl.BlockSpec((1, FFT_N1, LMUL_TN), lambda q, j: (q, 0, j)),
        out_shape=jax.ShapeDtypeStruct((npair, FFT_N1, cols), jnp.float32),
        compiler_params=_col_params(),
        name="idft_stage1_gate",
    )(f_inv, a, z, z, p, p, bias_row)
    return out.reshape(2 * npair, half, cols)


def _stage2_conv_kernel(a_ref, g_ref, gt_ref, h_ref, o_ref):
    for j in range(KB_K1):
        x = jnp.concatenate([a_ref[0, 0, j], a_ref[0, 1, j]], axis=0)
        z = jnp.dot(g_ref[j], x, preferred_element_type=jnp.float32)
        zr, zi = z[:FFT_N2], z[FFT_N2:]
        hr, hi = h_ref[0, j], h_ref[1, j]
        w = jnp.concatenate([zr * hr - zi * hi, zr * hi + zi * hr], axis=0).astype(jnp.bfloat16)
        y = jnp.dot(gt_ref[j], w, preferred_element_type=jnp.float32)
        o_ref[0, 0, j] = y[:FFT_N2].astype(o_ref.dtype)
        o_ref[0, 1, j] = y[FFT_N2:].astype(o_ref.dtype)


def stage2_conv(a, g, gt, hf, order):
    npair = a.shape[0]
    C = HYENA_WIDTH
    ablk = pl.BlockSpec((1, 2, KB_K1, FFT_N2, C), lambda p, i: (p, 0, i, 0, 0))
    gblk = pl.BlockSpec((KB_K1, 2 * FFT_N2, 2 * FFT_N2), lambda p, i: (i, 0, 0))
    return pl.pallas_call(
        _stage2_conv_kernel,
        grid=(npair, FFT_N1 // KB_K1),
        in_specs=[ablk, gblk, gblk,
                  pl.BlockSpec((2, KB_K1, FFT_N2, C), lambda p, i: (0, i, 0, order))],
        out_specs=ablk,
        out_shape=jax.ShapeDtypeStruct(a.shape, a.dtype),
        compiler_params=_col_params(),
        name="stage2_conv",
    )(a, g, gt, hf)


def _stage2_filter_kernel(a_ref, g_ref, s_ref, o_ref):
    for j in range(KB_K1):
        x = jnp.concatenate([a_ref[0, j], a_ref[1, j]], axis=0)
        z = jnp.dot(g_ref[j], x, preferred_element_type=jnp.float32) * s_ref[...]
        o_ref[0, j] = z[:FFT_N2]
        o_ref[1, j] = z[FFT_N2:]


def stage2_filter(a, g, inv_norm):
    W = a.shape[-1]
    blk = pl.BlockSpec((2, KB_K1, FFT_N2, W), lambda i: (0, i, 0, 0))
    return pl.pallas_call(
        _stage2_filter_kernel,
        grid=(FFT_N1 // KB_K1,),
        in_specs=[blk, pl.BlockSpec((KB_K1, 2 * FFT_N2, 2 * FFT_N2), lambda i: (i, 0, 0)),
                  pl.BlockSpec((1, W), lambda i: (0, 0))],
        out_specs=blk,
        out_shape=jax.ShapeDtypeStruct(a.shape, jnp.float32),
        compiler_params=pltpu.CompilerParams(dimension_semantics=("arbitrary",), vmem_limit_bytes=VMEM_LIMIT_BYTES),
        name="stage2_filter",
    )(a, g, inv_norm)


def _filter_gen_kernel(bands_ref, w0_ref, wc_ref, ws_ref, wmid_ref, fb_ref, ff_ref, wdir_ref, wbwd_ref,
                       adelta_ref, two_ref, norm_ref, *, seq):
    i = pl.program_id(0)
    bf = jnp.bfloat16
    row = i * FILT_TM + lax.broadcasted_iota(jnp.int32, (FILT_TM, 1), 0)
    pos_i = jnp.where(row < seq, row, 2 * seq - row)
    pos = pos_i.astype(jnp.float32)
    t = pos / max(seq - 1, 1)
    w = (jnp.float32(2.0 * math.pi) * pos) / seq
    ang = w * bands_ref[...]
    pre = (jnp.dot(jnp.cos(ang).astype(bf), wc_ref[...], preferred_element_type=jnp.float32)
           + jnp.dot((-jnp.sin(ang)).astype(bf), ws_ref[...], preferred_element_type=jnp.float32)
           + t.astype(bf).astype(jnp.float32) * w0_ref[...])
    h = jnp.sin(ff_ref[0:1, :] * (pre + fb_ref[0:1, :]))
    for m in range(FILTER_INNER):
        pre = jnp.dot(h.astype(bf), wmid_ref[m], preferred_element_type=jnp.float32)
        h = jnp.sin(ff_ref[m + 1:m + 2, :] * (pre + fb_ref[m + 1:m + 2, :]))
    hb = h.astype(bf)
    decay = jnp.exp(-t * adelta_ref[...])
    out = jnp.dot(hb, wdir_ref[0], preferred_element_type=jnp.float32) * decay
    out = jnp.where(pos_i < seq, out, 0.0)

    @pl.when(i == 0)
    def _():
        extra = jnp.dot(hb, wbwd_ref[...], preferred_element_type=jnp.float32) * decay
        first = jnp.where(row == 0, out + extra, out)
        two_ref[...] = first
        norm_ref[...] = jnp.sum(jnp.abs(first), axis=0, keepdims=True)

    @pl.when(i != 0)
    def _():
        two_ref[...] = out
        norm_ref[...] += jnp.sum(jnp.abs(out), axis=0, keepdims=True)


def hyena_filter_rows(seq, f_w_in, f_w_mid, f_b, f_freq, f_w_out):
    f32, bf = jnp.float32, jnp.bfloat16
    C, H = HYENA_WIDTH, FILTER_HIDDEN
    bands = jnp.linspace(1e-4, FILTER_BANDS - 1, FILTER_BANDS, dtype=f32)
    bands = jnp.pad(bands, (0, VREG_LANES - FILTER_BANDS)).reshape(1, VREG_LANES)
    w_in = f_w_in.astype(bf)
    w0 = w_in[0:1].astype(f32)
    pad_rows = ((0, VREG_LANES - FILTER_BANDS), (0, 0))
    wc = jnp.pad(w_in[1:1 + FILTER_BANDS], pad_rows)
    ws = jnp.pad(w_in[1 + FILTER_BANDS:], pad_rows)
    w_out = f_w_out.astype(bf).reshape(H, HYENA_ORDER, 2, C)
    wdir = jnp.stack([w_out[:, :, 0].reshape(H, HYENA_ORDER * C), w_out[:, :, 1].reshape(H, HYENA_ORDER * C)])
    max_decay = math.log(DECAY_TARGET) / FAST_DECAY_PCT
    min_decay = math.log(DECAY_TARGET) / SLOW_DECAY_PCT
    adelta = jnp.abs(jnp.linspace(min_decay, max_decay, C, dtype=f32))
    adelta = jnp.tile(adelta, HYENA_ORDER).reshape(1, HYENA_ORDER * C)
    n_tiles = 2 * seq // FILT_TM
    half_tiles = seq // FILT_TM
    whole = lambda a: pl.BlockSpec(a.shape, lambda i: (0,) * a.ndim)
    wmid = f_w_mid.astype(bf)
    fb, ff = f_b.astype(f32), f_freq.astype(f32)
    return pl.pallas_call(
        functools.partial(_filter_gen_kernel, seq=seq),
        grid=(n_tiles,),
        in_specs=[whole(bands), whole(w0), whole(wc), whole(ws), whole(wmid), whole(fb), whole(ff),
                  pl.BlockSpec((1, H, HYENA_ORDER * C), lambda i: (i // half_tiles, 0, 0)),
                  pl.BlockSpec((None, H, HYENA_ORDER * C), lambda i: (1, 0, 0)),
                  whole(adelta)],
        out_specs=[pl.BlockSpec((FILT_TM, HYENA_ORDER * C), lambda i: (i, 0)),
                   pl.BlockSpec((1, HYENA_ORDER * C), lambda i: (0, 0))],
        out_shape=[jax.ShapeDtypeStruct((2 * seq, HYENA_ORDER * C), f32),
                   jax.ShapeDtypeStruct((1, HYENA_ORDER * C), f32)],
        compiler_params=pltpu.CompilerParams(dimension_semantics=("arbitrary",), vmem_limit_bytes=VMEM_LIMIT_BYTES),
        name="hyena_filter_gen",
    )(bands, w0, wc, ws, wmid, fb, ff, wdir, wdir, adelta)


def _short_conv_kernel(x_ref, w_ref, b_ref, o_ref):
    x = x_ref[0]
    n = x.shape[0]
    t = lax.broadcasted_iota(jnp.int32, x.shape, 0)
    prev = jnp.where(t == 0, 0.0, pltpu.roll(x, 1, 0))
    nxt = jnp.where(t == n - 1, 0.0, pltpu.roll(x, n - 1, 0))
    o_ref[0] = ((b_ref[...] + prev * w_ref[0:1, :]) + x * w_ref[1:2, :]) + nxt * w_ref[2:3, :]


def short_conv_pallas(u, w, b):
    B, L, W = u.shape
    tc = VREG_LANES
    return pl.pallas_call(
        _short_conv_kernel,
        grid=(B, W // tc),
        in_specs=[pl.BlockSpec((1, L, tc), lambda i, j: (i, 0, j)),
                  pl.BlockSpec((SHORT_CONV, tc), lambda i, j: (0, j)),
                  pl.BlockSpec((1, tc), lambda i, j: (0, j))],
        out_specs=pl.BlockSpec((1, L, tc), lambda i, j: (i, 0, j)),
        out_shape=jax.ShapeDtypeStruct((B, L, W), jnp.float32),
        compiler_params=_col_params(),
        name="short_conv",
    )(u, w, b.reshape(1, W))


def hyena_mixer_pallas(u, conv_w, conv_b, f_w_in, f_w_mid, f_b, f_freq, f_w_out, hyena_bias):
    B, L, _ = u.shape
    assert 2 * L == FFT_N
    C = HYENA_WIDTH
    f_fwd, f_inv, f_real, g, gt = _dft_constants()
    two, norm = hyena_filter_rows(L, f_w_in, f_w_mid, f_b, f_freq, f_w_out)
    af = dft_stage1_real(f_real, two.reshape(1, FFT_N1, FFT_N2 * HYENA_ORDER * C))
    af = af.reshape(2, FFT_N1, FFT_N2, HYENA_ORDER * C)
    hf = stage2_filter(af, g, 1.0 / norm)
    uc = short_conv_pallas(u, conv_w, conv_b)
    half = FFT_N1 // 2
    parts = [uc[..., o * C:(o + 1) * C].reshape(B, half, HY_COLS) for o in range(HYENA_ORDER + 1)]
    z = parts[0]
    for o in range(HYENA_ORDER):
        a = dft_stage1_pairs(f_fwd, z)
        npair = a.shape[0]
        a = stage2_conv(a.reshape(npair, 2, FFT_N1, FFT_N2, C), g, gt, hf, o)
        bias_row = jnp.tile(hyena_bias[o], LMUL_TN // C).reshape(1, LMUL_TN)
        z = idft_stage1_gate(f_inv, a.reshape(npair, 2 * FFT_N1, HY_COLS), z, parts[o + 1], bias_row)
    return z[:B].reshape(B, L, C)


PEER_SEL = PEER_HEADS * PEER_TOPK
PEER_TB = 128
VREG_SUBLANES = 8
VREG_LANES = 128
EXPERT_ROWS = 4
TILE_ROWS = 16
V_UNROLL = 16
IDX_SPLIT = 8
IDX_PER = PEER_SEL // IDX_SPLIT


def _row_reader(idx_refs, t):
    offs = [t * IDX_PER + j for j in range(IDX_PER)]
    return lambda k: idx_refs[k // IDX_PER][offs[k % IDX_PER]]


def pack_expert_table(tab):
    e = tab.shape[0]
    b = lax.bitcast_convert_type(tab.astype(jnp.bfloat16), jnp.uint16).astype(jnp.uint32)
    b = b.reshape(e, 2, EXPERT_ROWS, VREG_LANES)
    w = (b[:, 0] | (b[:, 1] << 16)).reshape(e * EXPERT_ROWS, VREG_LANES)
    return jnp.pad(w, ((0, VREG_SUBLANES), (0, 0)))


def from_tile_rows(y):
    T = y.shape[0]
    return y.reshape(T, EXPERT_ROWS, 2, VREG_LANES).transpose(0, 2, 1, 3).reshape(T, D_MODEL)


def _split_bf16(v):
    hi = v.astype(jnp.bfloat16)
    return hi, (v - hi.astype(jnp.float32)).astype(jnp.bfloat16)


def _gelu_exact(x):
    return 0.5 * x * (1.0 + lax.erf(x * (2.0 ** -0.5)))


_COMBINE_POS = (3, 7, 1, 5, 2, 6, 0, 4)


def _sublane_sums(c, sub):
    mv = (sub & 2) != 0
    e = []
    for c1, c2 in ((c[0], c[1]), (c[2], c[3])):
        e.append(jnp.where(mv, c1 + pltpu.roll(c1, 2, 0), c2 + pltpu.roll(c2, 6, 0)))
    mo = (sub & 1) != 0
    return jnp.where(mo, e[0] + pltpu.roll(e[0], 1, 0), e[1] + pltpu.roll(e[1], 7, 0))


def _peer_u_kernel(*refs):
    idx_refs = refs[:IDX_SPLIT]
    x_ref, g_ref, tab_ref, o_ref, act_ref, r_ref = refs[IDX_SPLIT:]
    sub = lax.broadcasted_iota(jnp.int32, (VREG_SUBLANES, VREG_LANES), 0)
    lo4 = sub < EXPERT_ROWS

    def token(t, carry):
        sel_row = _row_reader(idx_refs, t)
        x8 = x_ref[t]
        xr = pltpu.roll(x8, EXPERT_ROWS, 0)
        x_lo = jnp.where(lo4, x8, xr)
        x_hi = jnp.where(lo4, xr, x8)
        rs = []
        for grp in range(PEER_SEL // VREG_SUBLANES):
            pairs = []
            for i in range(VREG_SUBLANES // 2):
                wa = tab_ref[pl.ds(sel_row(grp * VREG_SUBLANES + _COMBINE_POS[2 * i]), VREG_SUBLANES), :]
                wb = tab_ref[pl.ds(sel_row(grp * VREG_SUBLANES + _COMBINE_POS[2 * i + 1]), VREG_SUBLANES), :]
                w = jnp.where(lo4, wa, pltpu.roll(wb, EXPERT_ROWS, 0))
                lo = lax.bitcast_convert_type(w << 16, jnp.float32)
                hi = lax.bitcast_convert_type(w & jnp.uint32(0xFFFF0000), jnp.float32)
                pairs.append(lo * x_lo + hi * x_hi)
            rs.append(_sublane_sums(pairs, sub))
        r0 = pl.multiple_of(t * PEER_SEL, PEER_SEL)
        r_ref[pl.ds(r0, PEER_SEL), :] = jnp.concatenate(rs, axis=0).astype(jnp.bfloat16)
        return carry

    lax.fori_loop(0, PEER_TB, token, 0)
    ones = jnp.ones((VREG_LANES, VREG_LANES), jnp.bfloat16)
    keep = (lax.broadcasted_iota(jnp.int32, (PEER_SEL, VREG_LANES), 0)
            == lax.broadcasted_iota(jnp.int32, (PEER_SEL, VREG_LANES), 1))
    chunk = VREG_SUBLANES * PEER_SEL
    for c in range(PEER_TB // VREG_SUBLANES):
        s = jnp.dot(r_ref[c * chunk:(c + 1) * chunk, :], ones, preferred_element_type=jnp.float32)
        for j in range(VREG_SUBLANES):
            blk = jnp.where(keep, s[j * PEER_SEL:(j + 1) * PEER_SEL], 0.0)
            act_ref[c * VREG_SUBLANES + j:c * VREG_SUBLANES + j + 1, :] = jnp.sum(blk, axis=0, keepdims=True)
    o_ref[...] = g_ref[...] * _gelu_exact(act_ref[...])


def _peer_v_kernel(*refs):
    idx_refs = refs[:IDX_SPLIT]
    coef_ref, e_ref, d_ref, tab_ref, o_ref, chi_ref, clo_ref = refs[IDX_SPLIT:]
    c_hi, c_lo = _split_bf16(coef_ref[...])
    chi_ref[...] = jnp.dot(c_hi, e_ref[...], preferred_element_type=jnp.float32)
    clo_ref[...] = jnp.dot(c_lo, e_ref[...], preferred_element_type=jnp.float32)

    def token(t, carry):
        sel_row = _row_reader(idx_refs, t)
        cmat = jnp.concatenate([chi_ref[pl.ds(t, 1), :] * d_ref[...], clo_ref[pl.ds(t, 1), :] * d_ref[...]],
                               axis=0).astype(jnp.bfloat16)
        w = jnp.concatenate([pltpu.bitcast(tab_ref[pl.ds(sel_row(k), VREG_SUBLANES), :], jnp.bfloat16)
                             for k in range(PEER_SEL)], axis=0)
        acc = jnp.dot(cmat, w, preferred_element_type=jnp.float32)
        o_ref[t] = acc[:VREG_SUBLANES] + acc[VREG_SUBLANES:]
        return carry

    lax.fori_loop(0, PEER_TB, token, 0, unroll=V_UNROLL)


def peer_experts(x, eidx, g, tab_u, tab_v):
    T = x.shape[0]
    grid = (T // PEER_TB,)
    f32, bf = jnp.float32, jnp.bfloat16
    sel = jnp.arange(PEER_SEL, dtype=jnp.int32)
    col = jnp.arange(PEER_SEL * TILE_ROWS, dtype=jnp.int32)
    expand = (col[None, :] // TILE_ROWS == sel[:, None]).astype(bf)
    diag = (col[None, :] % TILE_ROWS == jnp.arange(VREG_SUBLANES, dtype=jnp.int32)[:, None]).astype(f32)
    smem_blk = pl.BlockSpec((PEER_TB * IDX_PER,), lambda i: (i,), memory_space=pltpu.SMEM)
    idx_parts = [eidx[:, j * IDX_PER:(j + 1) * IDX_PER].reshape(T * IDX_PER) for j in range(IDX_SPLIT)]
    vec_blk = pl.BlockSpec((PEER_TB, PEER_SEL), lambda i: (i, 0))
    row_blk = pl.BlockSpec((PEER_TB, VREG_SUBLANES, VREG_LANES), lambda i: (i, 0, 0))
    whole = lambda a: pl.BlockSpec(a.shape, lambda i: (0,) * a.ndim)
    tab_spec = pl.BlockSpec(memory_space=pltpu.VMEM)
    params = pltpu.CompilerParams(dimension_semantics=("arbitrary",), vmem_limit_bytes=VMEM_LIMIT_BYTES)
    coef = pl.pallas_call(
        _peer_u_kernel,
        grid=grid,
        in_specs=[smem_blk] * IDX_SPLIT + [row_blk, vec_blk, tab_spec],
        out_specs=vec_blk,
        out_shape=jax.ShapeDtypeStruct((T, PEER_SEL), f32),
        scratch_shapes=[pltpu.VMEM((PEER_TB, PEER_SEL), f32),
                        pltpu.VMEM((PEER_TB * PEER_SEL, VREG_LANES), bf)],
        compiler_params=params,
        name="peer_u",
    )(*idx_parts, x.reshape(T, VREG_SUBLANES, VREG_LANES), g, tab_u)
    out = pl.pallas_call(
        _peer_v_kernel,
        grid=grid,
        in_specs=[smem_blk] * IDX_SPLIT + [vec_blk, whole(expand), whole(diag), tab_spec],
        out_specs=row_blk,
        out_shape=jax.ShapeDtypeStruct((T, VREG_SUBLANES, VREG_LANES), f32),
        scratch_shapes=[pltpu.VMEM((PEER_TB, PEER_SEL * TILE_ROWS), f32),
                        pltpu.VMEM((PEER_TB, PEER_SEL * TILE_ROWS), f32)],
        compiler_params=params,
        name="peer_v",
    )(*idx_parts, coef, expand, diag, tab_v)
    return from_tile_rows(out)


ROUTE_TM = 512


def _top16_rows(s, key_id):
    row16 = lax.broadcasted_iota(jnp.int32, (PEER_TOPK, VREG_LANES), 0)
    vals = jnp.zeros((PEER_TOPK, VREG_LANES), jnp.float32)
    ids = jnp.zeros((PEER_TOPK, VREG_LANES), jnp.float32)
    big = jnp.float32(2 ** 30)
    for j in range(PEER_TOPK):
        m = jnp.max(s, axis=0, keepdims=True)
        am = jnp.min(jnp.where(s == m, key_id, big), axis=0, keepdims=True)
        vals = jnp.where(row16 == j, m, vals)
        ids = jnp.where(row16 == j, am, ids)
        s = jnp.where(key_id == am, -jnp.inf, s)
    return vals, ids


_PAIR_GROUPS = ((0, 0, 8), (0, 8, 8), (1, 0, 8), (2, 0, 5), (3, 0, 4), (4, 0, 3), (5, 0, 2), (6, 0, 2), (7, 0, 2))


def _route_head(s0, i0, s1, i1):
    sub_i = lax.broadcasted_iota(jnp.int32, (VREG_SUBLANES, VREG_LANES), 0)
    sub = sub_i.astype(jnp.float32)
    cands, flat, eids = [], [], []
    for a, b0, nb in _PAIR_GROUPS:
        c = s0[a:a + 1] + s1[b0:b0 + VREG_SUBLANES]
        cands.append(jnp.where(sub_i < nb, c, -jnp.inf) if nb < VREG_SUBLANES else c)
        flat.append(a * PEER_TOPK + b0 + sub)
        eids.append(i0[a:a + 1] * N_KEYS + i1[b0:b0 + VREG_SUBLANES])
    cands.append(s0[VREG_SUBLANES:] + s1[0:1])
    flat.append((sub + VREG_SUBLANES) * PEER_TOPK)
    eids.append(i0[VREG_SUBLANES:] * N_KEYS + i1[0:1])
    cand = jnp.concatenate(cands, axis=0)
    flat = jnp.concatenate(flat, axis=0)
    eid = jnp.concatenate(eids, axis=0)
    row16 = lax.broadcasted_iota(jnp.int32, (PEER_TOPK, VREG_LANES), 0)
    sc = jnp.zeros((PEER_TOPK, VREG_LANES), jnp.float32)
    sel = jnp.zeros((PEER_TOPK, VREG_LANES), jnp.float32)
    big = jnp.float32(2 ** 30)
    for j in range(PEER_TOPK):
        m = jnp.max(cand, axis=0, keepdims=True)
        am = jnp.min(jnp.where(cand == m, flat, big), axis=0, keepdims=True)
        hit = flat == am
        e = jnp.max(jnp.where(hit, eid, -1.0), axis=0, keepdims=True)
        sc = jnp.where(row16 == j, m, sc)
        sel = jnp.where(row16 == j, e, sel)
        cand = jnp.where(hit, -jnp.inf, cand)
    p = jnp.exp(sc - sc[0:1])
    return sel, p / jnp.sum(p, axis=0, keepdims=True)


def _route_kernel(x_ref, wq_ref, sk_ref, rows_ref, g_ref, q_ref):
    q = jnp.dot(x_ref[...].astype(jnp.bfloat16), wq_ref[...], preferred_element_type=jnp.float32)
    q_ref[...] = q.astype(jnp.bfloat16)
    key_id = lax.broadcasted_iota(jnp.int32, (N_KEYS, VREG_LANES), 0).astype(jnp.float32)

    def head(h, carry):
        tops = []
        for c in range(2):
            hc = h * 2 + c
            qhc = q_ref[:, pl.ds(pl.multiple_of(hc * PEER_HALF, PEER_HALF), PEER_HALF)]
            s = lax.dot_general(sk_ref[hc], qhc, _NT_DIMS, preferred_element_type=jnp.float32)
            tops.append([_top16_rows(s[:, j * VREG_LANES:(j + 1) * VREG_LANES], key_id)
                         for j in range(ROUTE_TM // VREG_LANES)])
        r0 = pl.multiple_of(h * PEER_TOPK, PEER_TOPK)
        for j in range(ROUTE_TM // VREG_LANES):
            (s0, i0), (s1, i1) = tops[0][j], tops[1][j]
            sel, g = _route_head(s0, i0, s1, i1)
            rows_ref[pl.ds(r0, PEER_TOPK), j * VREG_LANES:(j + 1) * VREG_LANES] = (sel * EXPERT_ROWS).astype(jnp.int32)
            g_ref[pl.ds(r0, PEER_TOPK), j * VREG_LANES:(j + 1) * VREG_LANES] = g
        return carry

    lax.fori_loop(0, PEER_HEADS, head, 0)


def peer_route(x, w_query, sub_keys):
    T = x.shape[0]
    wq = w_query.astype(jnp.bfloat16)
    sk = sub_keys.reshape(PEER_HEADS * 2, N_KEYS, PEER_HALF).astype(jnp.bfloat16)
    out_blk = pl.BlockSpec((PEER_SEL, ROUTE_TM), lambda i: (0, i))
    rows, g = pl.pallas_call(
        _route_kernel,
        grid=(T // ROUTE_TM,),
        in_specs=[
            pl.BlockSpec((ROUTE_TM, D_MODEL), lambda i: (i, 0)),
            pl.BlockSpec(wq.shape, lambda i: (0, 0)),
            pl.BlockSpec(sk.shape, lambda i: (0, 0, 0)),
        ],
        out_specs=[out_blk, out_blk],
        out_shape=[jax.ShapeDtypeStruct((PEER_SEL, T), jnp.int32),
                   jax.ShapeDtypeStruct((PEER_SEL, T), jnp.float32)],
        scratch_shapes=[pltpu.VMEM((ROUTE_TM, PEER_HEADS * PEER_QDIM), jnp.bfloat16)],
        compiler_params=pltpu.CompilerParams(dimension_semantics=("arbitrary",),
                                             vmem_limit_bytes=VMEM_LIMIT_BYTES),
        name="peer_route",
    )(x, wq, sk)
    return rows.T, g.T


def encoder_layer(x, mem, rel_bias, w_in, b_in, conv_w, conv_b, attn_sink, f_w_in, f_w_mid, f_b, f_freq,
                  f_w_out, hyena_bias, w_mem_kv, w_branch, w_out, ln1_g, ln1_b, w_query, sub_keys,
                  expert_u, expert_v, ln2_g, ln2_b):
    B, S, _ = x.shape
    T = B * S
    xf = x.reshape(T, D_MODEL)
    proj = lambda lo, hi, **kw: linear(xf, w_in[:, lo:hi], b_in[lo:hi], **kw)
    qkv = proj(0, O_HY).reshape(B, S, O_HY)
    hy = proj(O_HY, O_MQ).reshape(B, S, O_MQ - O_HY)
    q_m = proj(O_MQ, O_GATE).reshape(B, S, MEM_WIDTH)
    gates = proj(O_GATE, IN_WIDTH, tn=(IN_WIDTH - O_GATE) // 2, sigmoid=True)
    M = mem.shape[1]
    kv = linear(mem.reshape(B * M, D_MODEL), w_mem_kv, jnp.zeros((2 * MEM_WIDTH,), jnp.float32), tm=B * M)
    att = window_attention_pallas(qkv, window_bias_table(rel_bias), attn_sink)
    hyo = hyena_mixer_pallas(hy, conv_w, conv_b, f_w_in, f_w_mid, f_b, f_freq, f_w_out, hyena_bias)
    mat = memory_attention_pallas(q_m, kv.reshape(B, M, 2 * MEM_WIDTH))
    x1 = merge_norm(att.reshape(T, ATT_WIDTH), hyo.reshape(T, HYENA_WIDTH), mat.reshape(T, MEM_WIDTH), gates, xf,
                    w_branch, w_out, ln1_g, ln1_b)
    rows, g = peer_route(x1, w_query, sub_keys)
    peer = peer_experts(x1, rows, g, pack_expert_table(expert_u), pack_expert_table(expert_v))
    return residual_norm(x1, peer, ln2_g, ln2_b).reshape(B, S, D_MODEL)


def kernel(x_prompt, x_sample, mem_prompt, mem_sample, rel_bias, w_in, b_in, conv_w, conv_b, attn_sink,
           f_w_in, f_w_mid, f_b, f_freq, f_w_out, hyena_bias, w_mem_kv, w_branch, w_out, ln1_g, ln1_b,
           w_query, sub_keys, expert_u, expert_v, ln2_g, ln2_b):
    nb = x_prompt.shape[0]
    x = jnp.concatenate([x_prompt, x_sample], axis=0)
    mem = jnp.concatenate([mem_prompt, mem_sample], axis=0)
    for l in range(DEPTH):
        x = encoder_layer(x, mem, rel_bias, w_in[l], b_in[l], conv_w[l], conv_b[l], attn_sink[l],
                          f_w_in[l], f_w_mid[l], f_b[l], f_freq[l], f_w_out[l], hyena_bias[l],
                          w_mem_kv[l], w_branch[l], w_out[l], ln1_g[l], ln1_b[l], w_query[l],
                          sub_keys[l], expert_u[l], expert_v[l], ln2_g[l], ln2_b[l])
    return (x[:nb], x[nb:])
```

```python
import functools
import math

import jax
import jax.numpy as jnp
from jax import lax
from jax.experimental import pallas as pl
from jax.experimental.pallas import tpu as pltpu

D_MODEL = 1024
DEPTH = 2
N_MEM = 256
ATT_HEADS = 8
ATT_KV_HEADS = 2
ATT_HEAD_DIM = 64
ATT_WIDTH = ATT_HEADS * ATT_HEAD_DIM
ATT_KV_WIDTH = ATT_KV_HEADS * ATT_HEAD_DIM
WINDOW = 128
BLOCK = 128
N_BUCKETS = 32
MAX_DISTANCE = 128
HYENA_WIDTH = 512
HYENA_ORDER = 2
SHORT_CONV = 3
FILTER_EMB = 33
FILTER_BANDS = (FILTER_EMB - 1) // 2
FILTER_HIDDEN = 64
FILTER_INNER = 2
FAST_DECAY_PCT = 0.3
SLOW_DECAY_PCT = 1.5
DECAY_TARGET = 1e-2
MEM_HEADS = 4
MEM_HEAD_DIM = 128
MEM_WIDTH = MEM_HEADS * MEM_HEAD_DIM
N_BRANCH = 3
BRANCH_WIDTH = 512
PEER_HEADS = 8
N_KEYS = 128
N_EXPERTS = N_KEYS * N_KEYS
PEER_TOPK = 16
PEER_HALF = 128
PEER_QDIM = 2 * PEER_HALF
PEER_BLOCK = 128
O_K = ATT_WIDTH
O_V = O_K + ATT_KV_WIDTH
O_HY = O_V + ATT_KV_WIDTH
O_MQ = O_HY + (HYENA_ORDER + 1) * HYENA_WIDTH
O_GATE = O_MQ + MEM_WIDTH
IN_WIDTH = O_GATE + N_BRANCH * D_MODEL
ALPHA = (2 * DEPTH) ** 0.25
BETA = (8 * DEPTH) ** -0.25
LN_EPS = 1e-5
NEG_INF = -1e30

VMEM_LIMIT_BYTES = 56 * 1024 * 1024


def _linear_kernel(x_ref, w_ref, b_ref, o_ref, *, sigmoid):
    x = x_ref[...].astype(jnp.bfloat16)
    acc = jnp.dot(x, w_ref[...], preferred_element_type=jnp.float32) + b_ref[...]
    o_ref[...] = jax.nn.sigmoid(acc) if sigmoid else acc


def linear(x, w, b, *, tm=512, tn=None, sigmoid=False):
    T, K = x.shape
    N = w.shape[1]
    tn = N if tn is None else tn
    wb = w.astype(jnp.bfloat16)
    return pl.pallas_call(
        functools.partial(_linear_kernel, sigmoid=sigmoid),
        grid=(N // tn, T // tm),
        in_specs=[
            pl.BlockSpec((tm, K), lambda j, i: (i, 0)),
            pl.BlockSpec((K, tn), lambda j, i: (0, j)),
            pl.BlockSpec((1, tn), lambda j, i: (0, j)),
        ],
        out_specs=pl.BlockSpec((tm, tn), lambda j, i: (i, j)),
        out_shape=jax.ShapeDtypeStruct((T, N), jnp.float32),
        compiler_params=pltpu.CompilerParams(
            dimension_semantics=("arbitrary", "arbitrary"),
            vmem_limit_bytes=VMEM_LIMIT_BYTES),
        name="linear",
    )(x, wb, b.reshape(1, N))


ATT_TQ = 512
ATT_KEYS = 3 * BLOCK
_NT_DIMS = (((1,), (1,)), ((), ()))


def t5_bucket(rel):
    nb = N_BUCKETS // 2
    max_exact = nb // 2
    ret = jnp.where(rel > 0, nb, 0)
    n = jnp.abs(rel)
    nf = jnp.maximum(n, 1).astype(jnp.float32)
    large = max_exact + (jnp.log(nf / max_exact) / math.log(MAX_DISTANCE / max_exact)
                         * (nb - max_exact)).astype(jnp.int32)
    large = jnp.minimum(large, nb - 1)
    return ret + jnp.where(n < max_exact, n, large)


def window_bias_table(rel_bias):
    rel = (jnp.arange(ATT_KEYS) - BLOCK)[None, :] - jnp.arange(BLOCK)[:, None]
    bias = rel_bias[t5_bucket(rel)].astype(jnp.float32).transpose(2, 0, 1)
    return jnp.where((jnp.abs(rel) <= WINDOW)[None], bias, NEG_INF)


def _col_params():
    return pltpu.CompilerParams(dimension_semantics=("arbitrary", "arbitrary"), vmem_limit_bytes=VMEM_LIMIT_BYTES)


def _window_attn_kernel(sink_ref, q_ref, kp_ref, kc_ref, kn_ref, vp_ref, vc_ref, vn_ref, bias_ref, o_ref, *, n_steps):
    i = pl.program_id(1)
    bf = jnp.bfloat16
    q = q_ref[0].astype(bf)
    k_all = jnp.concatenate([kp_ref[0], kc_ref[0], kn_ref[0]], axis=0).astype(bf)
    v_all = jnp.concatenate([vp_ref[0], vc_ref[0], vn_ref[0]], axis=0).astype(bf)
    col = lax.broadcasted_iota(jnp.int32, (BLOCK, ATT_KEYS), 1)
    n_blk = ATT_TQ // BLOCK
    scale = ATT_HEAD_DIM ** -0.5
    group = ATT_HEADS // ATT_KV_HEADS
    for j in range(n_blk):
        kj = k_all[j * BLOCK:j * BLOCK + ATT_KEYS]
        vj = v_all[j * BLOCK:j * BLOCK + ATT_KEYS]
        off_seq = None
        if j == 0:
            off_seq = (i == 0) & (col < BLOCK)
        if j == n_blk - 1:
            last = (i == n_steps - 1) & (col >= 2 * BLOCK)
            off_seq = last if off_seq is None else off_seq | last
        for h in range(ATT_HEADS):
            hk = h // group
            qh = q[j * BLOCK:(j + 1) * BLOCK, h * ATT_HEAD_DIM:(h + 1) * ATT_HEAD_DIM]
            kh = kj[:, hk * ATT_HEAD_DIM:(hk + 1) * ATT_HEAD_DIM]
            s = lax.dot_general(qh, kh, _NT_DIMS, preferred_element_type=jnp.float32) * scale + bias_ref[h]
            if off_seq is not None:
                s = jnp.where(off_seq, NEG_INF, s)
            sink = sink_ref[h]
            m = jnp.maximum(jnp.max(s, axis=-1, keepdims=True), sink)
            p = jnp.exp(s - m)
            denom = jnp.sum(p, axis=-1, keepdims=True) + jnp.exp(sink - m)
            oh = jnp.dot((p / denom).astype(bf), vj[:, hk * ATT_HEAD_DIM:(hk + 1) * ATT_HEAD_DIM],
                         preferred_element_type=jnp.float32)
            o_ref[0, j * BLOCK:(j + 1) * BLOCK, h * ATT_HEAD_DIM:(h + 1) * ATT_HEAD_DIM] = oh.astype(o_ref.dtype)


def window_attention_pallas(qkv, bias, sink):
    B, S, _ = qkv.shape
    n_steps = S // ATT_TQ
    per = ATT_TQ // BLOCK
    last_blk = S // BLOCK - 1
    k_col, v_col = O_K // ATT_KV_WIDTH, O_V // ATT_KV_WIDTH

    def edge(col, nxt):
        if nxt:
            return pl.BlockSpec((1, BLOCK, ATT_KV_WIDTH), lambda b, i: (b, jnp.minimum((i + 1) * per, last_blk), col))
        return pl.BlockSpec((1, BLOCK, ATT_KV_WIDTH), lambda b, i: (b, jnp.maximum(i * per - 1, 0), col))

    cur = lambda col: pl.BlockSpec((1, ATT_TQ, ATT_KV_WIDTH), lambda b, i: (b, i, col))
    return pl.pallas_call(
        functools.partial(_window_attn_kernel, n_steps=n_steps),
        grid=(B, n_steps),
        in_specs=[pl.BlockSpec(memory_space=pltpu.SMEM),
                  pl.BlockSpec((1, ATT_TQ, ATT_WIDTH), lambda b, i: (b, i, 0)),
                  edge(k_col, False), cur(k_col), edge(k_col, True),
                  edge(v_col, False), cur(v_col), edge(v_col, True),
                  pl.BlockSpec(bias.shape, lambda b, i: (0, 0, 0))],
        out_specs=pl.BlockSpec((1, ATT_TQ, ATT_WIDTH), lambda b, i: (b, i, 0)),
        out_shape=jax.ShapeDtypeStruct((B, S, ATT_WIDTH), jnp.bfloat16),
        compiler_params=_col_params(),
        name="window_attn",
    )(sink.astype(jnp.float32), qkv, qkv, qkv, qkv, qkv, qkv, qkv, bias)


MEM_TQ = 512


def _mem_attn_kernel(q_ref, kv_ref, o_ref):
    bf = jnp.bfloat16
    q = q_ref[0].astype(bf)
    kv = kv_ref[0].astype(bf)
    scale = MEM_HEAD_DIM ** -0.5
    for h in range(MEM_HEADS):
        sl = slice(h * MEM_HEAD_DIM, (h + 1) * MEM_HEAD_DIM)
        s = lax.dot_general(q[:, sl], kv[:, sl], _NT_DIMS, preferred_element_type=jnp.float32) * scale
        p = jnp.exp(s - jnp.max(s, axis=-1, keepdims=True))
        p = (p / jnp.sum(p, axis=-1, keepdims=True)).astype(bf)
        vh = kv[:, MEM_WIDTH + h * MEM_HEAD_DIM:MEM_WIDTH + (h + 1) * MEM_HEAD_DIM]
        o_ref[0, :, sl] = jnp.dot(p, vh, preferred_element_type=jnp.float32).astype(o_ref.dtype)


def memory_attention_pallas(q, kv):
    B, S, _ = q.shape
    M = kv.shape[1]
    return pl.pallas_call(
        _mem_attn_kernel,
        grid=(B, S // MEM_TQ),
        in_specs=[pl.BlockSpec((1, MEM_TQ, MEM_WIDTH), lambda b, i: (b, i, 0)),
                  pl.BlockSpec((1, M, 2 * MEM_WIDTH), lambda b, i: (b, 0, 0))],
        out_specs=pl.BlockSpec((1, MEM_TQ, MEM_WIDTH), lambda b, i: (b, i, 0)),
        out_shape=jax.ShapeDtypeStruct((B, S, MEM_WIDTH), jnp.bfloat16),
        compiler_params=_col_params(),
        name="mem_attn",
    )(q, kv)


MERGE_TM = 256


def _layer_norm_rows(y, g, b):
    mu = jnp.mean(y, axis=-1, keepdims=True)
    d = y - mu
    var = jnp.mean(d * d, axis=-1, keepdims=True)
    return d * lax.rsqrt(var + LN_EPS) * g + b


def _merge_kernel(a_ref, h_ref, m_ref, g_ref, x_ref, wb_ref, wo_ref, lg_ref, lb_ref, o_ref):
    bf = jnp.bfloat16
    f32 = jnp.float32
    merged = g_ref[:, 0:D_MODEL] * jnp.dot(a_ref[...].astype(bf), wb_ref[0], preferred_element_type=f32)
    merged = merged + g_ref[:, D_MODEL:2 * D_MODEL] * jnp.dot(h_ref[...].astype(bf), wb_ref[1], preferred_element_type=f32)
    merged = merged + g_ref[:, 2 * D_MODEL:] * jnp.dot(m_ref[...].astype(bf), wb_ref[2], preferred_element_type=f32)
    y = ALPHA * x_ref[...] + jnp.dot(merged.astype(bf), wo_ref[...], preferred_element_type=f32)
    o_ref[...] = _layer_norm_rows(y, lg_ref[...], lb_ref[...])


def merge_norm(att, hy, mem, gates, x, w_branch, w_out, ln_g, ln_b):
    T = x.shape[0]
    bf = jnp.bfloat16
    rows = lambda w: pl.BlockSpec((MERGE_TM, w), lambda i: (i, 0))
    whole = lambda a: pl.BlockSpec(a.shape, lambda i: (0,) * a.ndim)
    wb, wo = w_branch.astype(bf), w_out.astype(bf)
    lg, lb = ln_g.reshape(1, D_MODEL), ln_b.reshape(1, D_MODEL)
    return pl.pallas_call(
        _merge_kernel,
        grid=(T // MERGE_TM,),
        in_specs=[rows(BRANCH_WIDTH), rows(BRANCH_WIDTH), rows(BRANCH_WIDTH), rows(N_BRANCH * D_MODEL), rows(D_MODEL),
                  whole(wb), whole(wo), whole(lg), whole(lb)],
        out_specs=rows(D_MODEL),
        out_shape=jax.ShapeDtypeStruct((T, D_MODEL), jnp.float32),
        compiler_params=pltpu.CompilerParams(dimension_semantics=("arbitrary",), vmem_limit_bytes=VMEM_LIMIT_BYTES),
        name="merge_norm",
    )(att, hy, mem, gates, x, wb, wo, lg, lb)


def _residual_norm_kernel(x_ref, r_ref, lg_ref, lb_ref, o_ref):
    o_ref[...] = _layer_norm_rows(ALPHA * x_ref[...] + r_ref[...], lg_ref[...], lb_ref[...])


def residual_norm(x, r, ln_g, ln_b):
    T = x.shape[0]
    tm = 512
    rows = pl.BlockSpec((tm, D_MODEL), lambda i: (i, 0))
    one = pl.BlockSpec((1, D_MODEL), lambda i: (0, 0))
    return pl.pallas_call(
        _residual_norm_kernel,
        grid=(T // tm,),
        in_specs=[rows, rows, one, one],
        out_specs=rows,
        out_shape=jax.ShapeDtypeStruct((T, D_MODEL), jnp.float32),
        compiler_params=pltpu.CompilerParams(dimension_semantics=("arbitrary",), vmem_limit_bytes=VMEM_LIMIT_BYTES),
        name="residual_norm",
    )(x, r, ln_g.reshape(1, D_MODEL), ln_b.reshape(1, D_MODEL))


FFT_N1 = 256
FFT_N2 = 128
FFT_N = FFT_N1 * FFT_N2
HY_COLS = FFT_N2 * HYENA_WIDTH
LMUL_TN = 2048
KB_K1 = 8
FILT_TM = 512


def _dft_constants():
    f32 = jnp.float32
    n1 = jnp.arange(FFT_N1, dtype=jnp.int32)
    ang1 = ((n1[:, None] * n1[None, :]) % FFT_N1).astype(f32) * f32(2.0 * math.pi / FFT_N1)
    fr, fi = jnp.cos(ang1), -jnp.sin(ang1)
    h = FFT_N1 // 2
    f_fwd = jnp.concatenate([jnp.concatenate([fr[:, :h], -fi[:, :h]], axis=1),
                             jnp.concatenate([fi[:, :h], fr[:, :h]], axis=1)], axis=0)
    f_inv = f_fwd.T * f32(1.0 / FFT_N)
    f_real = jnp.concatenate([fr, fi], axis=0)
    k2 = jnp.arange(FFT_N2, dtype=jnp.int32)
    k = n1[:, None, None] + FFT_N1 * k2[None, :, None]
    ang2 = ((k * k2[None, None, :]) % FFT_N).astype(f32) * f32(2.0 * math.pi / FFT_N)
    gr, gi = jnp.cos(ang2), -jnp.sin(ang2)
    g = jnp.concatenate([jnp.concatenate([gr, -gi], axis=2), jnp.concatenate([gi, gr], axis=2)], axis=1)
    bf = jnp.bfloat16
    return f_fwd.astype(bf), f_inv.astype(bf), f_real.astype(bf), g.astype(bf), g.transpose(0, 2, 1).astype(bf)


def _lmul_pair_kernel(l_ref, xr_ref, xi_ref, o_ref):
    x = jnp.concatenate([xr_ref[0], xi_ref[0]], axis=0).astype(jnp.bfloat16)
    o_ref[0] = jnp.dot(l_ref[...], x, preferred_element_type=jnp.float32).astype(o_ref.dtype)


def _lmul_kernel(l_ref, x_ref, o_ref):
    o_ref[0] = jnp.dot(l_ref[...], x_ref[0].astype(jnp.bfloat16),
                       preferred_element_type=jnp.float32).astype(o_ref.dtype)


def _lmul_gate_kernel(l_ref, a_ref, zr_ref, zi_ref, pr_ref, pi_ref, b_ref, o_ref):
    y = jnp.dot(l_ref[...], a_ref[0], preferred_element_type=jnp.float32)
    h = FFT_N1 // 2
    o_ref[0, :h] = pr_ref[0] * (y[:h] + zr_ref[0] * b_ref[...])
    o_ref[0, h:] = pi_ref[0] * (y[h:] + zi_ref[0] * b_ref[...])


def dft_stage1_pairs(f_fwd, z):
    nb, h, cols = z.shape
    npair = (nb + 1) // 2
    last = nb - 1
    return pl.pallas_call(
        _lmul_pair_kernel,
        grid=(npair, cols // LMUL_TN),
        in_specs=[
            pl.BlockSpec(f_fwd.shape, lambda p, j: (0, 0)),
            pl.BlockSpec((1, h, LMUL_TN), lambda p, j: (jnp.minimum(2 * p, last), 0, j)),
            pl.BlockSpec((1, h, LMUL_TN), lambda p, j: (jnp.minimum(2 * p + 1, last), 0, j)),
        ],
        out_specs=pl.BlockSpec((1, 2 * FFT_N1, LMUL_TN), lambda p, j: (p, 0, j)),
        out_shape=jax.ShapeDtypeStruct((npair, 2 * FFT_N1, cols), jnp.bfloat16),
        compiler_params=_col_params(),
        name="dft_stage1",
    )(f_fwd, z, z)


def dft_stage1_real(f_real, x):
    _, h, cols = x.shape
    return pl.pallas_call(
        _lmul_kernel,
        grid=(1, cols // LMUL_TN),
        in_specs=[pl.BlockSpec(f_real.shape, lambda p, j: (0, 0)),
                  pl.BlockSpec((1, h, LMUL_TN), lambda p, j: (p, 0, j))],
        out_specs=pl.BlockSpec((1, 2 * FFT_N1, LMUL_TN), lambda p, j: (p, 0, j)),
        out_shape=jax.ShapeDtypeStruct((1, 2 * FFT_N1, cols), jnp.bfloat16),
        compiler_params=_col_params(),
        name="dft_stage1_real",
    )(f_real, x)


def idft_stage1_gate(f_inv, a, z, p, bias_row):
    npair, _, cols = a.shape
    half = FFT_N1 // 2

    def half_blk(x, r):
        last = x.shape[0] - 1
        return pl.BlockSpec((1, half, LMUL_TN), lambda q, j: (jnp.minimum(2 * q + r, last), 0, j))

    out = pl.pallas_call(
        _lmul_gate_kernel,
        grid=(npair, cols // LMUL_TN),
        in_specs=[pl.BlockSpec(f_inv.shape, lambda q, j: (0, 0)),
                  pl.BlockSpec((1, 2 * FFT_N1, LMUL_TN), lambda q, j: (q, 0, j)),
                  half_blk(z, 0), half_blk(z, 1), half_blk(p, 0), half_blk(p, 1),
                  pl.BlockSpec((1, LMUL_TN), lambda q, j: (0, 0))],
        out_specs=pl.BlockSpec((1, FFT_N1, LMUL_TN), lambda q, j: (q, 0, j)),
        out_shape=jax.ShapeDtypeStruct((npair, FFT_N1, cols), jnp.float32),
        compiler_params=_col_params(),
        name="idft_stage1_gate",
    )(f_inv, a, z, z, p, p, bias_row)
    return out.reshape(2 * npair, half, cols)


def _stage2_conv_kernel(a_ref, g_ref, gt_ref, h_ref, o_ref):
    for j in range(KB_K1):
        x = jnp.concatenate([a_ref[0, 0, j], a_ref[0, 1, j]], axis=0)
        z = jnp.dot(g_ref[j], x, preferred_element_type=jnp.float32)
        zr, zi = z[:FFT_N2], z[FFT_N2:]
        hr, hi = h_ref[0, j], h_ref[1, j]
        w = jnp.concatenate([zr * hr - zi * hi, zr * hi + zi * hr], axis=0).astype(jnp.bfloat16)
        y = jnp.dot(gt_ref[j], w, preferred_element_type=jnp.float32)
        o_ref[0, 0, j] = y[:FFT_N2].astype(o_ref.dtype)
        o_ref[0, 1, j] = y[FFT_N2:].astype(o_ref.dtype)


def stage2_conv(a, g, gt, hf, order):
    npair = a.shape[0]
    C = HYENA_WIDTH
    ablk = pl.BlockSpec((1, 2, KB_K1, FFT_N2, C), lambda p, i: (p, 0, i, 0, 0))
    gblk = pl.BlockSpec((KB_K1, 2 * FFT_N2, 2 * FFT_N2), lambda p, i: (i, 0, 0))
    return pl.pallas_call(
        _stage2_conv_kernel,
        grid=(npair, FFT_N1 // KB_K1),
        in_specs=[ablk, gblk, gblk,
                  pl.BlockSpec((2, KB_K1, FFT_N2, C), lambda p, i: (0, i, 0, order))],
        out_specs=ablk,
        out_shape=jax.ShapeDtypeStruct(a.shape, a.dtype),
        compiler_params=_col_params(),
        name="stage2_conv",
    )(a, g, gt, hf)


def _stage2_filter_kernel(a_ref, g_ref, s_ref, o_ref):
    for j in range(KB_K1):
        x = jnp.concatenate([a_ref[0, j], a_ref[1, j]], axis=0)
        z = jnp.dot(g_ref[j], x, preferred_element_type=jnp.float32) * s_ref[...]
        o_ref[0, j] = z[:FFT_N2]
        o_ref[1, j] = z[FFT_N2:]


def stage2_filter(a, g, inv_norm):
    W = a.shape[-1]
    blk = pl.BlockSpec((2, KB_K1, FFT_N2, W), lambda i: (0, i, 0, 0))
    return pl.pallas_call(
        _stage2_filter_kernel,
        grid=(FFT_N1 // KB_K1,),
        in_specs=[blk, pl.BlockSpec((KB_K1, 2 * FFT_N2, 2 * FFT_N2), lambda i: (i, 0, 0)),
                  pl.BlockSpec((1, W), lambda i: (0, 0))],
        out_specs=blk,
        out_shape=jax.ShapeDtypeStruct(a.shape, jnp.float32),
        compiler_params=pltpu.CompilerParams(dimension_semantics=("arbitrary",), vmem_limit_bytes=VMEM_LIMIT_BYTES),
        name="stage2_filter",
    )(a, g, inv_norm)


def _filter_gen_kernel(bands_ref, w0_ref, wc_ref, ws_ref, wmid_ref, fb_ref, ff_ref, wdir_ref, wbwd_ref,
                       adelta_ref, two_ref, norm_ref, *, seq):
    i = pl.program_id(0)
    bf = jnp.bfloat16
    row = i * FILT_TM + lax.broadcasted_iota(jnp.int32, (FILT_TM, 1), 0)
    pos_i = jnp.where(row < seq, row, 2 * seq - row)
    pos = pos_i.astype(jnp.float32)
    t = pos / max(seq - 1, 1)
    w = (jnp.float32(2.0 * math.pi) * pos) / seq
    ang = w * bands_ref[...]
    pre = (jnp.dot(jnp.cos(ang).astype(bf), wc_ref[...], preferred_element_type=jnp.float32)
           + jnp.dot((-jnp.sin(ang)).astype(bf), ws_ref[...], preferred_element_type=jnp.float32)
           + t.astype(bf).astype(jnp.float32) * w0_ref[...])
    h = jnp.sin(ff_ref[0:1, :] * (pre + fb_ref[0:1, :]))
    for m in range(FILTER_INNER):
        pre = jnp.dot(h.astype(bf), wmid_ref[m], preferred_element_type=jnp.float32)
        h = jnp.sin(ff_ref[m + 1:m + 2, :] * (pre + fb_ref[m + 1:m + 2, :]))
    hb = h.astype(bf)
    decay = jnp.exp(-t * adelta_ref[...])
    out = jnp.dot(hb, wdir_ref[0], preferred_element_type=jnp.float32) * decay
    out = jnp.where(pos_i < seq, out, 0.0)

    @pl.when(i == 0)
    def _():
        extra = jnp.dot(hb, wbwd_ref[...], preferred_element_type=jnp.float32) * decay
        first = jnp.where(row == 0, out + extra, out)
        two_ref[...] = first
        norm_ref[...] = jnp.sum(jnp.abs(first), axis=0, keepdims=True)

    @pl.when(i != 0)
    def _():
        two_ref[...] = out
        norm_ref[...] += jnp.sum(jnp.abs(out), axis=0, keepdims=True)


def hyena_filter_rows(seq, f_w_in, f_w_mid, f_b, f_freq, f_w_out):
    f32, bf = jnp.float32, jnp.bfloat16
    C, H = HYENA_WIDTH, FILTER_HIDDEN
    bands = jnp.linspace(1e-4, FILTER_BANDS - 1, FILTER_BANDS, dtype=f32)
    bands = jnp.pad(bands, (0, VREG_LANES - FILTER_BANDS)).reshape(1, VREG_LANES)
    w_in = f_w_in.astype(bf)
    w0 = w_in[0:1].astype(f32)
    pad_rows = ((0, VREG_LANES - FILTER_BANDS), (0, 0))
    wc = jnp.pad(w_in[1:1 + FILTER_BANDS], pad_rows)
    ws = jnp.pad(w_in[1 + FILTER_BANDS:], pad_rows)
    w_out = f_w_out.astype(bf).reshape(H, HYENA_ORDER, 2, C)
    wdir = jnp.stack([w_out[:, :, 0].reshape(H, HYENA_ORDER * C), w_out[:, :, 1].reshape(H, HYENA_ORDER * C)])
    max_decay = math.log(DECAY_TARGET) / FAST_DECAY_PCT
    min_decay = math.log(DECAY_TARGET) / SLOW_DECAY_PCT
    adelta = jnp.abs(jnp.linspace(min_decay, max_decay, C, dtype=f32))
    adelta = jnp.tile(adelta, HYENA_ORDER).reshape(1, HYENA_ORDER * C)
    n_tiles = 2 * seq // FILT_TM
    half_tiles = seq // FILT_TM
    whole = lambda a: pl.BlockSpec(a.shape, lambda i: (0,) * a.ndim)
    wmid = f_w_mid.astype(bf)
    fb, ff = f_b.astype(f32), f_freq.astype(f32)
    return pl.pallas_call(
        functools.partial(_filter_gen_kernel, seq=seq),
        grid=(n_tiles,),
        in_specs=[whole(bands), whole(w0), whole(wc), whole(ws), whole(wmid), whole(fb), whole(ff),
                  pl.BlockSpec((1, H, HYENA_ORDER * C), lambda i: (i // half_tiles, 0, 0)),
                  pl.BlockSpec((None, H, HYENA_ORDER * C), lambda i: (1, 0, 0)),
                  whole(adelta)],
        out_specs=[pl.BlockSpec((FILT_TM, HYENA_ORDER * C), lambda i: (i, 0)),
                   pl.BlockSpec((1, HYENA_ORDER * C), lambda i: (0, 0))],
        out_shape=[jax.ShapeDtypeStruct((2 * seq, HYENA_ORDER * C), f32),
                   jax.ShapeDtypeStruct((1, HYENA_ORDER * C), f32)],
        compiler_params=pltpu.CompilerParams(dimension_semantics=("arbitrary",), vmem_limit_bytes=VMEM_LIMIT_BYTES),
        name="hyena_filter_gen",
    )(bands, w0, wc, ws, wmid, fb, ff, wdir, wdir, adelta)


def _short_conv_kernel(x_ref, w_ref, b_ref, o_ref):
    x = x_ref[0]
    n = x.shape[0]
    t = lax.broadcasted_iota(jnp.int32, x.shape, 0)
    prev = jnp.where(t == 0, 0.0, pltpu.roll(x, 1, 0))
    nxt = jnp.where(t == n - 1, 0.0, pltpu.roll(x, n - 1, 0))
    o_ref[0] = ((b_ref[...] + prev * w_ref[0:1, :]) + x * w_ref[1:2, :]) + nxt * w_ref[2:3, :]


def short_conv_pallas(u, w, b):
    B, L, W = u.shape
    tc = VREG_LANES
    return pl.pallas_call(
        _short_conv_kernel,
        grid=(B, W // tc),
        in_specs=[pl.BlockSpec((1, L, tc), lambda i, j: (i, 0, j)),
                  pl.BlockSpec((SHORT_CONV, tc), lambda i, j: (0, j)),
                  pl.BlockSpec((1, tc), lambda i, j: (0, j))],
        out_specs=pl.BlockSpec((1, L, tc), lambda i, j: (i, 0, j)),
        out_shape=jax.ShapeDtypeStruct((B, L, W), jnp.float32),
        compiler_params=_col_params(),
        name="short_conv",
    )(u, w, b.reshape(1, W))


def hyena_mixer_pallas(u, conv_w, conv_b, f_w_in, f_w_mid, f_b, f_freq, f_w_out, hyena_bias):
    B, L, _ = u.shape
    assert 2 * L == FFT_N
    C = HYENA_WIDTH
    f_fwd, f_inv, f_real, g, gt = _dft_constants()
    two, norm = hyena_filter_rows(L, f_w_in, f_w_mid, f_b, f_freq, f_w_out)
    af = dft_stage1_real(f_real, two.reshape(1, FFT_N1, FFT_N2 * HYENA_ORDER * C))
    af = af.reshape(2, FFT_N1, FFT_N2, HYENA_ORDER * C)
    hf = stage2_filter(af, g, 1.0 / norm)
    uc = short_conv_pallas(u, conv_w, conv_b)
    half = FFT_N1 // 2
    parts = [uc[..., o * C:(o + 1) * C].reshape(B, half, HY_COLS) for o in range(HYENA_ORDER + 1)]
    z = parts[0]
    for o in range(HYENA_ORDER):
        a = dft_stage1_pairs(f_fwd, z)
        npair = a.shape[0]
        a = stage2_conv(a.reshape(npair, 2, FFT_N1, FFT_N2, C), g, gt, hf, o)
        bias_row = jnp.tile(hyena_bias[o], LMUL_TN // C).reshape(1, LMUL_TN)
        z = idft_stage1_gate(f_inv, a.reshape(npair, 2 * FFT_N1, HY_COLS), z, parts[o + 1], bias_row)
    return z[:B].reshape(B, L, C)


PEER_SEL = PEER_HEADS * PEER_TOPK
PEER_TB = 128
VREG_SUBLANES = 8
VREG_LANES = 128
EXPERT_ROWS = 4
TILE_ROWS = 16
V_UNROLL = 16
IDX_SPLIT = 16
IDX_PER = PEER_SEL // IDX_SPLIT


def _row_reader(idx_refs, t):
    offs = [t * IDX_PER + j for j in range(IDX_PER)]
    return lambda k: idx_refs[k // IDX_PER][offs[k % IDX_PER]]


def pack_expert_table(tab):
    e = tab.shape[0]
    b = lax.bitcast_convert_type(tab.astype(jnp.bfloat16), jnp.uint16).astype(jnp.uint32)
    b = b.reshape(e, 2, EXPERT_ROWS, VREG_LANES)
    w = (b[:, 0] | (b[:, 1] << 16)).reshape(e * EXPERT_ROWS, VREG_LANES)
    return jnp.pad(w, ((0, VREG_SUBLANES), (0, 0)))


def from_tile_rows(y):
    T = y.shape[0]
    return y.reshape(T, EXPERT_ROWS, 2, VREG_LANES).transpose(0, 2, 1, 3).reshape(T, D_MODEL)


def _split_bf16(v):
    hi = v.astype(jnp.bfloat16)
    return hi, (v - hi.astype(jnp.float32)).astype(jnp.bfloat16)


def _gelu_exact(x):
    return 0.5 * x * (1.0 + lax.erf(x * (2.0 ** -0.5)))


_COMBINE_POS = (3, 7, 1, 5, 2, 6, 0, 4)


def _sublane_sums(c, sub):
    mv = (sub & 2) != 0
    e = []
    for c1, c2 in ((c[0], c[1]), (c[2], c[3])):
        e.append(jnp.where(mv, c1 + pltpu.roll(c1, 2, 0), c2 + pltpu.roll(c2, 6, 0)))
    mo = (sub & 1) != 0
    return jnp.where(mo, e[0] + pltpu.roll(e[0], 1, 0), e[1] + pltpu.roll(e[1], 7, 0))


def _peer_u_kernel(*refs):
    idx_refs = refs[:IDX_SPLIT]
    x_ref, g_ref, tab_ref, o_ref, act_ref, r_ref = refs[IDX_SPLIT:]
    sub = lax.broadcasted_iota(jnp.int32, (VREG_SUBLANES, VREG_LANES), 0)
    lo4 = sub < EXPERT_ROWS

    def token(t, carry):
        sel_row = _row_reader(idx_refs, t)
        x8 = x_ref[t]
        xr = pltpu.roll(x8, EXPERT_ROWS, 0)
        x_lo = jnp.where(lo4, x8, xr)
        x_hi = jnp.where(lo4, xr, x8)
        rs = []
        for grp in range(PEER_SEL // VREG_SUBLANES):
            pairs = []
            for i in range(VREG_SUBLANES // 2):
                wa = tab_ref[pl.ds(sel_row(grp * VREG_SUBLANES + _COMBINE_POS[2 * i]), VREG_SUBLANES), :]
                wb = tab_ref[pl.ds(sel_row(grp * VREG_SUBLANES + _COMBINE_POS[2 * i + 1]), VREG_SUBLANES), :]
                w = jnp.where(lo4, wa, pltpu.roll(wb, EXPERT_ROWS, 0))
                lo = lax.bitcast_convert_type(w << 16, jnp.float32)
                hi = lax.bitcast_convert_type(w & jnp.uint32(0xFFFF0000), jnp.float32)
                pairs.append(lo * x_lo + hi * x_hi)
            rs.append(_sublane_sums(pairs, sub))
        r0 = pl.multiple_of(t * PEER_SEL, PEER_SEL)
        r_ref[pl.ds(r0, PEER_SEL), :] = jnp.concatenate(rs, axis=0).astype(jnp.bfloat16)
        return carry

    lax.fori_loop(0, PEER_TB, token, 0)
    ones = jnp.ones((VREG_LANES, VREG_LANES), jnp.bfloat16)
    keep = (lax.broadcasted_iota(jnp.int32, (PEER_SEL, VREG_LANES), 0)
            == lax.broadcasted_iota(jnp.int32, (PEER_SEL, VREG_LANES), 1))
    chunk = VREG_SUBLANES * PEER_SEL
    for c in range(PEER_TB // VREG_SUBLANES):
        s = jnp.dot(r_ref[c * chunk:(c + 1) * chunk, :], ones, preferred_element_type=jnp.float32)
        for j in range(VREG_SUBLANES):
            blk = jnp.where(keep, s[j * PEER_SEL:(j + 1) * PEER_SEL], 0.0)
            act_ref[c * VREG_SUBLANES + j:c * VREG_SUBLANES + j + 1, :] = jnp.sum(blk, axis=0, keepdims=True)
    o_ref[...] = g_ref[...] * _gelu_exact(act_ref[...])


def _peer_v_kernel(*refs):
    idx_refs = refs[:IDX_SPLIT]
    coef_ref, e_ref, d_ref, tab_ref, o_ref, chi_ref, clo_ref = refs[IDX_SPLIT:]
    c_hi, c_lo = _split_bf16(coef_ref[...])
    chi_ref[...] = jnp.dot(c_hi, e_ref[...], preferred_element_type=jnp.float32)
    clo_ref[...] = jnp.dot(c_lo, e_ref[...], preferred_element_type=jnp.float32)

    def token(t, carry):
        sel_row = _row_reader(idx_refs, t)
        cmat = jnp.concatenate([chi_ref[pl.ds(t, 1), :] * d_ref[...], clo_ref[pl.ds(t, 1), :] * d_ref[...]],
                               axis=0).astype(jnp.bfloat16)
        w = jnp.concatenate([pltpu.bitcast(tab_ref[pl.ds(sel_row(k), VREG_SUBLANES), :], jnp.bfloat16)
                             for k in range(PEER_SEL)], axis=0)
        acc = jnp.dot(cmat, w, preferred_element_type=jnp.float32)
        o_ref[t] = acc[:VREG_SUBLANES] + acc[VREG_SUBLANES:]
        return carry

    lax.fori_loop(0, PEER_TB, token, 0, unroll=V_UNROLL)


def peer_experts(x, eidx, g, tab_u, tab_v):
    T = x.shape[0]
    grid = (T // PEER_TB,)
    f32, bf = jnp.float32, jnp.bfloat16
    sel = jnp.arange(PEER_SEL, dtype=jnp.int32)
    col = jnp.arange(PEER_SEL * TILE_ROWS, dtype=jnp.int32)
    expand = (col[None, :] // TILE_ROWS == sel[:, None]).astype(bf)
    diag = (col[None, :] % TILE_ROWS == jnp.arange(VREG_SUBLANES, dtype=jnp.int32)[:, None]).astype(f32)
    smem_blk = pl.BlockSpec((PEER_TB * IDX_PER,), lambda i: (i,), memory_space=pltpu.SMEM)
    idx_parts = [eidx[:, j * IDX_PER:(j + 1) * IDX_PER].reshape(T * IDX_PER) for j in range(IDX_SPLIT)]
    vec_blk = pl.BlockSpec((PEER_TB, PEER_SEL), lambda i: (i, 0))
    row_blk = pl.BlockSpec((PEER_TB, VREG_SUBLANES, VREG_LANES), lambda i: (i, 0, 0))
    whole = lambda a: pl.BlockSpec(a.shape, lambda i: (0,) * a.ndim)
    tab_spec = pl.BlockSpec(memory_space=pltpu.VMEM)
    params = pltpu.CompilerParams(dimension_semantics=("arbitrary",), vmem_limit_bytes=VMEM_LIMIT_BYTES)
    coef = pl.pallas_call(
        _peer_u_kernel,
        grid=grid,
        in_specs=[smem_blk] * IDX_SPLIT + [row_blk, vec_blk, tab_spec],
        out_specs=vec_blk,
        out_shape=jax.ShapeDtypeStruct((T, PEER_SEL), f32),
        scratch_shapes=[pltpu.VMEM((PEER_TB, PEER_SEL), f32),
                        pltpu.VMEM((PEER_TB * PEER_SEL, VREG_LANES), bf)],
        compiler_params=params,
        name="peer_u",
    )(*idx_parts, x.reshape(T, VREG_SUBLANES, VREG_LANES), g, tab_u)
    out = pl.pallas_call(
        _peer_v_kernel,
        grid=grid,
        in_specs=[smem_blk] * IDX_SPLIT + [vec_blk, whole(expand), whole(diag), tab_spec],
        out_specs=row_blk,
        out_shape=jax.ShapeDtypeStruct((T, VREG_SUBLANES, VREG_LANES), f32),
        scratch_shapes=[pltpu.VMEM((PEER_TB, PEER_SEL * TILE_ROWS), f32),
                        pltpu.VMEM((PEER_TB, PEER_SEL * TILE_ROWS), f32)],
        compiler_params=params,
        name="peer_v",
    )(*idx_parts, coef, expand, diag, tab_v)
    return from_tile_rows(out)


ROUTE_TM = 512


def _top16_rows(s, key_id):
    row16 = lax.broadcasted_iota(jnp.int32, (PEER_TOPK, VREG_LANES), 0)
    vals = jnp.zeros((PEER_TOPK, VREG_LANES), jnp.float32)
    ids = jnp.zeros((PEER_TOPK, VREG_LANES), jnp.float32)
    big = jnp.float32(2 ** 30)
    for j in range(PEER_TOPK):
        m = jnp.max(s, axis=0, keepdims=True)
        am = jnp.min(jnp.where(s == m, key_id, big), axis=0, keepdims=True)
        vals = jnp.where(row16 == j, m, vals)
        ids = jnp.where(row16 == j, am, ids)
        s = jnp.where(key_id == am, -jnp.inf, s)
    return vals, ids


_PAIR_GROUPS = ((0, 0, 8), (0, 8, 8), (1, 0, 8), (2, 0, 5), (3, 0, 4), (4, 0, 3), (5, 0, 2), (6, 0, 2), (7, 0, 2))


def _route_head(s0, i0, s1, i1):
    sub_i = lax.broadcasted_iota(jnp.int32, (VREG_SUBLANES, VREG_LANES), 0)
    sub = sub_i.astype(jnp.float32)
    cands, flat, eids = [], [], []
    for a, b0, nb in _PAIR_GROUPS:
        c = s0[a:a + 1] + s1[b0:b0 + VREG_SUBLANES]
        cands.append(jnp.where(sub_i < nb, c, -jnp.inf) if nb < VREG_SUBLANES else c)
        flat.append(a * PEER_TOPK + b0 + sub)
        eids.append(i0[a:a + 1] * N_KEYS + i1[b0:b0 + VREG_SUBLANES])
    cands.append(s0[VREG_SUBLANES:] + s1[0:1])
    flat.append((sub + VREG_SUBLANES) * PEER_TOPK)
    eids.append(i0[VREG_SUBLANES:] * N_KEYS + i1[0:1])
    cand = jnp.concatenate(cands, axis=0)
    flat = jnp.concatenate(flat, axis=0)
    eid = jnp.concatenate(eids, axis=0)
    row16 = lax.broadcasted_iota(jnp.int32, (PEER_TOPK, VREG_LANES), 0)
    sc = jnp.zeros((PEER_TOPK, VREG_LANES), jnp.float32)
    sel = jnp.zeros((PEER_TOPK, VREG_LANES), jnp.float32)
    big = jnp.float32(2 ** 30)
    for j in range(PEER_TOPK):
        m = jnp.max(cand, axis=0, keepdims=True)
        am = jnp.min(jnp.where(cand == m, flat, big), axis=0, keepdims=True)
        hit = flat == am
        e = jnp.max(jnp.where(hit, eid, -1.0), axis=0, keepdims=True)
        sc = jnp.where(row16 == j, m, sc)
        sel = jnp.where(row16 == j, e, sel)
        cand = jnp.where(hit, -jnp.inf, cand)
    p = jnp.exp(sc - sc[0:1])
    return sel, p / jnp.sum(p, axis=0, keepdims=True)


def _route_kernel(x_ref, wq_ref, sk_ref, rows_ref, g_ref, q_ref):
    q = jnp.dot(x_ref[...].astype(jnp.bfloat16), wq_ref[...], preferred_element_type=jnp.float32)
    q_ref[...] = q.astype(jnp.bfloat16)
    key_id = lax.broadcasted_iota(jnp.int32, (N_KEYS, VREG_LANES), 0).astype(jnp.float32)

    def head(h, carry):
        tops = []
        for c in range(2):
            hc = h * 2 + c
            qhc = q_ref[:, pl.ds(pl.multiple_of(hc * PEER_HALF, PEER_HALF), PEER_HALF)]
            s = lax.dot_general(sk_ref[hc], qhc, _NT_DIMS, preferred_element_type=jnp.float32)
            tops.append([_top16_rows(s[:, j * VREG_LANES:(j + 1) * VREG_LANES], key_id)
                         for j in range(ROUTE_TM // VREG_LANES)])
        r0 = pl.multiple_of(h * PEER_TOPK, PEER_TOPK)
        for j in range(ROUTE_TM // VREG_LANES):
            (s0, i0), (s1, i1) = tops[0][j], tops[1][j]
            sel, g = _route_head(s0, i0, s1, i1)
            rows_ref[pl.ds(r0, PEER_TOPK), j * VREG_LANES:(j + 1) * VREG_LANES] = (sel * EXPERT_ROWS).astype(jnp.int32)
            g_ref[pl.ds(r0, PEER_TOPK), j * VREG_LANES:(j + 1) * VREG_LANES] = g
        return carry

    lax.fori_loop(0, PEER_HEADS, head, 0)


def peer_route(x, w_query, sub_keys):
    T = x.shape[0]
    wq = w_query.astype(jnp.bfloat16)
    sk = sub_keys.reshape(PEER_HEADS * 2, N_KEYS, PEER_HALF).astype(jnp.bfloat16)
    out_blk = pl.BlockSpec((PEER_SEL, ROUTE_TM), lambda i: (0, i))
    rows, g = pl.pallas_call(
        _route_kernel,
        grid=(T // ROUTE_TM,),
        in_specs=[
            pl.BlockSpec((ROUTE_TM, D_MODEL), lambda i: (i, 0)),
            pl.BlockSpec(wq.shape, lambda i: (0, 0)),
            pl.BlockSpec(sk.shape, lambda i: (0, 0, 0)),
        ],
        out_specs=[out_blk, out_blk],
        out_shape=[jax.ShapeDtypeStruct((PEER_SEL, T), jnp.int32),
                   jax.ShapeDtypeStruct((PEER_SEL, T), jnp.float32)],
        scratch_shapes=[pltpu.VMEM((ROUTE_TM, PEER_HEADS * PEER_QDIM), jnp.bfloat16)],
        compiler_params=pltpu.CompilerParams(dimension_semantics=("arbitrary",),
                                             vmem_limit_bytes=VMEM_LIMIT_BYTES),
        name="peer_route",
    )(x, wq, sk)
    return rows.T, g.T


def encoder_layer(x, mem, rel_bias, w_in, b_in, conv_w, conv_b, attn_sink, f_w_in, f_w_mid, f_b, f_freq,
                  f_w_out, hyena_bias, w_mem_kv, w_branch, w_out, ln1_g, ln1_b, w_query, sub_keys,
                  expert_u, expert_v, ln2_g, ln2_b):
    B, S, _ = x.shape
    T = B * S
    xf = x.reshape(T, D_MODEL)
    proj = lambda lo, hi, **kw: linear(xf, w_in[:, lo:hi], b_in[lo:hi], **kw)
    qkv = proj(0, O_HY).reshape(B, S, O_HY)
    hy = proj(O_HY, O_MQ).reshape(B, S, O_MQ - O_HY)
    q_m = proj(O_MQ, O_GATE).reshape(B, S, MEM_WIDTH)
    gates = proj(O_GATE, IN_WIDTH, tn=(IN_WIDTH - O_GATE) // 2, sigmoid=True)
    M = mem.shape[1]
    kv = linear(mem.reshape(B * M, D_MODEL), w_mem_kv, jnp.zeros((2 * MEM_WIDTH,), jnp.float32), tm=B * M)
    att = window_attention_pallas(qkv, window_bias_table(rel_bias), attn_sink)
    hyo = hyena_mixer_pallas(hy, conv_w, conv_b, f_w_in, f_w_mid, f_b, f_freq, f_w_out, hyena_bias)
    mat = memory_attention_pallas(q_m, kv.reshape(B, M, 2 * MEM_WIDTH))
    x1 = merge_norm(att.reshape(T, ATT_WIDTH), hyo.reshape(T, HYENA_WIDTH), mat.reshape(T, MEM_WIDTH), gates, xf,
                    w_branch, w_out, ln1_g, ln1_b)
    rows, g = peer_route(x1, w_query, sub_keys)
    peer = peer_experts(x1, rows, g, pack_expert_table(expert_u), pack_expert_table(expert_v))
    return residual_norm(x1, peer, ln2_g, ln2_b).reshape(B, S, D_MODEL)


def kernel(x_prompt, x_sample, mem_prompt, mem_sample, rel_bias, w_in, b_in, conv_w, conv_b, attn_sink,
           f_w_in, f_w_mid, f_b, f_freq, f_w_out, hyena_bias, w_mem_kv, w_branch, w_out, ln1_g, ln1_b,
           w_query, sub_keys, expert_u, expert_v, ln2_g, ln2_b):
    nb = x_prompt.shape[0]
    x = jnp.concatenate([x_prompt, x_sample], axis=0)
    mem = jnp.concatenate([mem_prompt, mem_sample], axis=0)
    for l in range(DEPTH):
        x = encoder_layer(x, mem, rel_bias, w_in[l], b_in[l], conv_w[l], conv_b[l], attn_sink[l],
                          f_w_in[l], f_w_mid[l], f_b[l], f_freq[l], f_w_out[l], hyena_bias[l],
                          w_mem_kv[l], w_branch[l], w_out[l], ln1_g[l], ln1_b[l], w_query[l],
                          sub_keys[l], expert_u[l], expert_v[l], ln2_g[l], ln2_b[l])
    return (x[:nb], x[nb:])
```

```python
import functools
import math

import jax
import jax.numpy as jnp
from jax import lax
from jax.experimental import pallas as pl
from jax.experimental.pallas import tpu as pltpu

D_MODEL = 1024
DEPTH = 2
N_MEM = 256
ATT_HEADS = 8
ATT_KV_HEADS = 2
ATT_HEAD_DIM = 64
ATT_WIDTH = ATT_HEADS * ATT_HEAD_DIM
ATT_KV_WIDTH = ATT_KV_HEADS * ATT_HEAD_DIM
WINDOW = 128
BLOCK = 128
N_BUCKETS = 32
MAX_DISTANCE = 128
HYENA_WIDTH = 512
HYENA_ORDER = 2
SHORT_CONV = 3
FILTER_EMB = 33
FILTER_BANDS = (FILTER_EMB - 1) // 2
FILTER_HIDDEN = 64
FILTER_INNER = 2
FAST_DECAY_PCT = 0.3
SLOW_DECAY_PCT = 1.5
DECAY_TARGET = 1e-2
MEM_HEADS = 4
MEM_HEAD_DIM = 128
MEM_WIDTH = MEM_HEADS * MEM_HEAD_DIM
N_BRANCH = 3
BRANCH_WIDTH = 512
PEER_HEADS = 8
N_KEYS = 128
N_EXPERTS = N_KEYS * N_KEYS
PEER_TOPK = 16
PEER_HALF = 128
PEER_QDIM = 2 * PEER_HALF
PEER_BLOCK = 128
O_K = ATT_WIDTH
O_V = O_K + ATT_KV_WIDTH
O_HY = O_V + ATT_KV_WIDTH
O_MQ = O_HY + (HYENA_ORDER + 1) * HYENA_WIDTH
O_GATE = O_MQ + MEM_WIDTH
IN_WIDTH = O_GATE + N_BRANCH * D_MODEL
ALPHA = (2 * DEPTH) ** 0.25
BETA = (8 * DEPTH) ** -0.25
LN_EPS = 1e-5
NEG_INF = -1e30

VMEM_LIMIT_BYTES = 56 * 1024 * 1024


def _linear_kernel(x_ref, w_ref, b_ref, o_ref, *, sigmoid):
    x = x_ref[...].astype(jnp.bfloat16)
    acc = jnp.dot(x, w_ref[...], preferred_element_type=jnp.float32) + b_ref[...]
    o_ref[...] = jax.nn.sigmoid(acc) if sigmoid else acc


def linear(x, w, b, *, tm=512, tn=None, sigmoid=False):
    T, K = x.shape
    N = w.shape[1]
    tn = N if tn is None else tn
    wb = w.astype(jnp.bfloat16)
    return pl.pallas_call(
        functools.partial(_linear_kernel, sigmoid=sigmoid),
        grid=(N // tn, T // tm),
        in_specs=[
            pl.BlockSpec((tm, K), lambda j, i: (i, 0)),
            pl.BlockSpec((K, tn), lambda j, i: (0, j)),
            pl.BlockSpec((1, tn), lambda j, i: (0, j)),
        ],
        out_specs=pl.BlockSpec((tm, tn), lambda j, i: (i, j)),
        out_shape=jax.ShapeDtypeStruct((T, N), jnp.float32),
        compiler_params=pltpu.CompilerParams(
            dimension_semantics=("arbitrary", "arbitrary"),
            vmem_limit_bytes=VMEM_LIMIT_BYTES),
        name="linear",
    )(x, wb, b.reshape(1, N))


ATT_TQ = 512
ATT_KEYS = 3 * BLOCK
_NT_DIMS = (((1,), (1,)), ((), ()))


def t5_bucket(rel):
    nb = N_BUCKETS // 2
    max_exact = nb // 2
    ret = jnp.where(rel > 0, nb, 0)
    n = jnp.abs(rel)
    nf = jnp.maximum(n, 1).astype(jnp.float32)
    large = max_exact + (jnp.log(nf / max_exact) / math.log(MAX_DISTANCE / max_exact)
                         * (nb - max_exact)).astype(jnp.int32)
    large = jnp.minimum(large, nb - 1)
    return ret + jnp.where(n < max_exact, n, large)


def window_bias_table(rel_bias):
    rel = (jnp.arange(ATT_KEYS) - BLOCK)[None, :] - jnp.arange(BLOCK)[:, None]
    bias = rel_bias[t5_bucket(rel)].astype(jnp.float32).transpose(2, 0, 1)
    return jnp.where((jnp.abs(rel) <= WINDOW)[None], bias, NEG_INF)


def _col_params():
    return pltpu.CompilerParams(dimension_semantics=("arbitrary", "arbitrary"), vmem_limit_bytes=VMEM_LIMIT_BYTES)


def _window_attn_kernel(sink_ref, q_ref, kp_ref, kc_ref, kn_ref, vp_ref, vc_ref, vn_ref, bias_ref, o_ref, *, n_steps):
    i = pl.program_id(1)
    bf = jnp.bfloat16
    q = q_ref[0].astype(bf)
    k_all = jnp.concatenate([kp_ref[0], kc_ref[0], kn_ref[0]], axis=0).astype(bf)
    v_all = jnp.concatenate([vp_ref[0], vc_ref[0], vn_ref[0]], axis=0).astype(bf)
    col = lax.broadcasted_iota(jnp.int32, (BLOCK, ATT_KEYS), 1)
    n_blk = ATT_TQ // BLOCK
    scale = ATT_HEAD_DIM ** -0.5
    group = ATT_HEADS // ATT_KV_HEADS
    for j in range(n_blk):
        kj = k_all[j * BLOCK:j * BLOCK + ATT_KEYS]
        vj = v_all[j * BLOCK:j * BLOCK + ATT_KEYS]
        off_seq = None
        if j == 0:
            off_seq = (i == 0) & (col < BLOCK)
        if j == n_blk - 1:
            last = (i == n_steps - 1) & (col >= 2 * BLOCK)
            off_seq = last if off_seq is None else off_seq | last
        for h in range(ATT_HEADS):
            hk = h // group
            qh = q[j * BLOCK:(j + 1) * BLOCK, h * ATT_HEAD_DIM:(h + 1) * ATT_HEAD_DIM]
            kh = kj[:, hk * ATT_HEAD_DIM:(hk + 1) * ATT_HEAD_DIM]
            s = lax.dot_general(qh, kh, _NT_DIMS, preferred_element_type=jnp.float32) * scale + bias_ref[h]
            if off_seq is not None:
                s = jnp.where(off_seq, NEG_INF, s)
            sink = sink_ref[h]
            m = jnp.maximum(jnp.max(s, axis=-1, keepdims=True), sink)
            p = jnp.exp(s - m)
            denom = jnp.sum(p, axis=-1, keepdims=True) + jnp.exp(sink - m)
            oh = jnp.dot((p / denom).astype(bf), vj[:, hk * ATT_HEAD_DIM:(hk + 1) * ATT_HEAD_DIM],
                         preferred_element_type=jnp.float32)
            o_ref[0, j * BLOCK:(j + 1) * BLOCK, h * ATT_HEAD_DIM:(h + 1) * ATT_HEAD_DIM] = oh.astype(o_ref.dtype)


def window_attention_pallas(qkv, bias, sink):
    B, S, _ = qkv.shape
    n_steps = S // ATT_TQ
    per = ATT_TQ // BLOCK
    last_blk = S // BLOCK - 1
    k_col, v_col = O_K // ATT_KV_WIDTH, O_V // ATT_KV_WIDTH

    def edge(col, nxt):
        if nxt:
            return pl.BlockSpec((1, BLOCK, ATT_KV_WIDTH), lambda b, i: (b, jnp.minimum((i + 1) * per, last_blk), col))
        return pl.BlockSpec((1, BLOCK, ATT_KV_WIDTH), lambda b, i: (b, jnp.maximum(i * per - 1, 0), col))

    cur = lambda col: pl.BlockSpec((1, ATT_TQ, ATT_KV_WIDTH), lambda b, i: (b, i, col))
    return pl.pallas_call(
        functools.partial(_window_attn_kernel, n_steps=n_steps),
        grid=(B, n_steps),
        in_specs=[pl.BlockSpec(memory_space=pltpu.SMEM),
                  pl.BlockSpec((1, ATT_TQ, ATT_WIDTH), lambda b, i: (b, i, 0)),
                  edge(k_col, False), cur(k_col), edge(k_col, True),
                  edge(v_col, False), cur(v_col), edge(v_col, True),
                  pl.BlockSpec(bias.shape, lambda b, i: (0, 0, 0))],
        out_specs=pl.BlockSpec((1, ATT_TQ, ATT_WIDTH), lambda b, i: (b, i, 0)),
        out_shape=jax.ShapeDtypeStruct((B, S, ATT_WIDTH), jnp.bfloat16),
        compiler_params=_col_params(),
        name="window_attn",
    )(sink.astype(jnp.float32), qkv, qkv, qkv, qkv, qkv, qkv, qkv, bias)


MEM_TQ = 512


def _mem_attn_kernel(q_ref, kv_ref, o_ref):
    bf = jnp.bfloat16
    q = q_ref[0].astype(bf)
    kv = kv_ref[0].astype(bf)
    scale = MEM_HEAD_DIM ** -0.5
    for h in range(MEM_HEADS):
        sl = slice(h * MEM_HEAD_DIM, (h + 1) * MEM_HEAD_DIM)
        s = lax.dot_general(q[:, sl], kv[:, sl], _NT_DIMS, preferred_element_type=jnp.float32) * scale
        p = jnp.exp(s - jnp.max(s, axis=-1, keepdims=True))
        p = (p / jnp.sum(p, axis=-1, keepdims=True)).astype(bf)
        vh = kv[:, MEM_WIDTH + h * MEM_HEAD_DIM:MEM_WIDTH + (h + 1) * MEM_HEAD_DIM]
        o_ref[0, :, sl] = jnp.dot(p, vh, preferred_element_type=jnp.float32).astype(o_ref.dtype)


def memory_attention_pallas(q, kv):
    B, S, _ = q.shape
    M = kv.shape[1]
    return pl.pallas_call(
        _mem_attn_kernel,
        grid=(B, S // MEM_TQ),
        in_specs=[pl.BlockSpec((1, MEM_TQ, MEM_WIDTH), lambda b, i: (b, i, 0)),
                  pl.BlockSpec((1, M, 2 * MEM_WIDTH), lambda b, i: (b, 0, 0))],
        out_specs=pl.BlockSpec((1, MEM_TQ, MEM_WIDTH), lambda b, i: (b, i, 0)),
        out_shape=jax.ShapeDtypeStruct((B, S, MEM_WIDTH), jnp.bfloat16),
        compiler_params=_col_params(),
        name="mem_attn",
    )(q, kv)


MERGE_TM = 256


def _layer_norm_rows(y, g, b):
    mu = jnp.mean(y, axis=-1, keepdims=True)
    d = y - mu
    var = jnp.mean(d * d, axis=-1, keepdims=True)
    return d * lax.rsqrt(var + LN_EPS) * g + b


def _merge_kernel(a_ref, h_ref, m_ref, g_ref, x_ref, wb_ref, wo_ref, lg_ref, lb_ref, o_ref):
    bf = jnp.bfloat16
    f32 = jnp.float32
    merged = g_ref[:, 0:D_MODEL] * jnp.dot(a_ref[...].astype(bf), wb_ref[0], preferred_element_type=f32)
    merged = merged + g_ref[:, D_MODEL:2 * D_MODEL] * jnp.dot(h_ref[...].astype(bf), wb_ref[1], preferred_element_type=f32)
    merged = merged + g_ref[:, 2 * D_MODEL:] * jnp.dot(m_ref[...].astype(bf), wb_ref[2], preferred_element_type=f32)
    y = ALPHA * x_ref[...] + jnp.dot(merged.astype(bf), wo_ref[...], preferred_element_type=f32)
    o_ref[...] = _layer_norm_rows(y, lg_ref[...], lb_ref[...])


def merge_norm(att, hy, mem, gates, x, w_branch, w_out, ln_g, ln_b):
    T = x.shape[0]
    bf = jnp.bfloat16
    rows = lambda w: pl.BlockSpec((MERGE_TM, w), lambda i: (i, 0))
    whole = lambda a: pl.BlockSpec(a.shape, lambda i: (0,) * a.ndim)
    wb, wo = w_branch.astype(bf), w_out.astype(bf)
    lg, lb = ln_g.reshape(1, D_MODEL), ln_b.reshape(1, D_MODEL)
    return pl.pallas_call(
        _merge_kernel,
        grid=(T // MERGE_TM,),
        in_specs=[rows(BRANCH_WIDTH), rows(BRANCH_WIDTH), rows(BRANCH_WIDTH), rows(N_BRANCH * D_MODEL), rows(D_MODEL),
                  whole(wb), whole(wo), whole(lg), whole(lb)],
        out_specs=rows(D_MODEL),
        out_shape=jax.ShapeDtypeStruct((T, D_MODEL), jnp.float32),
        compiler_params=pltpu.CompilerParams(dimension_semantics=("arbitrary",), vmem_limit_bytes=VMEM_LIMIT_BYTES),
        name="merge_norm",
    )(att, hy, mem, gates, x, wb, wo, lg, lb)


def _residual_norm_kernel(x_ref, r_ref, lg_ref, lb_ref, o_ref):
    o_ref[...] = _layer_norm_rows(ALPHA * x_ref[...] + r_ref[...], lg_ref[...], lb_ref[...])


def residual_norm(x, r, ln_g, ln_b):
    T = x.shape[0]
    tm = 512
    rows = pl.BlockSpec((tm, D_MODEL), lambda i: (i, 0))
    one = pl.BlockSpec((1, D_MODEL), lambda i: (0, 0))
    return pl.pallas_call(
        _residual_norm_kernel,
        grid=(T // tm,),
        in_specs=[rows, rows, one, one],
        out_specs=rows,
        out_shape=jax.ShapeDtypeStruct((T, D_MODEL), jnp.float32),
        compiler_params=pltpu.CompilerParams(dimension_semantics=("arbitrary",), vmem_limit_bytes=VMEM_LIMIT_BYTES),
        name="residual_norm",
    )(x, r, ln_g.reshape(1, D_MODEL), ln_b.reshape(1, D_MODEL))


FFT_N1 = 256
FFT_N2 = 128
FFT_N = FFT_N1 * FFT_N2
HY_COLS = FFT_N2 * HYENA_WIDTH
LMUL_TN = 2048
KB_K1 = 8
FILT_TM = 512


def _dft_constants():
    f32 = jnp.float32
    n1 = jnp.arange(FFT_N1, dtype=jnp.int32)
    ang1 = ((n1[:, None] * n1[None, :]) % FFT_N1).astype(f32) * f32(2.0 * math.pi / FFT_N1)
    fr, fi = jnp.cos(ang1), -jnp.sin(ang1)
    h = FFT_N1 // 2
    f_fwd = jnp.concatenate([jnp.concatenate([fr[:, :h], -fi[:, :h]], axis=1),
                             jnp.concatenate([fi[:, :h], fr[:, :h]], axis=1)], axis=0)
    f_inv = f_fwd.T * f32(1.0 / FFT_N)
    f_real = jnp.concatenate([fr, fi], axis=0)
    k2 = jnp.arange(FFT_N2, dtype=jnp.int32)
    k = n1[:, None, None] + FFT_N1 * k2[None, :, None]
    ang2 = ((k * k2[None, None, :]) % FFT_N).astype(f32) * f32(2.0 * math.pi / FFT_N)
    gr, gi = jnp.cos(ang2), -jnp.sin(ang2)
    g = jnp.concatenate([jnp.concatenate([gr, -gi], axis=2), jnp.concatenate([gi, gr], axis=2)], axis=1)
    bf = jnp.bfloat16
    return f_fwd.astype(bf), f_inv.astype(bf), f_real.astype(bf), g.astype(bf), g.transpose(0, 2, 1).astype(bf)


def _lmul_pair_kernel(l_ref, xr_ref, xi_ref, o_ref):
    x = jnp.concatenate([xr_ref[0], xi_ref[0]], axis=0).astype(jnp.bfloat16)
    o_ref[0] = jnp.dot(l_ref[...], x, preferred_element_type=jnp.float32).astype(o_ref.dtype)


def _lmul_kernel(l_ref, x_ref, o_ref):
    o_ref[0] = jnp.dot(l_ref[...], x_ref[0].astype(jnp.bfloat16),
                       preferred_element_type=jnp.float32).astype(o_ref.dtype)


def _lmul_gate_kernel(l_ref, a_ref, zr_ref, zi_ref, pr_ref, pi_ref, b_ref, o_ref):
    y = jnp.dot(l_ref[...], a_ref[0], preferred_element_type=jnp.float32)
    h = FFT_N1 // 2
    o_ref[0, :h] = pr_ref[0] * (y[:h] + zr_ref[0] * b_ref[...])
    o_ref[0, h:] = pi_ref[0] * (y[h:] + zi_ref[0] * b_ref[...])


def dft_stage1_pairs(f_fwd, z):
    nb, h, cols = z.shape
    npair = (nb + 1) // 2
    last = nb - 1
    return pl.pallas_call(
        _lmul_pair_kernel,
        grid=(npair, cols // LMUL_TN),
        in_specs=[
            pl.BlockSpec(f_fwd.shape, lambda p, j: (0, 0)),
            pl.BlockSpec((1, h, LMUL_TN), lambda p, j: (jnp.minimum(2 * p, last), 0, j)),
            pl.BlockSpec((1, h, LMUL_TN), lambda p, j: (jnp.minimum(2 * p + 1, last), 0, j)),
        ],
        out_specs=pl.BlockSpec((1, 2 * FFT_N1, LMUL_TN), lambda p, j: (p, 0, j)),
        out_shape=jax.ShapeDtypeStruct((npair, 2 * FFT_N1, cols), jnp.bfloat16),
        compiler_params=_col_params(),
        name="dft_stage1",
    )(f_fwd, z, z)


def dft_stage1_real(f_real, x):
    _, h, cols = x.shape
    return pl.pallas_call(
        _lmul_kernel,
        grid=(1, cols // LMUL_TN),
        in_specs=[pl.BlockSpec(f_real.shape, lambda p, j: (0, 0)),
                  pl.BlockSpec((1, h, LMUL_TN), lambda p, j: (p, 0, j))],
        out_specs=pl.BlockSpec((1, 2 * FFT_N1, LMUL_TN), lambda p, j: (p, 0, j)),
        out_shape=jax.ShapeDtypeStruct((1, 2 * FFT_N1, cols), jnp.bfloat16),
        compiler_params=_col_params(),
        name="dft_stage1_real",
    )(f_real, x)


def idft_stage1_gate(f_inv, a, z, p, bias_row):
    npair, _, cols = a.shape
    half = FFT_N1 // 2

    def half_blk(x, r):
        last = x.shape[0] - 1
        return pl.BlockSpec((1, half, LMUL_TN), lambda q, j: (jnp.minimum(2 * q + r, last), 0, j))

    out = pl.pallas_call(
        _lmul_gate_kernel,
        grid=(npair, cols // LMUL_TN),
        in_specs=[pl.BlockSpec(f_inv.shape, lambda q, j: (0, 0)),
                  pl.BlockSpec((1, 2 * FFT_N1, LMUL_TN), lambda q, j: (q, 0, j)),
                  half_blk(z, 0), half_blk(z, 1), half_blk(p, 0), half_blk(p, 1),
                  pl.BlockSpec((1, LMUL_TN), lambda q, j: (0, 0))],
        out_specs=pl.BlockSpec((1, FFT_N1, LMUL_TN), lambda q, j: (q, 0, j)),
        out_shape=jax.ShapeDtypeStruct((npair, FFT_N1, cols), jnp.float32),
        compiler_params=_col_params(),
        name="idft_stage1_gate",
    )(f_inv, a, z, z, p, p, bias_row)
    return out.reshape(2 * npair, half, cols)


def _stage2_conv_kernel(a_ref, g_ref, gt_ref, h_ref, o_ref):
    for j in range(KB_K1):
        x = jnp.concatenate([a_ref[0, 0, j], a_ref[0, 1, j]], axis=0)
        z = jnp.dot(g_ref[j], x, preferred_element_type=jnp.float32)
        zr, zi = z[:FFT_N2], z[FFT_N2:]
        hr, hi = h_ref[0, j], h_ref[1, j]
        w = jnp.concatenate([zr * hr - zi * hi, zr * hi + zi * hr], axis=0).astype(jnp.bfloat16)
        y = jnp.dot(gt_ref[j], w, preferred_element_type=jnp.float32)
        o_ref[0, 0, j] = y[:FFT_N2].astype(o_ref.dtype)
        o_ref[0, 1, j] = y[FFT_N2:].astype(o_ref.dtype)


def stage2_conv(a, g, gt, hf, order):
    npair = a.shape[0]
    C = HYENA_WIDTH
    ablk = pl.BlockSpec((1, 2, KB_K1, FFT_N2, C), lambda p, i: (p, 0, i, 0, 0))
    gblk = pl.BlockSpec((KB_K1, 2 * FFT_N2, 2 * FFT_N2), lambda p, i: (i, 0, 0))
    return pl.pallas_call(
        _stage2_conv_kernel,
        grid=(npair, FFT_N1 // KB_K1),
        in_specs=[ablk, gblk, gblk,
                  pl.BlockSpec((2, KB_K1, FFT_N2, C), lambda p, i: (0, i, 0, order))],
        out_specs=ablk,
        out_shape=jax.ShapeDtypeStruct(a.shape, a.dtype),
        compiler_params=_col_params(),
        name="stage2_conv",
    )(a, g, gt, hf)


def _stage2_filter_kernel(a_ref, g_ref, s_ref, o_ref):
    for j in range(KB_K1):
        x = jnp.concatenate([a_ref[0, j], a_ref[1, j]], axis=0)
        z = jnp.dot(g_ref[j], x, preferred_element_type=jnp.float32) * s_ref[...]
        o_ref[0, j] = z[:FFT_N2]
        o_ref[1, j] = z[FFT_N2:]


def stage2_filter(a, g, inv_norm):
    W = a.shape[-1]
    blk = pl.BlockSpec((2, KB_K1, FFT_N2, W), lambda i: (0, i, 0, 0))
    return pl.pallas_call(
        _stage2_filter_kernel,
        grid=(FFT_N1 // KB_K1,),
        in_specs=[blk, pl.BlockSpec((KB_K1, 2 * FFT_N2, 2 * FFT_N2), lambda i: (i, 0, 0)),
                  pl.BlockSpec((1, W), lambda i: (0, 0))],
        out_specs=blk,
        out_shape=jax.ShapeDtypeStruct(a.shape, jnp.float32),
        compiler_params=pltpu.CompilerParams(dimension_semantics=("arbitrary",), vmem_limit_bytes=VMEM_LIMIT_BYTES),
        name="stage2_filter",
    )(a, g, inv_norm)


def _filter_gen_kernel(bands_ref, w0_ref, wc_ref, ws_ref, wmid_ref, fb_ref, ff_ref, wdir_ref, wbwd_ref,
                       adelta_ref, two_ref, norm_ref, *, seq):
    i = pl.program_id(0)
    bf = jnp.bfloat16
    row = i * FILT_TM + lax.broadcasted_iota(jnp.int32, (FILT_TM, 1), 0)
    pos_i = jnp.where(row < seq, row, 2 * seq - row)
    pos = pos_i.astype(jnp.float32)
    t = pos / max(seq - 1, 1)
    w = (jnp.float32(2.0 * math.pi) * pos) / seq
    ang = w * bands_ref[...]
    pre = (jnp.dot(jnp.cos(ang).astype(bf), wc_ref[...], preferred_element_type=jnp.float32)
           + jnp.dot((-jnp.sin(ang)).astype(bf), ws_ref[...], preferred_element_type=jnp.float32)
           + t.astype(bf).astype(jnp.float32) * w0_ref[...])
    h = jnp.sin(ff_ref[0:1, :] * (pre + fb_ref[0:1, :]))
    for m in range(FILTER_INNER):
        pre = jnp.dot(h.astype(bf), wmid_ref[m], preferred_element_type=jnp.float32)
        h = jnp.sin(ff_ref[m + 1:m + 2, :] * (pre + fb_ref[m + 1:m + 2, :]))
    hb = h.astype(bf)
    decay = jnp.exp(-t * adelta_ref[...])
    out = jnp.dot(hb, wdir_ref[0], preferred_element_type=jnp.float32) * decay
    out = jnp.where(pos_i < seq, out, 0.0)

    @pl.when(i == 0)
    def _():
        extra = jnp.dot(hb, wbwd_ref[...], preferred_element_type=jnp.float32) * decay
        first = jnp.where(row == 0, out + extra, out)
        two_ref[...] = first
        norm_ref[...] = jnp.sum(jnp.abs(first), axis=0, keepdims=True)

    @pl.when(i != 0)
    def _():
        two_ref[...] = out
        norm_ref[...] += jnp.sum(jnp.abs(out), axis=0, keepdims=True)


def hyena_filter_rows(seq, f_w_in, f_w_mid, f_b, f_freq, f_w_out):
    f32, bf = jnp.float32, jnp.bfloat16
    C, H = HYENA_WIDTH, FILTER_HIDDEN
    bands = jnp.linspace(1e-4, FILTER_BANDS - 1, FILTER_BANDS, dtype=f32)
    bands = jnp.pad(bands, (0, VREG_LANES - FILTER_BANDS)).reshape(1, VREG_LANES)
    w_in = f_w_in.astype(bf)
    w0 = w_in[0:1].astype(f32)
    pad_rows = ((0, VREG_LANES - FILTER_BANDS), (0, 0))
    wc = jnp.pad(w_in[1:1 + FILTER_BANDS], pad_rows)
    ws = jnp.pad(w_in[1 + FILTER_BANDS:], pad_rows)
    w_out = f_w_out.astype(bf).reshape(H, HYENA_ORDER, 2, C)
    wdir = jnp.stack([w_out[:, :, 0].reshape(H, HYENA_ORDER * C), w_out[:, :, 1].reshape(H, HYENA_ORDER * C)])
    max_decay = math.log(DECAY_TARGET) / FAST_DECAY_PCT
    min_decay = math.log(DECAY_TARGET) / SLOW_DECAY_PCT
    adelta = jnp.abs(jnp.linspace(min_decay, max_decay, C, dtype=f32))
    adelta = jnp.tile(adelta, HYENA_ORDER).reshape(1, HYENA_ORDER * C)
    n_tiles = 2 * seq // FILT_TM
    half_tiles = seq // FILT_TM
    whole = lambda a: pl.BlockSpec(a.shape, lambda i: (0,) * a.ndim)
    wmid = f_w_mid.astype(bf)
    fb, ff = f_b.astype(f32), f_freq.astype(f32)
    return pl.pallas_call(
        functools.partial(_filter_gen_kernel, seq=seq),
        grid=(n_tiles,),
        in_specs=[whole(bands), whole(w0), whole(wc), whole(ws), whole(wmid), whole(fb), whole(ff),
                  pl.BlockSpec((1, H, HYENA_ORDER * C), lambda i: (i // half_tiles, 0, 0)),
                  pl.BlockSpec((None, H, HYENA_ORDER * C), lambda i: (1, 0, 0)),
                  whole(adelta)],
        out_specs=[pl.BlockSpec((FILT_TM, HYENA_ORDER * C), lambda i: (i, 0)),
                   pl.BlockSpec((1, HYENA_ORDER * C), lambda i: (0, 0))],
        out_shape=[jax.ShapeDtypeStruct((2 * seq, HYENA_ORDER * C), f32),
                   jax.ShapeDtypeStruct((1, HYENA_ORDER * C), f32)],
        compiler_params=pltpu.CompilerParams(dimension_semantics=("arbitrary",), vmem_limit_bytes=VMEM_LIMIT_BYTES),
        name="hyena_filter_gen",
    )(bands, w0, wc, ws, wmid, fb, ff, wdir, wdir, adelta)


def _short_conv_kernel(x_ref, w_ref, b_ref, o_ref):
    x = x_ref[0]
    n = x.shape[0]
    t = lax.broadcasted_iota(jnp.int32, x.shape, 0)
    prev = jnp.where(t == 0, 0.0, pltpu.roll(x, 1, 0))
    nxt = jnp.where(t == n - 1, 0.0, pltpu.roll(x, n - 1, 0))
    o_ref[0] = ((b_ref[...] + prev * w_ref[0:1, :]) + x * w_ref[1:2, :]) + nxt * w_ref[2:3, :]


def short_conv_pallas(u, w, b):
    B, L, W = u.shape
    tc = VREG_LANES
    return pl.pallas_call(
        _short_conv_kernel,
        grid=(B, W // tc),
        in_specs=[pl.BlockSpec((1, L, tc), lambda i, j: (i, 0, j)),
                  pl.BlockSpec((SHORT_CONV, tc), lambda i, j: (0, j)),
                  pl.BlockSpec((1, tc), lambda i, j: (0, j))],
        out_specs=pl.BlockSpec((1, L, tc), lambda i, j: (i, 0, j)),
        out_shape=jax.ShapeDtypeStruct((B, L, W), jnp.float32),
        compiler_params=_col_params(),
        name="short_conv",
    )(u, w, b.reshape(1, W))


def hyena_mixer_pallas(u, conv_w, conv_b, f_w_in, f_w_mid, f_b, f_freq, f_w_out, hyena_bias):
    B, L, _ = u.shape
    assert 2 * L == FFT_N
    C = HYENA_WIDTH
    f_fwd, f_inv, f_real, g, gt = _dft_constants()
    two, norm = hyena_filter_rows(L, f_w_in, f_w_mid, f_b, f_freq, f_w_out)
    af = dft_stage1_real(f_real, two.reshape(1, FFT_N1, FFT_N2 * HYENA_ORDER * C))
    af = af.reshape(2, FFT_N1, FFT_N2, HYENA_ORDER * C)
    hf = stage2_filter(af, g, 1.0 / norm)
    uc = short_conv_pallas(u, conv_w, conv_b)
    half = FFT_N1 // 2
    parts = [uc[..., o * C:(o + 1) * C].reshape(B, half, HY_COLS) for o in range(HYENA_ORDER + 1)]
    z = parts[0]
    for o in range(HYENA_ORDER):
        a = dft_stage1_pairs(f_fwd, z)
        npair = a.shape[0]
        a = stage2_conv(a.reshape(npair, 2, FFT_N1, FFT_N2, C), g, gt, hf, o)
        bias_row = jnp.tile(hyena_bias[o], LMUL_TN // C).reshape(1, LMUL_TN)
        z = idft_stage1_gate(f_inv, a.reshape(npair, 2 * FFT_N1, HY_COLS), z, parts[o + 1], bias_row)
    return z[:B].reshape(B, L, C)


PEER_SEL = PEER_HEADS * PEER_TOPK
PEER_TB = 128
VREG_SUBLANES = 8
VREG_LANES = 128
EXPERT_ROWS = 4
TILE_ROWS = 16
V_UNROLL = 16
IDX_SPLIT = 16
IDX_PER = PEER_SEL // IDX_SPLIT


def _row_reader(idx_refs, t):
    return lambda k: idx_refs[k // IDX_PER][0, k % IDX_PER, t]


def pack_expert_table(tab):
    e = tab.shape[0]
    b = lax.bitcast_convert_type(tab.astype(jnp.bfloat16), jnp.uint16).astype(jnp.uint32)
    b = b.reshape(e, 2, EXPERT_ROWS, VREG_LANES)
    w = (b[:, 0] | (b[:, 1] << 16)).reshape(e * EXPERT_ROWS, VREG_LANES)
    return jnp.pad(w, ((0, VREG_SUBLANES), (0, 0)))


def from_tile_rows(y):
    T = y.shape[0]
    return y.reshape(T, EXPERT_ROWS, 2, VREG_LANES).transpose(0, 2, 1, 3).reshape(T, D_MODEL)


def _split_bf16(v):
    hi = v.astype(jnp.bfloat16)
    return hi, (v - hi.astype(jnp.float32)).astype(jnp.bfloat16)


def _gelu_exact(x):
    return 0.5 * x * (1.0 + lax.erf(x * (2.0 ** -0.5)))


_COMBINE_POS = (3, 7, 1, 5, 2, 6, 0, 4)


def _sublane_sums(c, sub):
    mv = (sub & 2) != 0
    e = []
    for c1, c2 in ((c[0], c[1]), (c[2], c[3])):
        e.append(jnp.where(mv, c1 + pltpu.roll(c1, 2, 0), c2 + pltpu.roll(c2, 6, 0)))
    mo = (sub & 1) != 0
    return jnp.where(mo, e[0] + pltpu.roll(e[0], 1, 0), e[1] + pltpu.roll(e[1], 7, 0))


def _peer_u_kernel(*refs):
    idx_refs = refs[:IDX_SPLIT]
    x_ref, g_ref, tab_ref, o_ref, act_ref, r_ref = refs[IDX_SPLIT:]
    sub = lax.broadcasted_iota(jnp.int32, (VREG_SUBLANES, VREG_LANES), 0)
    lo4 = sub < EXPERT_ROWS

    def token(t, carry):
        sel_row = _row_reader(idx_refs, t)
        x8 = x_ref[t]
        xr = pltpu.roll(x8, EXPERT_ROWS, 0)
        x_lo = jnp.where(lo4, x8, xr)
        x_hi = jnp.where(lo4, xr, x8)
        rs = []
        for grp in range(PEER_SEL // VREG_SUBLANES):
            pairs = []
            for i in range(VREG_SUBLANES // 2):
                wa = tab_ref[pl.ds(sel_row(grp * VREG_SUBLANES + _COMBINE_POS[2 * i]), VREG_SUBLANES), :]
                wb = tab_ref[pl.ds(sel_row(grp * VREG_SUBLANES + _COMBINE_POS[2 * i + 1]), VREG_SUBLANES), :]
                w = jnp.where(lo4, wa, pltpu.roll(wb, EXPERT_ROWS, 0))
                lo = lax.bitcast_convert_type(w << 16, jnp.float32)
                hi = lax.bitcast_convert_type(w & jnp.uint32(0xFFFF0000), jnp.float32)
                pairs.append(lo * x_lo + hi * x_hi)
            rs.append(_sublane_sums(pairs, sub))
        r0 = pl.multiple_of(t * PEER_SEL, PEER_SEL)
        r_ref[pl.ds(r0, PEER_SEL), :] = jnp.concatenate(rs, axis=0).astype(jnp.bfloat16)
        return carry

    lax.fori_loop(0, PEER_TB, token, 0)
    ones = jnp.ones((VREG_LANES, VREG_LANES), jnp.bfloat16)
    keep = (lax.broadcasted_iota(jnp.int32, (PEER_SEL, VREG_LANES), 0)
            == lax.broadcasted_iota(jnp.int32, (PEER_SEL, VREG_LANES), 1))
    chunk = VREG_SUBLANES * PEER_SEL
    for c in range(PEER_TB // VREG_SUBLANES):
        s = jnp.dot(r_ref[c * chunk:(c + 1) * chunk, :], ones, preferred_element_type=jnp.float32)
        for j in range(VREG_SUBLANES):
            blk = jnp.where(keep, s[j * PEER_SEL:(j + 1) * PEER_SEL], 0.0)
            act_ref[c * VREG_SUBLANES + j:c * VREG_SUBLANES + j + 1, :] = jnp.sum(blk, axis=0, keepdims=True)
    o_ref[...] = g_ref[...] * _gelu_exact(act_ref[...])


def _peer_v_kernel(*refs):
    idx_refs = refs[:IDX_SPLIT]
    coef_ref, e_ref, d_ref, tab_ref, o_ref, chi_ref, clo_ref = refs[IDX_SPLIT:]
    c_hi, c_lo = _split_bf16(coef_ref[...])
    chi_ref[...] = jnp.dot(c_hi, e_ref[...], preferred_element_type=jnp.float32)
    clo_ref[...] = jnp.dot(c_lo, e_ref[...], preferred_element_type=jnp.float32)

    def token(t, carry):
        sel_row = _row_reader(idx_refs, t)
        cmat = jnp.concatenate([chi_ref[pl.ds(t, 1), :] * d_ref[...], clo_ref[pl.ds(t, 1), :] * d_ref[...]],
                               axis=0).astype(jnp.bfloat16)
        w = jnp.concatenate([pltpu.bitcast(tab_ref[pl.ds(sel_row(k), VREG_SUBLANES), :], jnp.bfloat16)
                             for k in range(PEER_SEL)], axis=0)
        acc = jnp.dot(cmat, w, preferred_element_type=jnp.float32)
        o_ref[t] = acc[:VREG_SUBLANES] + acc[VREG_SUBLANES:]
        return carry

    lax.fori_loop(0, PEER_TB, token, 0, unroll=V_UNROLL)


def peer_experts(x, eidx, g, tab_u, tab_v):
    T = x.shape[0]
    grid = (T // PEER_TB,)
    f32, bf = jnp.float32, jnp.bfloat16
    sel = jnp.arange(PEER_SEL, dtype=jnp.int32)
    col = jnp.arange(PEER_SEL * TILE_ROWS, dtype=jnp.int32)
    expand = (col[None, :] // TILE_ROWS == sel[:, None]).astype(bf)
    diag = (col[None, :] % TILE_ROWS == jnp.arange(VREG_SUBLANES, dtype=jnp.int32)[:, None]).astype(f32)
    assert eidx.shape == (T // PEER_TB, PEER_SEL, PEER_TB)
    idx_specs = [pl.BlockSpec((1, IDX_PER, PEER_TB), lambda i, p=p: (i, p, 0), memory_space=pltpu.SMEM)
                 for p in range(IDX_SPLIT)]
    idx_parts = [eidx] * IDX_SPLIT
    vec_blk = pl.BlockSpec((PEER_TB, PEER_SEL), lambda i: (i, 0))
    row_blk = pl.BlockSpec((PEER_TB, VREG_SUBLANES, VREG_LANES), lambda i: (i, 0, 0))
    whole = lambda a: pl.BlockSpec(a.shape, lambda i: (0,) * a.ndim)
    tab_spec = pl.BlockSpec(memory_space=pltpu.VMEM)
    params = pltpu.CompilerParams(dimension_semantics=("arbitrary",), vmem_limit_bytes=VMEM_LIMIT_BYTES)
    coef = pl.pallas_call(
        _peer_u_kernel,
        grid=grid,
        in_specs=idx_specs + [row_blk, vec_blk, tab_spec],
        out_specs=vec_blk,
        out_shape=jax.ShapeDtypeStruct((T, PEER_SEL), f32),
        scratch_shapes=[pltpu.VMEM((PEER_TB, PEER_SEL), f32),
                        pltpu.VMEM((PEER_TB * PEER_SEL, VREG_LANES), bf)],
        compiler_params=params,
        name="peer_u",
    )(*idx_parts, x.reshape(T, VREG_SUBLANES, VREG_LANES), g, tab_u)
    out = pl.pallas_call(
        _peer_v_kernel,
        grid=grid,
        in_specs=idx_specs + [vec_blk, whole(expand), whole(diag), tab_spec],
        out_specs=row_blk,
        out_shape=jax.ShapeDtypeStruct((T, VREG_SUBLANES, VREG_LANES), f32),
        scratch_shapes=[pltpu.VMEM((PEER_TB, PEER_SEL * TILE_ROWS), f32),
                        pltpu.VMEM((PEER_TB, PEER_SEL * TILE_ROWS), f32)],
        compiler_params=params,
        name="peer_v",
    )(*idx_parts, coef, expand, diag, tab_v)
    return from_tile_rows(out)


ROUTE_TM = 512


def _top16_rows(s, key_id):
    row16 = lax.broadcasted_iota(jnp.int32, (PEER_TOPK, VREG_LANES), 0)
    vals = jnp.zeros((PEER_TOPK, VREG_LANES), jnp.float32)
    ids = jnp.zeros((PEER_TOPK, VREG_LANES), jnp.float32)
    big = jnp.float32(2 ** 30)
    for j in range(PEER_TOPK):
        m = jnp.max(s, axis=0, keepdims=True)
        am = jnp.min(jnp.where(s == m, key_id, big), axis=0, keepdims=True)
        vals = jnp.where(row16 == j, m, vals)
        ids = jnp.where(row16 == j, am, ids)
        s = jnp.where(key_id == am, -jnp.inf, s)
    return vals, ids


_PAIR_GROUPS = ((0, 0, 8), (0, 8, 8), (1, 0, 8), (2, 0, 5), (3, 0, 4), (4, 0, 3), (5, 0, 2), (6, 0, 2), (7, 0, 2))


def _route_head(s0, i0, s1, i1):
    sub_i = lax.broadcasted_iota(jnp.int32, (VREG_SUBLANES, VREG_LANES), 0)
    sub = sub_i.astype(jnp.float32)
    cands, flat, eids = [], [], []
    for a, b0, nb in _PAIR_GROUPS:
        c = s0[a:a + 1] + s1[b0:b0 + VREG_SUBLANES]
        cands.append(jnp.where(sub_i < nb, c, -jnp.inf) if nb < VREG_SUBLANES else c)
        flat.append(a * PEER_TOPK + b0 + sub)
        eids.append(i0[a:a + 1] * N_KEYS + i1[b0:b0 + VREG_SUBLANES])
    cands.append(s0[VREG_SUBLANES:] + s1[0:1])
    flat.append((sub + VREG_SUBLANES) * PEER_TOPK)
    eids.append(i0[VREG_SUBLANES:] * N_KEYS + i1[0:1])
    cand = jnp.concatenate(cands, axis=0)
    flat = jnp.concatenate(flat, axis=0)
    eid = jnp.concatenate(eids, axis=0)
    row16 = lax.broadcasted_iota(jnp.int32, (PEER_TOPK, VREG_LANES), 0)
    sc = jnp.zeros((PEER_TOPK, VREG_LANES), jnp.float32)
    sel = jnp.zeros((PEER_TOPK, VREG_LANES), jnp.float32)
    big = jnp.float32(2 ** 30)
    for j in range(PEER_TOPK):
        m = jnp.max(cand, axis=0, keepdims=True)
        am = jnp.min(jnp.where(cand == m, flat, big), axis=0, keepdims=True)
        hit = flat == am
        e = jnp.max(jnp.where(hit, eid, -1.0), axis=0, keepdims=True)
        sc = jnp.where(row16 == j, m, sc)
        sel = jnp.where(row16 == j, e, sel)
        cand = jnp.where(hit, -jnp.inf, cand)
    p = jnp.exp(sc - sc[0:1])
    return sel, p / jnp.sum(p, axis=0, keepdims=True)


def _route_kernel(x_ref, wq_ref, sk_ref, rows_ref, g_ref, q_ref):
    q = jnp.dot(x_ref[...].astype(jnp.bfloat16), wq_ref[...], preferred_element_type=jnp.float32)
    q_ref[...] = q.astype(jnp.bfloat16)
    key_id = lax.broadcasted_iota(jnp.int32, (N_KEYS, VREG_LANES), 0).astype(jnp.float32)

    def head(h, carry):
        tops = []
        for c in range(2):
            hc = h * 2 + c
            qhc = q_ref[:, pl.ds(pl.multiple_of(hc * PEER_HALF, PEER_HALF), PEER_HALF)]
            s = lax.dot_general(sk_ref[hc], qhc, _NT_DIMS, preferred_element_type=jnp.float32)
            tops.append([_top16_rows(s[:, j * VREG_LANES:(j + 1) * VREG_LANES], key_id)
                         for j in range(ROUTE_TM // VREG_LANES)])
        r0 = pl.multiple_of(h * PEER_TOPK, PEER_TOPK)
        for j in range(ROUTE_TM // VREG_LANES):
            (s0, i0), (s1, i1) = tops[0][j], tops[1][j]
            sel, g = _route_head(s0, i0, s1, i1)
            rows_ref[j, pl.ds(r0, PEER_TOPK), :] = (sel * EXPERT_ROWS).astype(jnp.int32)
            g_ref[pl.ds(r0, PEER_TOPK), j * VREG_LANES:(j + 1) * VREG_LANES] = g
        return carry

    lax.fori_loop(0, PEER_HEADS, head, 0)


def peer_route(x, w_query, sub_keys):
    T = x.shape[0]
    assert PEER_TB == VREG_LANES
    wq = w_query.astype(jnp.bfloat16)
    sk = sub_keys.reshape(PEER_HEADS * 2, N_KEYS, PEER_HALF).astype(jnp.bfloat16)
    out_blk = pl.BlockSpec((PEER_SEL, ROUTE_TM), lambda i: (0, i))
    rows, g = pl.pallas_call(
        _route_kernel,
        grid=(T // ROUTE_TM,),
        in_specs=[
            pl.BlockSpec((ROUTE_TM, D_MODEL), lambda i: (i, 0)),
            pl.BlockSpec(wq.shape, lambda i: (0, 0)),
            pl.BlockSpec(sk.shape, lambda i: (0, 0, 0)),
        ],
        out_specs=[pl.BlockSpec((ROUTE_TM // PEER_TB, PEER_SEL, PEER_TB), lambda i: (i, 0, 0)), out_blk],
        out_shape=[jax.ShapeDtypeStruct((T // PEER_TB, PEER_SEL, PEER_TB), jnp.int32),
                   jax.ShapeDtypeStruct((PEER_SEL, T), jnp.float32)],
        scratch_shapes=[pltpu.VMEM((ROUTE_TM, PEER_HEADS * PEER_QDIM), jnp.bfloat16)],
        compiler_params=pltpu.CompilerParams(dimension_semantics=("arbitrary",),
                                             vmem_limit_bytes=VMEM_LIMIT_BYTES),
        name="peer_route",
    )(x, wq, sk)
    return rows, g.T


def encoder_layer(x, mem, rel_bias, w_in, b_in, conv_w, conv_b, attn_sink, f_w_in, f_w_mid, f_b, f_freq,
                  f_w_out, hyena_bias, w_mem_kv, w_branch, w_out, ln1_g, ln1_b, w_query, sub_keys,
                  expert_u, expert_v, ln2_g, ln2_b):
    B, S, _ = x.shape
    T = B * S
    xf = x.reshape(T, D_MODEL)
    proj = lambda lo, hi, **kw: linear(xf, w_in[:, lo:hi], b_in[lo:hi], **kw)
    qkv = proj(0, O_HY).reshape(B, S, O_HY)
    hy = proj(O_HY, O_MQ).reshape(B, S, O_MQ - O_HY)
    q_m = proj(O_MQ, O_GATE).reshape(B, S, MEM_WIDTH)
    gates = proj(O_GATE, IN_WIDTH, tn=(IN_WIDTH - O_GATE) // 2, sigmoid=True)
    M = mem.shape[1]
    kv = linear(mem.reshape(B * M, D_MODEL), w_mem_kv, jnp.zeros((2 * MEM_WIDTH,), jnp.float32), tm=B * M)
    att = window_attention_pallas(qkv, window_bias_table(rel_bias), attn_sink)
    hyo = hyena_mixer_pallas(hy, conv_w, conv_b, f_w_in, f_w_mid, f_b, f_freq, f_w_out, hyena_bias)
    mat = memory_attention_pallas(q_m, kv.reshape(B, M, 2 * MEM_WIDTH))
    x1 = merge_norm(att.reshape(T, ATT_WIDTH), hyo.reshape(T, HYENA_WIDTH), mat.reshape(T, MEM_WIDTH), gates, xf,
                    w_branch, w_out, ln1_g, ln1_b)
    rows, g = peer_route(x1, w_query, sub_keys)
    peer = peer_experts(x1, rows, g, pack_expert_table(expert_u), pack_expert_table(expert_v))
    return residual_norm(x1, peer, ln2_g, ln2_b).reshape(B, S, D_MODEL)


def kernel(x_prompt, x_sample, mem_prompt, mem_sample, rel_bias, w_in, b_in, conv_w, conv_b, attn_sink,
           f_w_in, f_w_mid, f_b, f_freq, f_w_out, hyena_bias, w_mem_kv, w_branch, w_out, ln1_g, ln1_b,
           w_query, sub_keys, expert_u, expert_v, ln2_g, ln2_b):
    nb = x_prompt.shape[0]
    x = jnp.concatenate([x_prompt, x_sample], axis=0)
    mem = jnp.concatenate([mem_prompt, mem_sample], axis=0)
    for l in range(DEPTH):
        x = encoder_layer(x, mem, rel_bias, w_in[l], b_in[l], conv_w[l], conv_b[l], attn_sink[l],
                          f_w_in[l], f_w_mid[l], f_b[l], f_freq[l], f_w_out[l], hyena_bias[l],
                          w_mem_kv[l], w_branch[l], w_out[l], ln1_g[l], ln1_b[l], w_query[l],
                          sub_keys[l], expert_u[l], expert_v[l], ln2_g[l], ln2_b[l])
    return (x[:nb], x[nb:])
```

```python
import functools
import math

import jax
import jax.numpy as jnp
from jax import lax
from jax.experimental import pallas as pl
from jax.experimental.pallas import tpu as pltpu

D_MODEL = 1024
DEPTH = 2
ATT_HEADS = 8
ATT_KV_HEADS = 2
ATT_HEAD_DIM = 64
ATT_WIDTH = ATT_HEADS * ATT_HEAD_DIM
ATT_KV_WIDTH = ATT_KV_HEADS * ATT_HEAD_DIM
WINDOW = 128
BLOCK = 128
N_BUCKETS = 32
MAX_DISTANCE = 128
HYENA_WIDTH = 512
HYENA_ORDER = 2
SHORT_CONV = 3
FILTER_EMB = 33
FILTER_BANDS = (FILTER_EMB - 1) // 2
FILTER_HIDDEN = 64
FILTER_INNER = 2
FAST_DECAY_PCT = 0.3
SLOW_DECAY_PCT = 1.5
DECAY_TARGET = 1e-2
MEM_HEADS = 4
MEM_HEAD_DIM = 128
MEM_WIDTH = MEM_HEADS * MEM_HEAD_DIM
N_BRANCH = 3
BRANCH_WIDTH = 512
PEER_HEADS = 8
N_KEYS = 128
PEER_TOPK = 16
PEER_HALF = 128
PEER_QDIM = 2 * PEER_HALF
O_K = ATT_WIDTH
O_V = O_K + ATT_KV_WIDTH
O_HY = O_V + ATT_KV_WIDTH
O_MQ = O_HY + (HYENA_ORDER + 1) * HYENA_WIDTH
O_GATE = O_MQ + MEM_WIDTH
IN_WIDTH = O_GATE + N_BRANCH * D_MODEL
ALPHA = (2 * DEPTH) ** 0.25
LN_EPS = 1e-5
NEG_INF = -1e30

VMEM_LIMIT_BYTES = 56 * 1024 * 1024


def _linear_kernel(x_ref, w_ref, b_ref, o_ref, *, sigmoid):
    x = x_ref[...].astype(jnp.bfloat16)
    acc = jnp.dot(x, w_ref[...], preferred_element_type=jnp.float32) + b_ref[...]
    o_ref[...] = jax.nn.sigmoid(acc) if sigmoid else acc


def linear(x, w, b, *, tm=512, tn=None, sigmoid=False):
    T, K = x.shape
    N = w.shape[1]
    tn = N if tn is None else tn
    wb = w.astype(jnp.bfloat16)
    return pl.pallas_call(
        functools.partial(_linear_kernel, sigmoid=sigmoid),
        grid=(N // tn, T // tm),
        in_specs=[
            pl.BlockSpec((tm, K), lambda j, i: (i, 0)),
            pl.BlockSpec((K, tn), lambda j, i: (0, j)),
            pl.BlockSpec((1, tn), lambda j, i: (0, j)),
        ],
        out_specs=pl.BlockSpec((tm, tn), lambda j, i: (i, j)),
        out_shape=jax.ShapeDtypeStruct((T, N), jnp.float32),
        compiler_params=pltpu.CompilerParams(
            dimension_semantics=("arbitrary", "arbitrary"),
            vmem_limit_bytes=VMEM_LIMIT_BYTES),
        name="linear",
    )(x, wb, b.reshape(1, N))


ATT_TQ = 512
ATT_KEYS = 3 * BLOCK
_NT_DIMS = (((1,), (1,)), ((), ()))


def t5_bucket(rel):
    nb = N_BUCKETS // 2
    max_exact = nb // 2
    ret = jnp.where(rel > 0, nb, 0)
    n = jnp.abs(rel)
    nf = jnp.maximum(n, 1).astype(jnp.float32)
    large = max_exact + (jnp.log(nf / max_exact) / math.log(MAX_DISTANCE / max_exact)
                         * (nb - max_exact)).astype(jnp.int32)
    large = jnp.minimum(large, nb - 1)
    return ret + jnp.where(n < max_exact, n, large)


def window_bias_table(rel_bias):
    rel = (jnp.arange(ATT_KEYS) - BLOCK)[None, :] - jnp.arange(BLOCK)[:, None]
    bias = rel_bias[t5_bucket(rel)].astype(jnp.float32).transpose(2, 0, 1)
    return jnp.where((jnp.abs(rel) <= WINDOW)[None], bias, NEG_INF)


def _col_params():
    return pltpu.CompilerParams(dimension_semantics=("arbitrary", "arbitrary"), vmem_limit_bytes=VMEM_LIMIT_BYTES)


def _window_attn_kernel(sink_ref, q_ref, kp_ref, kc_ref, kn_ref, vp_ref, vc_ref, vn_ref, bias_ref, o_ref, *, n_steps):
    i = pl.program_id(1)
    bf = jnp.bfloat16
    q = q_ref[0].astype(bf)
    k_all = jnp.concatenate([kp_ref[0], kc_ref[0], kn_ref[0]], axis=0).astype(bf)
    v_all = jnp.concatenate([vp_ref[0], vc_ref[0], vn_ref[0]], axis=0).astype(bf)
    col = lax.broadcasted_iota(jnp.int32, (BLOCK, ATT_KEYS), 1)
    n_blk = ATT_TQ // BLOCK
    scale = ATT_HEAD_DIM ** -0.5
    group = ATT_HEADS // ATT_KV_HEADS
    for j in range(n_blk):
        kj = k_all[j * BLOCK:j * BLOCK + ATT_KEYS]
        vj = v_all[j * BLOCK:j * BLOCK + ATT_KEYS]
        off_seq = None
        if j == 0:
            off_seq = (i == 0) & (col < BLOCK)
        if j == n_blk - 1:
            last = (i == n_steps - 1) & (col >= 2 * BLOCK)
            off_seq = last if off_seq is None else off_seq | last
        for h in range(ATT_HEADS):
            hk = h // group
            qh = q[j * BLOCK:(j + 1) * BLOCK, h * ATT_HEAD_DIM:(h + 1) * ATT_HEAD_DIM]
            kh = kj[:, hk * ATT_HEAD_DIM:(hk + 1) * ATT_HEAD_DIM]
            s = lax.dot_general(qh, kh, _NT_DIMS, preferred_element_type=jnp.float32) * scale + bias_ref[h]
            if off_seq is not None:
                s = jnp.where(off_seq, NEG_INF, s)
            sink = sink_ref[h]
            m = jnp.maximum(jnp.max(s, axis=-1, keepdims=True), sink)
            p = jnp.exp(s - m)
            denom = jnp.sum(p, axis=-1, keepdims=True) + jnp.exp(sink - m)
            oh = jnp.dot((p / denom).astype(bf), vj[:, hk * ATT_HEAD_DIM:(hk + 1) * ATT_HEAD_DIM],
                         preferred_element_type=jnp.float32)
            o_ref[0, j * BLOCK:(j + 1) * BLOCK, h * ATT_HEAD_DIM:(h + 1) * ATT_HEAD_DIM] = oh.astype(o_ref.dtype)


def window_attention_pallas(qkv, bias, sink):
    B, S, _ = qkv.shape
    n_steps = S // ATT_TQ
    per = ATT_TQ // BLOCK
    last_blk = S // BLOCK - 1
    k_col, v_col = O_K // ATT_KV_WIDTH, O_V // ATT_KV_WIDTH

    def edge(col, nxt):
        if nxt:
            return pl.BlockSpec((1, BLOCK, ATT_KV_WIDTH), lambda b, i: (b, jnp.minimum((i + 1) * per, last_blk), col))
        return pl.BlockSpec((1, BLOCK, ATT_KV_WIDTH), lambda b, i: (b, jnp.maximum(i * per - 1, 0), col))

    cur = lambda col: pl.BlockSpec((1, ATT_TQ, ATT_KV_WIDTH), lambda b, i: (b, i, col))
    return pl.pallas_call(
        functools.partial(_window_attn_kernel, n_steps=n_steps),
        grid=(B, n_steps),
        in_specs=[pl.BlockSpec(memory_space=pltpu.SMEM),
                  pl.BlockSpec((1, ATT_TQ, ATT_WIDTH), lambda b, i: (b, i, 0)),
                  edge(k_col, False), cur(k_col), edge(k_col, True),
                  edge(v_col, False), cur(v_col), edge(v_col, True),
                  pl.BlockSpec(bias.shape, lambda b, i: (0, 0, 0))],
        out_specs=pl.BlockSpec((1, ATT_TQ, ATT_WIDTH), lambda b, i: (b, i, 0)),
        out_shape=jax.ShapeDtypeStruct((B, S, ATT_WIDTH), jnp.bfloat16),
        compiler_params=_col_params(),
        name="window_attn",
    )(sink.astype(jnp.float32), qkv, qkv, qkv, qkv, qkv, qkv, qkv, bias)


MEM_TQ = 512


def _mem_attn_kernel(q_ref, kv_ref, o_ref):
    bf = jnp.bfloat16
    q = q_ref[0].astype(bf)
    kv = kv_ref[0].astype(bf)
    scale = MEM_HEAD_DIM ** -0.5
    for h in range(MEM_HEADS):
        sl = slice(h * MEM_HEAD_DIM, (h + 1) * MEM_HEAD_DIM)
        s = lax.dot_general(q[:, sl], kv[:, sl], _NT_DIMS, preferred_element_type=jnp.float32) * scale
        p = jnp.exp(s - jnp.max(s, axis=-1, keepdims=True))
        p = (p / jnp.sum(p, axis=-1, keepdims=True)).astype(bf)
        vh = kv[:, MEM_WIDTH + h * MEM_HEAD_DIM:MEM_WIDTH + (h + 1) * MEM_HEAD_DIM]
        o_ref[0, :, sl] = jnp.dot(p, vh, preferred_element_type=jnp.float32).astype(o_ref.dtype)


def memory_attention_pallas(q, kv):
    B, S, _ = q.shape
    M = kv.shape[1]
    return pl.pallas_call(
        _mem_attn_kernel,
        grid=(B, S // MEM_TQ),
        in_specs=[pl.BlockSpec((1, MEM_TQ, MEM_WIDTH), lambda b, i: (b, i, 0)),
                  pl.BlockSpec((1, M, 2 * MEM_WIDTH), lambda b, i: (b, 0, 0))],
        out_specs=pl.BlockSpec((1, MEM_TQ, MEM_WIDTH), lambda b, i: (b, i, 0)),
        out_shape=jax.ShapeDtypeStruct((B, S, MEM_WIDTH), jnp.bfloat16),
        compiler_params=_col_params(),
        name="mem_attn",
    )(q, kv)


MERGE_TM = 256


def _layer_norm_rows(y, g, b):
    mu = jnp.mean(y, axis=-1, keepdims=True)
    d = y - mu
    var = jnp.mean(d * d, axis=-1, keepdims=True)
    return d * lax.rsqrt(var + LN_EPS) * g + b


def _merge_kernel(a_ref, h_ref, m_ref, g_ref, x_ref, wb_ref, wo_ref, lg_ref, lb_ref, o_ref):
    bf = jnp.bfloat16
    f32 = jnp.float32
    merged = g_ref[:, 0:D_MODEL] * jnp.dot(a_ref[...].astype(bf), wb_ref[0], preferred_element_type=f32)
    merged = merged + g_ref[:, D_MODEL:2 * D_MODEL] * jnp.dot(h_ref[...].astype(bf), wb_ref[1], preferred_element_type=f32)
    merged = merged + g_ref[:, 2 * D_MODEL:] * jnp.dot(m_ref[...].astype(bf), wb_ref[2], preferred_element_type=f32)
    y = ALPHA * x_ref[...] + jnp.dot(merged.astype(bf), wo_ref[...], preferred_element_type=f32)
    o_ref[...] = _layer_norm_rows(y, lg_ref[...], lb_ref[...])


def merge_norm(att, hy, mem, gates, x, w_branch, w_out, ln_g, ln_b):
    T = x.shape[0]
    bf = jnp.bfloat16
    rows = lambda w: pl.BlockSpec((MERGE_TM, w), lambda i: (i, 0))
    whole = lambda a: pl.BlockSpec(a.shape, lambda i: (0,) * a.ndim)
    wb, wo = w_branch.astype(bf), w_out.astype(bf)
    lg, lb = ln_g.reshape(1, D_MODEL), ln_b.reshape(1, D_MODEL)
    return pl.pallas_call(
        _merge_kernel,
        grid=(T // MERGE_TM,),
        in_specs=[rows(BRANCH_WIDTH), rows(BRANCH_WIDTH), rows(BRANCH_WIDTH), rows(N_BRANCH * D_MODEL), rows(D_MODEL),
                  whole(wb), whole(wo), whole(lg), whole(lb)],
        out_specs=rows(D_MODEL),
        out_shape=jax.ShapeDtypeStruct((T, D_MODEL), jnp.float32),
        compiler_params=pltpu.CompilerParams(dimension_semantics=("arbitrary",), vmem_limit_bytes=VMEM_LIMIT_BYTES),
        name="merge_norm",
    )(att, hy, mem, gates, x, wb, wo, lg, lb)


def _residual_norm_kernel(x_ref, r_ref, lg_ref, lb_ref, o_ref):
    o_ref[...] = _layer_norm_rows(ALPHA * x_ref[...] + r_ref[...], lg_ref[...], lb_ref[...])


def residual_norm(x, r, ln_g, ln_b):
    T = x.shape[0]
    tm = 512
    rows = pl.BlockSpec((tm, D_MODEL), lambda i: (i, 0))
    one = pl.BlockSpec((1, D_MODEL), lambda i: (0, 0))
    return pl.pallas_call(
        _residual_norm_kernel,
        grid=(T // tm,),
        in_specs=[rows, rows, one, one],
        out_specs=rows,
        out_shape=jax.ShapeDtypeStruct((T, D_MODEL), jnp.float32),
        compiler_params=pltpu.CompilerParams(dimension_semantics=("arbitrary",), vmem_limit_bytes=VMEM_LIMIT_BYTES),
        name="residual_norm",
    )(x, r, ln_g.reshape(1, D_MODEL), ln_b.reshape(1, D_MODEL))


FFT_N1 = 256
FFT_N2 = 128
FFT_N = FFT_N1 * FFT_N2
HY_COLS = FFT_N2 * HYENA_WIDTH
LMUL_TN = 2048
KB_K1 = 8
FILT_TM = 512


def _dft_constants():
    f32 = jnp.float32
    n1 = jnp.arange(FFT_N1, dtype=jnp.int32)
    ang1 = ((n1[:, None] * n1[None, :]) % FFT_N1).astype(f32) * f32(2.0 * math.pi / FFT_N1)
    fr, fi = jnp.cos(ang1), -jnp.sin(ang1)
    h = FFT_N1 // 2
    f_fwd = jnp.concatenate([jnp.concatenate([fr[:, :h], -fi[:, :h]], axis=1),
                             jnp.concatenate([fi[:, :h], fr[:, :h]], axis=1)], axis=0)
    f_inv = f_fwd.T * f32(1.0 / FFT_N)
    f_real = jnp.concatenate([fr, fi], axis=0)
    k2 = jnp.arange(FFT_N2, dtype=jnp.int32)
    k = n1[:, None, None] + FFT_N1 * k2[None, :, None]
    ang2 = ((k * k2[None, None, :]) % FFT_N).astype(f32) * f32(2.0 * math.pi / FFT_N)
    gr, gi = jnp.cos(ang2), -jnp.sin(ang2)
    g = jnp.concatenate([jnp.concatenate([gr, -gi], axis=2), jnp.concatenate([gi, gr], axis=2)], axis=1)
    bf = jnp.bfloat16
    return f_fwd.astype(bf), f_inv.astype(bf), f_real.astype(bf), g.astype(bf), g.transpose(0, 2, 1).astype(bf)


def _lmul_pair_kernel(l_ref, xr_ref, xi_ref, o_ref):
    x = jnp.concatenate([xr_ref[0], xi_ref[0]], axis=0).astype(jnp.bfloat16)
    o_ref[0] = jnp.dot(l_ref[...], x, preferred_element_type=jnp.float32).astype(o_ref.dtype)


def _lmul_kernel(l_ref, x_ref, o_ref):
    o_ref[0] = jnp.dot(l_ref[...], x_ref[0].astype(jnp.bfloat16),
                       preferred_element_type=jnp.float32).astype(o_ref.dtype)


def _lmul_gate_kernel(l_ref, a_ref, zr_ref, zi_ref, pr_ref, pi_ref, b_ref, o_ref):
    y = jnp.dot(l_ref[...], a_ref[0], preferred_element_type=jnp.float32)
    h = FFT_N1 // 2
    o_ref[0, :h] = pr_ref[0] * (y[:h] + zr_ref[0] * b_ref[...])
    o_ref[0, h:] = pi_ref[0] * (y[h:] + zi_ref[0] * b_ref[...])


def dft_stage1_pairs(f_fwd, z):
    nb, h, cols = z.shape
    npair = (nb + 1) // 2
    last = nb - 1
    return pl.pallas_call(
        _lmul_pair_kernel,
        grid=(npair, cols // LMUL_TN),
        in_specs=[
            pl.BlockSpec(f_fwd.shape, lambda p, j: (0, 0)),
            pl.BlockSpec((1, h, LMUL_TN), lambda p, j: (jnp.minimum(2 * p, last), 0, j)),
            pl.BlockSpec((1, h, LMUL_TN), lambda p, j: (jnp.minimum(2 * p + 1, last), 0, j)),
        ],
        out_specs=pl.BlockSpec((1, 2 * FFT_N1, LMUL_TN), lambda p, j: (p, 0, j)),
        out_shape=jax.ShapeDtypeStruct((npair, 2 * FFT_N1, cols), jnp.bfloat16),
        compiler_params=_col_params(),
        name="dft_stage1",
    )(f_fwd, z, z)


def dft_stage1_real(f_real, x):
    _, h, cols = x.shape
    return pl.pallas_call(
        _lmul_kernel,
        grid=(1, cols // LMUL_TN),
        in_specs=[pl.BlockSpec(f_real.shape, lambda p, j: (0, 0)),
                  pl.BlockSpec((1, h, LMUL_TN), lambda p, j: (p, 0, j))],
        out_specs=pl.BlockSpec((1, 2 * FFT_N1, LMUL_TN), lambda p, j: (p, 0, j)),
        out_shape=jax.ShapeDtypeStruct((1, 2 * FFT_N1, cols), jnp.bfloat16),
        compiler_params=_col_params(),
        name="dft_stage1_real",
    )(f_real, x)


def idft_stage1_gate(f_inv, a, z, p, bias_row):
    npair, _, cols = a.shape
    half = FFT_N1 // 2

    def half_blk(x, r):
        last = x.shape[0] - 1
        return pl.BlockSpec((1, half, LMUL_TN), lambda q, j: (jnp.minimum(2 * q + r, last), 0, j))

    out = pl.pallas_call(
        _lmul_gate_kernel,
        grid=(npair, cols // LMUL_TN),
        in_specs=[pl.BlockSpec(f_inv.shape, lambda q, j: (0, 0)),
                  pl.BlockSpec((1, 2 * FFT_N1, LMUL_TN), lambda q, j: (q, 0, j)),
                  half_blk(z, 0), half_blk(z, 1), half_blk(p, 0), half_blk(p, 1),
                  pl.BlockSpec((1, LMUL_TN), lambda q, j: (0, 0))],
        out_specs=pl.BlockSpec((1, FFT_N1, LMUL_TN), lambda q, j: (q, 0, j)),
        out_shape=jax.ShapeDtypeStruct((npair, FFT_N1, cols), jnp.float32),
        compiler_params=_col_params(),
        name="idft_stage1_gate",
    )(f_inv, a, z, z, p, p, bias_row)
    return out.reshape(2 * npair, half, cols)


def _stage2_conv_kernel(a_ref, g_ref, gt_ref, h_ref, o_ref):
    for j in range(KB_K1):
        x = jnp.concatenate([a_ref[0, 0, j], a_ref[0, 1, j]], axis=0)
        z = jnp.dot(g_ref[j], x, preferred_element_type=jnp.float32)
        zr, zi = z[:FFT_N2], z[FFT_N2:]
        hr, hi = h_ref[0, j], h_ref[1, j]
        w = jnp.concatenate([zr * hr - zi * hi, zr * hi + zi * hr], axis=0).astype(jnp.bfloat16)
        y = jnp.dot(gt_ref[j], w, preferred_element_type=jnp.float32)
        o_ref[0, 0, j] = y[:FFT_N2].astype(o_ref.dtype)
        o_ref[0, 1, j] = y[FFT_N2:].astype(o_ref.dtype)


def stage2_conv(a, g, gt, hf, order):
    npair = a.shape[0]
    C = HYENA_WIDTH
    ablk = pl.BlockSpec((1, 2, KB_K1, FFT_N2, C), lambda p, i: (p, 0, i, 0, 0))
    gblk = pl.BlockSpec((KB_K1, 2 * FFT_N2, 2 * FFT_N2), lambda p, i: (i, 0, 0))
    return pl.pallas_call(
        _stage2_conv_kernel,
        grid=(npair, FFT_N1 // KB_K1),
        in_specs=[ablk, gblk, gblk,
                  pl.BlockSpec((2, KB_K1, FFT_N2, C), lambda p, i: (0, i, 0, order))],
        out_specs=ablk,
        out_shape=jax.ShapeDtypeStruct(a.shape, a.dtype),
        compiler_params=_col_params(),
        name="stage2_conv",
    )(a, g, gt, hf)


def _stage2_filter_kernel(a_ref, g_ref, s_ref, o_ref):
    for j in range(KB_K1):
        x = jnp.concatenate([a_ref[0, j], a_ref[1, j]], axis=0)
        z = jnp.dot(g_ref[j], x, preferred_element_type=jnp.float32) * s_ref[...]
        o_ref[0, j] = z[:FFT_N2]
        o_ref[1, j] = z[FFT_N2:]


def stage2_filter(a, g, inv_norm):
    W = a.shape[-1]
    blk = pl.BlockSpec((2, KB_K1, FFT_N2, W), lambda i: (0, i, 0, 0))
    return pl.pallas_call(
        _stage2_filter_kernel,
        grid=(FFT_N1 // KB_K1,),
        in_specs=[blk, pl.BlockSpec((KB_K1, 2 * FFT_N2, 2 * FFT_N2), lambda i: (i, 0, 0)),
                  pl.BlockSpec((1, W), lambda i: (0, 0))],
        out_specs=blk,
        out_shape=jax.ShapeDtypeStruct(a.shape, jnp.float32),
        compiler_params=pltpu.CompilerParams(dimension_semantics=("arbitrary",), vmem_limit_bytes=VMEM_LIMIT_BYTES),
        name="stage2_filter",
    )(a, g, inv_norm)


def _filter_gen_kernel(bands_ref, w0_ref, wc_ref, ws_ref, wmid_ref, fb_ref, ff_ref, wdir_ref, wbwd_ref,
                       adelta_ref, two_ref, norm_ref, *, seq):
    i = pl.program_id(0)
    bf = jnp.bfloat16
    row = i * FILT_TM + lax.broadcasted_iota(jnp.int32, (FILT_TM, 1), 0)
    pos_i = jnp.where(row < seq, row, 2 * seq - row)
    pos = pos_i.astype(jnp.float32)
    t = pos / max(seq - 1, 1)
    w = (jnp.float32(2.0 * math.pi) * pos) / seq
    ang = w * bands_ref[...]
    pre = (jnp.dot(jnp.cos(ang).astype(bf), wc_ref[...], preferred_element_type=jnp.float32)
           + jnp.dot((-jnp.sin(ang)).astype(bf), ws_ref[...], preferred_element_type=jnp.float32)
           + t.astype(bf).astype(jnp.float32) * w0_ref[...])
    h = jnp.sin(ff_ref[0:1, :] * (pre + fb_ref[0:1, :]))
    for m in range(FILTER_INNER):
        pre = jnp.dot(h.astype(bf), wmid_ref[m], preferred_element_type=jnp.float32)
        h = jnp.sin(ff_ref[m + 1:m + 2, :] * (pre + fb_ref[m + 1:m + 2, :]))
    hb = h.astype(bf)
    decay = jnp.exp(-t * adelta_ref[...])
    out = jnp.dot(hb, wdir_ref[0], preferred_element_type=jnp.float32) * decay
    out = jnp.where(pos_i < seq, out, 0.0)

    @pl.when(i == 0)
    def _():
        extra = jnp.dot(hb, wbwd_ref[...], preferred_element_type=jnp.float32) * decay
        first = jnp.where(row == 0, out + extra, out)
        two_ref[...] = first
        norm_ref[...] = jnp.sum(jnp.abs(first), axis=0, keepdims=True)

    @pl.when(i != 0)
    def _():
        two_ref[...] = out
        norm_ref[...] += jnp.sum(jnp.abs(out), axis=0, keepdims=True)


def hyena_filter_rows(seq, f_w_in, f_w_mid, f_b, f_freq, f_w_out):
    f32, bf = jnp.float32, jnp.bfloat16
    C, H = HYENA_WIDTH, FILTER_HIDDEN
    bands = jnp.linspace(1e-4, FILTER_BANDS - 1, FILTER_BANDS, dtype=f32)
    bands = jnp.pad(bands, (0, VREG_LANES - FILTER_BANDS)).reshape(1, VREG_LANES)
    w_in = f_w_in.astype(bf)
    w0 = w_in[0:1].astype(f32)
    pad_rows = ((0, VREG_LANES - FILTER_BANDS), (0, 0))
    wc = jnp.pad(w_in[1:1 + FILTER_BANDS], pad_rows)
    ws = jnp.pad(w_in[1 + FILTER_BANDS:], pad_rows)
    w_out = f_w_out.astype(bf).reshape(H, HYENA_ORDER, 2, C)
    wdir = jnp.stack([w_out[:, :, 0].reshape(H, HYENA_ORDER * C), w_out[:, :, 1].reshape(H, HYENA_ORDER * C)])
    max_decay = math.log(DECAY_TARGET) / FAST_DECAY_PCT
    min_decay = math.log(DECAY_TARGET) / SLOW_DECAY_PCT
    adelta = jnp.abs(jnp.linspace(min_decay, max_decay, C, dtype=f32))
    adelta = jnp.tile(adelta, HYENA_ORDER).reshape(1, HYENA_ORDER * C)
    n_tiles = 2 * seq // FILT_TM
    half_tiles = seq // FILT_TM
    whole = lambda a: pl.BlockSpec(a.shape, lambda i: (0,) * a.ndim)
    wmid = f_w_mid.astype(bf)
    fb, ff = f_b.astype(f32), f_freq.astype(f32)
    return pl.pallas_call(
        functools.partial(_filter_gen_kernel, seq=seq),
        grid=(n_tiles,),
        in_specs=[whole(bands), whole(w0), whole(wc), whole(ws), whole(wmid), whole(fb), whole(ff),
                  pl.BlockSpec((1, H, HYENA_ORDER * C), lambda i: (i // half_tiles, 0, 0)),
                  pl.BlockSpec((None, H, HYENA_ORDER * C), lambda i: (1, 0, 0)),
                  whole(adelta)],
        out_specs=[pl.BlockSpec((FILT_TM, HYENA_ORDER * C), lambda i: (i, 0)),
                   pl.BlockSpec((1, HYENA_ORDER * C), lambda i: (0, 0))],
        out_shape=[jax.ShapeDtypeStruct((2 * seq, HYENA_ORDER * C), f32),
                   jax.ShapeDtypeStruct((1, HYENA_ORDER * C), f32)],
        compiler_params=pltpu.CompilerParams(dimension_semantics=("arbitrary",), vmem_limit_bytes=VMEM_LIMIT_BYTES),
        name="hyena_filter_gen",
    )(bands, w0, wc, ws, wmid, fb, ff, wdir, wdir, adelta)


def _short_conv_kernel(x_ref, w_ref, b_ref, o_ref):
    x = x_ref[0]
    n = x.shape[0]
    t = lax.broadcasted_iota(jnp.int32, x.shape, 0)
    prev = jnp.where(t == 0, 0.0, pltpu.roll(x, 1, 0))
    nxt = jnp.where(t == n - 1, 0.0, pltpu.roll(x, n - 1, 0))
    o_ref[0] = ((b_ref[...] + prev * w_ref[0:1, :]) + x * w_ref[1:2, :]) + nxt * w_ref[2:3, :]


def short_conv_pallas(u, w, b):
    B, L, W = u.shape
    tc = VREG_LANES
    return pl.pallas_call(
        _short_conv_kernel,
        grid=(B, W // tc),
        in_specs=[pl.BlockSpec((1, L, tc), lambda i, j: (i, 0, j)),
                  pl.BlockSpec((SHORT_CONV, tc), lambda i, j: (0, j)),
                  pl.BlockSpec((1, tc), lambda i, j: (0, j))],
        out_specs=pl.BlockSpec((1, L, tc), lambda i, j: (i, 0, j)),
        out_shape=jax.ShapeDtypeStruct((B, L, W), jnp.float32),
        compiler_params=_col_params(),
        name="short_conv",
    )(u, w, b.reshape(1, W))


def hyena_mixer_pallas(u, conv_w, conv_b, f_w_in, f_w_mid, f_b, f_freq, f_w_out, hyena_bias):
    B, L, _ = u.shape
    assert 2 * L == FFT_N
    C = HYENA_WIDTH
    f_fwd, f_inv, f_real, g, gt = _dft_constants()
    two, norm = hyena_filter_rows(L, f_w_in, f_w_mid, f_b, f_freq, f_w_out)
    af = dft_stage1_real(f_real, two.reshape(1, FFT_N1, FFT_N2 * HYENA_ORDER * C))
    af = af.reshape(2, FFT_N1, FFT_N2, HYENA_ORDER * C)
    hf = stage2_filter(af, g, 1.0 / norm)
    uc = short_conv_pallas(u, conv_w, conv_b)
    half = FFT_N1 // 2
    parts = [uc[..., o * C:(o + 1) * C].reshape(B, half, HY_COLS) for o in range(HYENA_ORDER + 1)]
    z = parts[0]
    for o in range(HYENA_ORDER):
        a = dft_stage1_pairs(f_fwd, z)
        npair = a.shape[0]
        a = stage2_conv(a.reshape(npair, 2, FFT_N1, FFT_N2, C), g, gt, hf, o)
        bias_row = jnp.tile(hyena_bias[o], LMUL_TN // C).reshape(1, LMUL_TN)
        z = idft_stage1_gate(f_inv, a.reshape(npair, 2 * FFT_N1, HY_COLS), z, parts[o + 1], bias_row)
    return z[:B].reshape(B, L, C)


PEER_SEL = PEER_HEADS * PEER_TOPK
PEER_TB = 128
VREG_SUBLANES = 8
VREG_LANES = 128
EXPERT_ROWS = 4
TABLE_LEAD_ROWS = 8
TILE_ROWS = 16
V_UNROLL = 16
IDX_SPLIT = 16
IDX_PER = PEER_SEL // IDX_SPLIT


def _row_reader(idx_refs, t):
    return lambda k: idx_refs[k // IDX_PER][0, k % IDX_PER, t]


def pack_expert_table(tab):
    e = tab.shape[0]
    b = lax.bitcast_convert_type(tab.astype(jnp.bfloat16), jnp.uint16).astype(jnp.uint32)
    b = b.reshape(e, 2, EXPERT_ROWS, VREG_LANES)
    w = (b[:, 0] | (b[:, 1] << 16)).reshape(e * EXPERT_ROWS, VREG_LANES)
    return jnp.pad(w, ((TABLE_LEAD_ROWS, VREG_SUBLANES), (0, 0)))


def from_tile_rows(y):
    T = y.shape[0]
    return y.reshape(T, EXPERT_ROWS, 2, VREG_LANES).transpose(0, 2, 1, 3).reshape(T, D_MODEL)


def _split_bf16(v):
    hi = v.astype(jnp.bfloat16)
    return hi, (v - hi.astype(jnp.float32)).astype(jnp.bfloat16)


def _gelu_exact(x):
    return 0.5 * x * (1.0 + lax.erf(x * (2.0 ** -0.5)))


_COMBINE_POS = (3, 7, 1, 5, 2, 6, 0, 4)


def _sublane_sums(c, sub):
    mv = (sub & 2) != 0
    e = []
    for c1, c2 in ((c[0], c[1]), (c[2], c[3])):
        e.append(jnp.where(mv, c1 + pltpu.roll(c1, 2, 0), c2 + pltpu.roll(c2, 6, 0)))
    mo = (sub & 1) != 0
    return jnp.where(mo, e[0] + pltpu.roll(e[0], 1, 0), e[1] + pltpu.roll(e[1], 7, 0))


def _peer_u_kernel(*refs):
    idx_refs = refs[:IDX_SPLIT]
    x_ref, g_ref, tab_ref, o_ref, act_ref, r_ref = refs[IDX_SPLIT:]
    sub = lax.broadcasted_iota(jnp.int32, (VREG_SUBLANES, VREG_LANES), 0)
    lo4 = sub < EXPERT_ROWS

    def token(t, carry):
        sel_row = _row_reader(idx_refs, t)
        x8 = x_ref[t]
        xr = pltpu.roll(x8, EXPERT_ROWS, 0)
        x_lo = jnp.where(lo4, x8, xr)
        x_hi = jnp.where(lo4, xr, x8)
        rs = []
        for grp in range(PEER_SEL // VREG_SUBLANES):
            pairs = []
            for i in range(VREG_SUBLANES // 2):
                wa = tab_ref[pl.ds(sel_row(grp * VREG_SUBLANES + _COMBINE_POS[2 * i]), VREG_SUBLANES), :]
                wb = tab_ref[pl.ds(sel_row(grp * VREG_SUBLANES + _COMBINE_POS[2 * i + 1]) - EXPERT_ROWS,
                                   VREG_SUBLANES), :]
                w = jnp.where(lo4, wa, wb)
                lo = lax.bitcast_convert_type(w << 16, jnp.float32)
                hi = lax.bitcast_convert_type(w & jnp.uint32(0xFFFF0000), jnp.float32)
                pairs.append(lo * x_lo + hi * x_hi)
            rs.append(_sublane_sums(pairs, sub))
        r0 = pl.multiple_of(t * PEER_SEL, PEER_SEL)
        r_ref[pl.ds(r0, PEER_SEL), :] = jnp.concatenate(rs, axis=0).astype(jnp.bfloat16)
        return carry

    lax.fori_loop(0, PEER_TB, token, 0)
    ones = jnp.ones((VREG_LANES, VREG_LANES), jnp.bfloat16)
    keep = (lax.broadcasted_iota(jnp.int32, (PEER_SEL, VREG_LANES), 0)
            == lax.broadcasted_iota(jnp.int32, (PEER_SEL, VREG_LANES), 1))
    chunk = VREG_SUBLANES * PEER_SEL
    for c in range(PEER_TB // VREG_SUBLANES):
        s = jnp.dot(r_ref[c * chunk:(c + 1) * chunk, :], ones, preferred_element_type=jnp.float32)
        for j in range(VREG_SUBLANES):
            blk = jnp.where(keep, s[j * PEER_SEL:(j + 1) * PEER_SEL], 0.0)
            act_ref[c * VREG_SUBLANES + j:c * VREG_SUBLANES + j + 1, :] = jnp.sum(blk, axis=0, keepdims=True)
    o_ref[...] = g_ref[...] * _gelu_exact(act_ref[...])


def _peer_v_kernel(*refs):
    idx_refs = refs[:IDX_SPLIT]
    coef_ref, e_ref, d_ref, tab_ref, o_ref, chi_ref, clo_ref = refs[IDX_SPLIT:]
    c_hi, c_lo = _split_bf16(coef_ref[...])
    chi_ref[...] = jnp.dot(c_hi, e_ref[...], preferred_element_type=jnp.float32)
    clo_ref[...] = jnp.dot(c_lo, e_ref[...], preferred_element_type=jnp.float32)

    def token(t, carry):
        sel_row = _row_reader(idx_refs, t)
        cmat = jnp.concatenate([chi_ref[pl.ds(t, 1), :] * d_ref[...], clo_ref[pl.ds(t, 1), :] * d_ref[...]],
                               axis=0).astype(jnp.bfloat16)
        w = jnp.concatenate([pltpu.bitcast(tab_ref[pl.ds(sel_row(k), VREG_SUBLANES), :], jnp.bfloat16)
                             for k in range(PEER_SEL)], axis=0)
        acc = jnp.dot(cmat, w, preferred_element_type=jnp.float32)
        o_ref[t] = acc[:VREG_SUBLANES] + acc[VREG_SUBLANES:]
        return carry

    lax.fori_loop(0, PEER_TB, token, 0, unroll=V_UNROLL)


def peer_experts(x, eidx, g, tab_u, tab_v):
    T = x.shape[0]
    grid = (T // PEER_TB,)
    f32, bf = jnp.float32, jnp.bfloat16
    sel = jnp.arange(PEER_SEL, dtype=jnp.int32)
    col = jnp.arange(PEER_SEL * TILE_ROWS, dtype=jnp.int32)
    expand = (col[None, :] // TILE_ROWS == sel[:, None]).astype(bf)
    diag = (col[None, :] % TILE_ROWS == jnp.arange(VREG_SUBLANES, dtype=jnp.int32)[:, None]).astype(f32)
    assert eidx.shape == (T // PEER_TB, PEER_SEL, PEER_TB)
    idx_specs = [pl.BlockSpec((1, IDX_PER, PEER_TB), lambda i, p=p: (i, p, 0), memory_space=pltpu.SMEM)
                 for p in range(IDX_SPLIT)]
    idx_parts = [eidx] * IDX_SPLIT
    vec_blk = pl.BlockSpec((PEER_TB, PEER_SEL), lambda i: (i, 0))
    row_blk = pl.BlockSpec((PEER_TB, VREG_SUBLANES, VREG_LANES), lambda i: (i, 0, 0))
    whole = lambda a: pl.BlockSpec(a.shape, lambda i: (0,) * a.ndim)
    tab_spec = pl.BlockSpec(memory_space=pltpu.VMEM)
    params = pltpu.CompilerParams(dimension_semantics=("arbitrary",), vmem_limit_bytes=VMEM_LIMIT_BYTES)
    coef = pl.pallas_call(
        _peer_u_kernel,
        grid=grid,
        in_specs=idx_specs + [row_blk, vec_blk, tab_spec],
        out_specs=vec_blk,
        out_shape=jax.ShapeDtypeStruct((T, PEER_SEL), f32),
        scratch_shapes=[pltpu.VMEM((PEER_TB, PEER_SEL), f32),
                        pltpu.VMEM((PEER_TB * PEER_SEL, VREG_LANES), bf)],
        compiler_params=params,
        name="peer_u",
    )(*idx_parts, x.reshape(T, VREG_SUBLANES, VREG_LANES), g, tab_u)
    out = pl.pallas_call(
        _peer_v_kernel,
        grid=grid,
        in_specs=idx_specs + [vec_blk, whole(expand), whole(diag), tab_spec],
        out_specs=row_blk,
        out_shape=jax.ShapeDtypeStruct((T, VREG_SUBLANES, VREG_LANES), f32),
        scratch_shapes=[pltpu.VMEM((PEER_TB, PEER_SEL * TILE_ROWS), f32),
                        pltpu.VMEM((PEER_TB, PEER_SEL * TILE_ROWS), f32)],
        compiler_params=params,
        name="peer_v",
    )(*idx_parts, coef, expand, diag, tab_v)
    return from_tile_rows(out)


ROUTE_TM = 512


def _top16_rows(s, key_id):
    row16 = lax.broadcasted_iota(jnp.int32, (PEER_TOPK, VREG_LANES), 0)
    vals = jnp.zeros((PEER_TOPK, VREG_LANES), jnp.float32)
    ids = jnp.zeros((PEER_TOPK, VREG_LANES), jnp.float32)
    big = jnp.float32(2 ** 30)
    for j in range(PEER_TOPK):
        m = jnp.max(s, axis=0, keepdims=True)
        am = jnp.min(jnp.where(s == m, key_id, big), axis=0, keepdims=True)
        vals = jnp.where(row16 == j, m, vals)
        ids = jnp.where(row16 == j, am, ids)
        if j + 1 < PEER_TOPK:
            s = jnp.where(key_id == am, -jnp.inf, s)
    return vals, ids


_PAIR_GROUPS = ((0, 0, 8), (0, 8, 8), (1, 0, 8), (2, 0, 5), (3, 0, 4), (4, 0, 3), (5, 0, 2), (6, 0, 2), (7, 0, 2))


def _route_head(s0, i0, s1, i1):
    sub_i = lax.broadcasted_iota(jnp.int32, (VREG_SUBLANES, VREG_LANES), 0)
    sub = sub_i.astype(jnp.float32)
    cands, flat, eids = [], [], []
    for a, b0, nb in _PAIR_GROUPS:
        c = s0[a:a + 1] + s1[b0:b0 + VREG_SUBLANES]
        cands.append(jnp.where(sub_i < nb, c, -jnp.inf) if nb < VREG_SUBLANES else c)
        flat.append(a * PEER_TOPK + b0 + sub)
        eids.append(i0[a:a + 1] * N_KEYS + i1[b0:b0 + VREG_SUBLANES])
    cands.append(s0[VREG_SUBLANES:] + s1[0:1])
    flat.append((sub + VREG_SUBLANES) * PEER_TOPK)
    eids.append(i0[VREG_SUBLANES:] * N_KEYS + i1[0:1])
    cand = jnp.concatenate(cands, axis=0)
    flat = jnp.concatenate(flat, axis=0)
    eid = jnp.concatenate(eids, axis=0)
    row16 = lax.broadcasted_iota(jnp.int32, (PEER_TOPK, VREG_LANES), 0)
    sc = jnp.zeros((PEER_TOPK, VREG_LANES), jnp.float32)
    sel = jnp.zeros((PEER_TOPK, VREG_LANES), jnp.float32)
    big = jnp.float32(2 ** 30)
    for j in range(PEER_TOPK):
        m = jnp.max(cand, axis=0, keepdims=True)
        am = jnp.min(jnp.where(cand == m, flat, big), axis=0, keepdims=True)
        hit = flat == am
        e = jnp.max(jnp.where(hit, eid, -1.0), axis=0, keepdims=True)
        sc = jnp.where(row16 == j, m, sc)
        sel = jnp.where(row16 == j, e, sel)
        if j + 1 < PEER_TOPK:
            cand = jnp.where(hit, -jnp.inf, cand)
    p = jnp.exp(sc - sc[0:1])
    return sel, p / jnp.sum(p, axis=0, keepdims=True)


def _route_kernel(x_ref, wq_ref, sk_ref, rows_ref, g_ref, q_ref):
    q = jnp.dot(x_ref[...].astype(jnp.bfloat16), wq_ref[...], preferred_element_type=jnp.float32)
    q_ref[...] = q.astype(jnp.bfloat16)
    key_id = lax.broadcasted_iota(jnp.int32, (N_KEYS, VREG_LANES), 0).astype(jnp.float32)

    def head(h, carry):
        tops = []
        for c in range(2):
            hc = h * 2 + c
            qhc = q_ref[:, pl.ds(pl.multiple_of(hc * PEER_HALF, PEER_HALF), PEER_HALF)]
            s = lax.dot_general(sk_ref[hc], qhc, _NT_DIMS, preferred_element_type=jnp.float32)
            tops.append([_top16_rows(s[:, j * VREG_LANES:(j + 1) * VREG_LANES], key_id)
                         for j in range(ROUTE_TM // VREG_LANES)])
        r0 = pl.multiple_of(h * PEER_TOPK, PEER_TOPK)
        for j in range(ROUTE_TM // VREG_LANES):
            (s0, i0), (s1, i1) = tops[0][j], tops[1][j]
            sel, g = _route_head(s0, i0, s1, i1)
            rows_ref[j, pl.ds(r0, PEER_TOPK), :] = (sel * EXPERT_ROWS + TABLE_LEAD_ROWS).astype(jnp.int32)
            g_ref[pl.ds(r0, PEER_TOPK), j * VREG_LANES:(j + 1) * VREG_LANES] = g
        return carry

    lax.fori_loop(0, PEER_HEADS, head, 0)


def peer_route(x, w_query, sub_keys):
    T = x.shape[0]
    assert PEER_TB == VREG_LANES
    wq = w_query.astype(jnp.bfloat16)
    sk = sub_keys.reshape(PEER_HEADS * 2, N_KEYS, PEER_HALF).astype(jnp.bfloat16)
    out_blk = pl.BlockSpec((PEER_SEL, ROUTE_TM), lambda i: (0, i))
    rows, g = pl.pallas_call(
        _route_kernel,
        grid=(T // ROUTE_TM,),
        in_specs=[
            pl.BlockSpec((ROUTE_TM, D_MODEL), lambda i: (i, 0)),
            pl.BlockSpec(wq.shape, lambda i: (0, 0)),
            pl.BlockSpec(sk.shape, lambda i: (0, 0, 0)),
        ],
        out_specs=[pl.BlockSpec((ROUTE_TM // PEER_TB, PEER_SEL, PEER_TB), lambda i: (i, 0, 0)), out_blk],
        out_shape=[jax.ShapeDtypeStruct((T // PEER_TB, PEER_SEL, PEER_TB), jnp.int32),
                   jax.ShapeDtypeStruct((PEER_SEL, T), jnp.float32)],
        scratch_shapes=[pltpu.VMEM((ROUTE_TM, PEER_HEADS * PEER_QDIM), jnp.bfloat16)],
        compiler_params=pltpu.CompilerParams(dimension_semantics=("arbitrary",),
                                             vmem_limit_bytes=VMEM_LIMIT_BYTES),
        name="peer_route",
    )(x, wq, sk)
    return rows, g.T


def encoder_layer(x, mem, rel_bias, w_in, b_in, conv_w, conv_b, attn_sink, f_w_in, f_w_mid, f_b, f_freq,
                  f_w_out, hyena_bias, w_mem_kv, w_branch, w_out, ln1_g, ln1_b, w_query, sub_keys,
                  expert_u, expert_v, ln2_g, ln2_b):
    B, S, _ = x.shape
    T = B * S
    xf = x.reshape(T, D_MODEL)
    proj = lambda lo, hi, **kw: linear(xf, w_in[:, lo:hi], b_in[lo:hi], **kw)
    qkv = proj(0, O_HY).reshape(B, S, O_HY)
    hy = proj(O_HY, O_MQ).reshape(B, S, O_MQ - O_HY)
    q_m = proj(O_MQ, O_GATE).reshape(B, S, MEM_WIDTH)
    gates = proj(O_GATE, IN_WIDTH, tn=(IN_WIDTH - O_GATE) // 2, sigmoid=True)
    M = mem.shape[1]
    kv = linear(mem.reshape(B * M, D_MODEL), w_mem_kv, jnp.zeros((2 * MEM_WIDTH,), jnp.float32), tm=B * M)
    att = window_attention_pallas(qkv, window_bias_table(rel_bias), attn_sink)
    hyo = hyena_mixer_pallas(hy, conv_w, conv_b, f_w_in, f_w_mid, f_b, f_freq, f_w_out, hyena_bias)
    mat = memory_attention_pallas(q_m, kv.reshape(B, M, 2 * MEM_WIDTH))
    x1 = merge_norm(att.reshape(T, ATT_WIDTH), hyo.reshape(T, HYENA_WIDTH), mat.reshape(T, MEM_WIDTH), gates, xf,
                    w_branch, w_out, ln1_g, ln1_b)
    rows, g = peer_route(x1, w_query, sub_keys)
    peer = peer_experts(x1, rows, g, pack_expert_table(expert_u), pack_expert_table(expert_v))
    return residual_norm(x1, peer, ln2_g, ln2_b).reshape(B, S, D_MODEL)


def kernel(x_prompt, x_sample, mem_prompt, mem_sample, rel_bias, w_in, b_in, conv_w, conv_b, attn_sink,
           f_w_in, f_w_mid, f_b, f_freq, f_w_out, hyena_bias, w_mem_kv, w_branch, w_out, ln1_g, ln1_b,
           w_query, sub_keys, expert_u, expert_v, ln2_g, ln2_b):
    nb = x_prompt.shape[0]
    x = jnp.concatenate([x_prompt, x_sample], axis=0)
    mem = jnp.concatenate([mem_prompt, mem_sample], axis=0)
    for l in range(DEPTH):
        x = encoder_layer(x, mem, rel_bias, w_in[l], b_in[l], conv_w[l], conv_b[l], attn_sink[l],
                          f_w_in[l], f_w_mid[l], f_b[l], f_freq[l], f_w_out[l], hyena_bias[l],
                          w_mem_kv[l], w_branch[l], w_out[l], ln1_g[l], ln1_b[l], w_query[l],
                          sub_keys[l], expert_u[l], expert_v[l], ln2_g[l], ln2_b[l])
    return (x[:nb], x[nb:])
```

```python
import functools
import math

import jax
import jax.numpy as jnp
from jax import lax
from jax.experimental import pallas as pl
from jax.experimental.pallas import tpu as pltpu

D_MODEL = 1024
DEPTH = 2
ATT_HEADS = 8
ATT_KV_HEADS = 2
ATT_HEAD_DIM = 64
ATT_WIDTH = ATT_HEADS * ATT_HEAD_DIM
ATT_KV_WIDTH = ATT_KV_HEADS * ATT_HEAD_DIM
WINDOW = 128
BLOCK = 128
N_BUCKETS = 32
MAX_DISTANCE = 128
HYENA_WIDTH = 512
HYENA_ORDER = 2
SHORT_CONV = 3
FILTER_EMB = 33
FILTER_BANDS = (FILTER_EMB - 1) // 2
FILTER_HIDDEN = 64
FILTER_INNER = 2
FAST_DECAY_PCT = 0.3
SLOW_DECAY_PCT = 1.5
DECAY_TARGET = 1e-2
MEM_HEADS = 4
MEM_HEAD_DIM = 128
MEM_WIDTH = MEM_HEADS * MEM_HEAD_DIM
N_BRANCH = 3
BRANCH_WIDTH = 512
PEER_HEADS = 8
N_KEYS = 128
PEER_TOPK = 16
PEER_HALF = 128
PEER_QDIM = 2 * PEER_HALF
O_K = ATT_WIDTH
O_V = O_K + ATT_KV_WIDTH
O_HY = O_V + ATT_KV_WIDTH
O_MQ = O_HY + (HYENA_ORDER + 1) * HYENA_WIDTH
O_GATE = O_MQ + MEM_WIDTH
IN_WIDTH = O_GATE + N_BRANCH * D_MODEL
ALPHA = (2 * DEPTH) ** 0.25
LN_EPS = 1e-5
NEG_INF = -1e30

VMEM_LIMIT_BYTES = 56 * 1024 * 1024


def _linear_kernel(x_ref, w_ref, b_ref, o_ref, *, sigmoid):
    x = x_ref[...].astype(jnp.bfloat16)
    acc = jnp.dot(x, w_ref[...], preferred_element_type=jnp.float32) + b_ref[...]
    o_ref[...] = jax.nn.sigmoid(acc) if sigmoid else acc


def linear(x, w, b, *, tm=512, tn=None, sigmoid=False):
    T, K = x.shape
    N = w.shape[1]
    tn = N if tn is None else tn
    wb = w.astype(jnp.bfloat16)
    return pl.pallas_call(
        functools.partial(_linear_kernel, sigmoid=sigmoid),
        grid=(N // tn, T // tm),
        in_specs=[
            pl.BlockSpec((tm, K), lambda j, i: (i, 0)),
            pl.BlockSpec((K, tn), lambda j, i: (0, j)),
            pl.BlockSpec((1, tn), lambda j, i: (0, j)),
        ],
        out_specs=pl.BlockSpec((tm, tn), lambda j, i: (i, j)),
        out_shape=jax.ShapeDtypeStruct((T, N), jnp.float32),
        compiler_params=pltpu.CompilerParams(
            dimension_semantics=("arbitrary", "arbitrary"),
            vmem_limit_bytes=VMEM_LIMIT_BYTES),
        name="linear",
    )(x, wb, b.reshape(1, N))


IN_PROJ_TM = 256
IN_SEGMENTS = ((0, O_HY), (O_HY, O_MQ), (O_MQ, O_GATE), (O_GATE, IN_WIDTH))


def _in_proj_kernel(x_ref, w_ref, b_ref, qkv_ref, hy_ref, qm_ref, gate_ref):
    x = x_ref[...].astype(jnp.bfloat16)
    for (lo, hi), o_ref in zip(IN_SEGMENTS, (qkv_ref, hy_ref, qm_ref, gate_ref)):
        acc = jnp.dot(x, w_ref[:, lo:hi], preferred_element_type=jnp.float32) + b_ref[:, lo:hi]
        o_ref[...] = jax.nn.sigmoid(acc) if o_ref is gate_ref else acc


def in_projection(x, w_in, b_in):
    T = x.shape[0]
    rows = lambda n: pl.BlockSpec((IN_PROJ_TM, n), lambda i: (i, 0))
    widths = [hi - lo for lo, hi in IN_SEGMENTS]
    return pl.pallas_call(
        _in_proj_kernel,
        grid=(T // IN_PROJ_TM,),
        in_specs=[rows(D_MODEL), pl.BlockSpec((D_MODEL, IN_WIDTH), lambda i: (0, 0)),
                  pl.BlockSpec((1, IN_WIDTH), lambda i: (0, 0))],
        out_specs=[rows(n) for n in widths],
        out_shape=[jax.ShapeDtypeStruct((T, n), jnp.float32) for n in widths],
        compiler_params=pltpu.CompilerParams(dimension_semantics=("arbitrary",), vmem_limit_bytes=VMEM_LIMIT_BYTES),
        name="in_projection",
    )(x, w_in.astype(jnp.bfloat16), b_in.reshape(1, IN_WIDTH))


ATT_TQ = 512
ATT_KEYS = 3 * BLOCK
_NT_DIMS = (((1,), (1,)), ((), ()))


def t5_bucket(rel):
    nb = N_BUCKETS // 2
    max_exact = nb // 2
    ret = jnp.where(rel > 0, nb, 0)
    n = jnp.abs(rel)
    nf = jnp.maximum(n, 1).astype(jnp.float32)
    large = max_exact + (jnp.log(nf / max_exact) / math.log(MAX_DISTANCE / max_exact)
                         * (nb - max_exact)).astype(jnp.int32)
    large = jnp.minimum(large, nb - 1)
    return ret + jnp.where(n < max_exact, n, large)


def window_bias_table(rel_bias):
    rel = (jnp.arange(ATT_KEYS) - BLOCK)[None, :] - jnp.arange(BLOCK)[:, None]
    bias = rel_bias[t5_bucket(rel)].astype(jnp.float32).transpose(2, 0, 1)
    return jnp.where((jnp.abs(rel) <= WINDOW)[None], bias, NEG_INF)


def _col_params():
    return pltpu.CompilerParams(dimension_semantics=("arbitrary", "arbitrary"), vmem_limit_bytes=VMEM_LIMIT_BYTES)


def _window_attn_kernel(sink_ref, q_ref, kp_ref, kc_ref, kn_ref, vp_ref, vc_ref, vn_ref, bias_ref, o_ref, *, n_steps):
    i = pl.program_id(1)
    bf = jnp.bfloat16
    q = q_ref[0].astype(bf)
    k_all = jnp.concatenate([kp_ref[0], kc_ref[0], kn_ref[0]], axis=0).astype(bf)
    v_all = jnp.concatenate([vp_ref[0], vc_ref[0], vn_ref[0]], axis=0).astype(bf)
    col = lax.broadcasted_iota(jnp.int32, (BLOCK, ATT_KEYS), 1)
    n_blk = ATT_TQ // BLOCK
    scale = ATT_HEAD_DIM ** -0.5
    group = ATT_HEADS // ATT_KV_HEADS
    for j in range(n_blk):
        kj = k_all[j * BLOCK:j * BLOCK + ATT_KEYS]
        vj = v_all[j * BLOCK:j * BLOCK + ATT_KEYS]
        off_seq = None
        if j == 0:
            off_seq = (i == 0) & (col < BLOCK)
        if j == n_blk - 1:
            last = (i == n_steps - 1) & (col >= 2 * BLOCK)
            off_seq = last if off_seq is None else off_seq | last
        for h in range(ATT_HEADS):
            hk = h // group
            qh = q[j * BLOCK:(j + 1) * BLOCK, h * ATT_HEAD_DIM:(h + 1) * ATT_HEAD_DIM]
            kh = kj[:, hk * ATT_HEAD_DIM:(hk + 1) * ATT_HEAD_DIM]
            s = lax.dot_general(qh, kh, _NT_DIMS, preferred_element_type=jnp.float32) * scale + bias_ref[h]
            if off_seq is not None:
                s = jnp.where(off_seq, NEG_INF, s)
            sink = sink_ref[h]
            m = jnp.maximum(jnp.max(s, axis=-1, keepdims=True), sink)
            p = jnp.exp(s - m)
            denom = jnp.sum(p, axis=-1, keepdims=True) + jnp.exp(sink - m)
            oh = jnp.dot((p / denom).astype(bf), vj[:, hk * ATT_HEAD_DIM:(hk + 1) * ATT_HEAD_DIM],
                         preferred_element_type=jnp.float32)
            o_ref[0, j * BLOCK:(j + 1) * BLOCK, h * ATT_HEAD_DIM:(h + 1) * ATT_HEAD_DIM] = oh.astype(o_ref.dtype)


def window_attention_pallas(qkv, bias, sink):
    B, S, _ = qkv.shape
    n_steps = S // ATT_TQ
    per = ATT_TQ // BLOCK
    last_blk = S // BLOCK - 1
    k_col, v_col = O_K // ATT_KV_WIDTH, O_V // ATT_KV_WIDTH

    def edge(col, nxt):
        if nxt:
            return pl.BlockSpec((1, BLOCK, ATT_KV_WIDTH), lambda b, i: (b, jnp.minimum((i + 1) * per, last_blk), col))
        return pl.BlockSpec((1, BLOCK, ATT_KV_WIDTH), lambda b, i: (b, jnp.maximum(i * per - 1, 0), col))

    cur = lambda col: pl.BlockSpec((1, ATT_TQ, ATT_KV_WIDTH), lambda b, i: (b, i, col))
    return pl.pallas_call(
        functools.partial(_window_attn_kernel, n_steps=n_steps),
        grid=(B, n_steps),
        in_specs=[pl.BlockSpec(memory_space=pltpu.SMEM),
                  pl.BlockSpec((1, ATT_TQ, ATT_WIDTH), lambda b, i: (b, i, 0)),
                  edge(k_col, False), cur(k_col), edge(k_col, True),
                  edge(v_col, False), cur(v_col), edge(v_col, True),
                  pl.BlockSpec(bias.shape, lambda b, i: (0, 0, 0))],
        out_specs=pl.BlockSpec((1, ATT_TQ, ATT_WIDTH), lambda b, i: (b, i, 0)),
        out_shape=jax.ShapeDtypeStruct((B, S, ATT_WIDTH), jnp.bfloat16),
        compiler_params=_col_params(),
        name="window_attn",
    )(sink.astype(jnp.float32), qkv, qkv, qkv, qkv, qkv, qkv, qkv, bias)


MEM_TQ = 512


def _mem_attn_kernel(q_ref, kv_ref, o_ref):
    bf = jnp.bfloat16
    q = q_ref[0].astype(bf)
    kv = kv_ref[0].astype(bf)
    scale = MEM_HEAD_DIM ** -0.5
    for h in range(MEM_HEADS):
        sl = slice(h * MEM_HEAD_DIM, (h + 1) * MEM_HEAD_DIM)
        s = lax.dot_general(q[:, sl], kv[:, sl], _NT_DIMS, preferred_element_type=jnp.float32) * scale
        p = jnp.exp(s - jnp.max(s, axis=-1, keepdims=True))
        p = (p / jnp.sum(p, axis=-1, keepdims=True)).astype(bf)
        vh = kv[:, MEM_WIDTH + h * MEM_HEAD_DIM:MEM_WIDTH + (h + 1) * MEM_HEAD_DIM]
        o_ref[0, :, sl] = jnp.dot(p, vh, preferred_element_type=jnp.float32).astype(o_ref.dtype)


def memory_attention_pallas(q, kv):
    B, S, _ = q.shape
    M = kv.shape[1]
    return pl.pallas_call(
        _mem_attn_kernel,
        grid=(B, S // MEM_TQ),
        in_specs=[pl.BlockSpec((1, MEM_TQ, MEM_WIDTH), lambda b, i: (b, i, 0)),
                  pl.BlockSpec((1, M, 2 * MEM_WIDTH), lambda b, i: (b, 0, 0))],
        out_specs=pl.BlockSpec((1, MEM_TQ, MEM_WIDTH), lambda b, i: (b, i, 0)),
        out_shape=jax.ShapeDtypeStruct((B, S, MEM_WIDTH), jnp.bfloat16),
        compiler_params=_col_params(),
        name="mem_attn",
    )(q, kv)


MERGE_TM = 256


def _layer_norm_rows(y, g, b):
    mu = jnp.mean(y, axis=-1, keepdims=True)
    d = y - mu
    var = jnp.mean(d * d, axis=-1, keepdims=True)
    return d * lax.rsqrt(var + LN_EPS) * g + b


def _merge_kernel(a_ref, h_ref, m_ref, g_ref, x_ref, wb_ref, wo_ref, lg_ref, lb_ref, o_ref):
    bf = jnp.bfloat16
    f32 = jnp.float32
    merged = g_ref[:, 0:D_MODEL] * jnp.dot(a_ref[...].astype(bf), wb_ref[0], preferred_element_type=f32)
    merged = merged + g_ref[:, D_MODEL:2 * D_MODEL] * jnp.dot(h_ref[...].astype(bf), wb_ref[1], preferred_element_type=f32)
    merged = merged + g_ref[:, 2 * D_MODEL:] * jnp.dot(m_ref[...].astype(bf), wb_ref[2], preferred_element_type=f32)
    y = ALPHA * x_ref[...] + jnp.dot(merged.astype(bf), wo_ref[...], preferred_element_type=f32)
    o_ref[...] = _layer_norm_rows(y, lg_ref[...], lb_ref[...])


def merge_norm(att, hy, mem, gates, x, w_branch, w_out, ln_g, ln_b):
    T = x.shape[0]
    bf = jnp.bfloat16
    rows = lambda w: pl.BlockSpec((MERGE_TM, w), lambda i: (i, 0))
    whole = lambda a: pl.BlockSpec(a.shape, lambda i: (0,) * a.ndim)
    wb, wo = w_branch.astype(bf), w_out.astype(bf)
    lg, lb = ln_g.reshape(1, D_MODEL), ln_b.reshape(1, D_MODEL)
    return pl.pallas_call(
        _merge_kernel,
        grid=(T // MERGE_TM,),
        in_specs=[rows(BRANCH_WIDTH), rows(BRANCH_WIDTH), rows(BRANCH_WIDTH), rows(N_BRANCH * D_MODEL), rows(D_MODEL),
                  whole(wb), whole(wo), whole(lg), whole(lb)],
        out_specs=rows(D_MODEL),
        out_shape=jax.ShapeDtypeStruct((T, D_MODEL), jnp.float32),
        compiler_params=pltpu.CompilerParams(dimension_semantics=("arbitrary",), vmem_limit_bytes=VMEM_LIMIT_BYTES),
        name="merge_norm",
    )(att, hy, mem, gates, x, wb, wo, lg, lb)


def _residual_norm_kernel(x_ref, r_ref, lg_ref, lb_ref, o_ref):
    o_ref[...] = _layer_norm_rows(ALPHA * x_ref[...] + r_ref[...], lg_ref[...], lb_ref[...])


def residual_norm(x, r, ln_g, ln_b):
    T = x.shape[0]
    tm = 512
    rows = pl.BlockSpec((tm, D_MODEL), lambda i: (i, 0))
    one = pl.BlockSpec((1, D_MODEL), lambda i: (0, 0))
    return pl.pallas_call(
        _residual_norm_kernel,
        grid=(T // tm,),
        in_specs=[rows, rows, one, one],
        out_specs=rows,
        out_shape=jax.ShapeDtypeStruct((T, D_MODEL), jnp.float32),
        compiler_params=pltpu.CompilerParams(dimension_semantics=("arbitrary",), vmem_limit_bytes=VMEM_LIMIT_BYTES),
        name="residual_norm",
    )(x, r, ln_g.reshape(1, D_MODEL), ln_b.reshape(1, D_MODEL))


FFT_N1 = 256
FFT_N2 = 128
FFT_N = FFT_N1 * FFT_N2
HY_COLS = FFT_N2 * HYENA_WIDTH
LMUL_TN = 2048
KB_K1 = 8
FILT_TM = 512


def _dft_constants():
    f32 = jnp.float32
    n1 = jnp.arange(FFT_N1, dtype=jnp.int32)
    ang1 = ((n1[:, None] * n1[None, :]) % FFT_N1).astype(f32) * f32(2.0 * math.pi / FFT_N1)
    fr, fi = jnp.cos(ang1), -jnp.sin(ang1)
    h = FFT_N1 // 2
    f_fwd = jnp.concatenate([jnp.concatenate([fr[:, :h], -fi[:, :h]], axis=1),
                             jnp.concatenate([fi[:, :h], fr[:, :h]], axis=1)], axis=0)
    f_inv = f_fwd.T * f32(1.0 / FFT_N)
    f_real = jnp.concatenate([fr, fi], axis=0)
    k2 = jnp.arange(FFT_N2, dtype=jnp.int32)
    k = n1[:, None, None] + FFT_N1 * k2[None, :, None]
    ang2 = ((k * k2[None, None, :]) % FFT_N).astype(f32) * f32(2.0 * math.pi / FFT_N)
    gr, gi = jnp.cos(ang2), -jnp.sin(ang2)
    g = jnp.concatenate([jnp.concatenate([gr, -gi], axis=2), jnp.concatenate([gi, gr], axis=2)], axis=1)
    bf = jnp.bfloat16
    return f_fwd.astype(bf), f_inv.astype(bf), f_real.astype(bf), g.astype(bf), g.transpose(0, 2, 1).astype(bf)


def _lmul_pair_kernel(l_ref, xr_ref, xi_ref, o_ref):
    x = jnp.concatenate([xr_ref[0], xi_ref[0]], axis=0).astype(jnp.bfloat16)
    o_ref[0] = jnp.dot(l_ref[...], x, preferred_element_type=jnp.float32).astype(o_ref.dtype)


def _lmul_kernel(l_ref, x_ref, o_ref):
    o_ref[0] = jnp.dot(l_ref[...], x_ref[0].astype(jnp.bfloat16),
                       preferred_element_type=jnp.float32).astype(o_ref.dtype)


def _lmul_gate_kernel(l_ref, a_ref, zr_ref, zi_ref, pr_ref, pi_ref, b_ref, o_ref):
    y = jnp.dot(l_ref[...], a_ref[0], preferred_element_type=jnp.float32)
    h = FFT_N1 // 2
    o_ref[0, :h] = pr_ref[0] * (y[:h] + zr_ref[0] * b_ref[...])
    o_ref[0, h:] = pi_ref[0] * (y[h:] + zi_ref[0] * b_ref[...])


def dft_stage1_pairs(f_fwd, z):
    nb, h, cols = z.shape
    npair = (nb + 1) // 2
    last = nb - 1
    return pl.pallas_call(
        _lmul_pair_kernel,
        grid=(npair, cols // LMUL_TN),
        in_specs=[
            pl.BlockSpec(f_fwd.shape, lambda p, j: (0, 0)),
            pl.BlockSpec((1, h, LMUL_TN), lambda p, j: (jnp.minimum(2 * p, last), 0, j)),
            pl.BlockSpec((1, h, LMUL_TN), lambda p, j: (jnp.minimum(2 * p + 1, last), 0, j)),
        ],
        out_specs=pl.BlockSpec((1, 2 * FFT_N1, LMUL_TN), lambda p, j: (p, 0, j)),
        out_shape=jax.ShapeDtypeStruct((npair, 2 * FFT_N1, cols), jnp.bfloat16),
        compiler_params=_col_params(),
        name="dft_stage1",
    )(f_fwd, z, z)


def dft_stage1_real(f_real, x):
    _, h, cols = x.shape
    return pl.pallas_call(
        _lmul_kernel,
        grid=(1, cols // LMUL_TN),
        in_specs=[pl.BlockSpec(f_real.shape, lambda p, j: (0, 0)),
                  pl.BlockSpec((1, h, LMUL_TN), lambda p, j: (p, 0, j))],
        out_specs=pl.BlockSpec((1, 2 * FFT_N1, LMUL_TN), lambda p, j: (p, 0, j)),
        out_shape=jax.ShapeDtypeStruct((1, 2 * FFT_N1, cols), jnp.bfloat16),
        compiler_params=_col_params(),
        name="dft_stage1_real",
    )(f_real, x)


def idft_stage1_gate(f_inv, a, z, p, bias_row):
    npair, _, cols = a.shape
    half = FFT_N1 // 2

    def half_blk(x, r):
        last = x.shape[0] - 1
        return pl.BlockSpec((1, half, LMUL_TN), lambda q, j: (jnp.minimum(2 * q + r, last), 0, j))

    out = pl.pallas_call(
        _lmul_gate_kernel,
        grid=(npair, cols // LMUL_TN),
        in_specs=[pl.BlockSpec(f_inv.shape, lambda q, j: (0, 0)),
                  pl.BlockSpec((1, 2 * FFT_N1, LMUL_TN), lambda q, j: (q, 0, j)),
                  half_blk(z, 0), half_blk(z, 1), half_blk(p, 0), half_blk(p, 1),
                  pl.BlockSpec((1, LMUL_TN), lambda q, j: (0, 0))],
        out_specs=pl.BlockSpec((1, FFT_N1, LMUL_TN), lambda q, j: (q, 0, j)),
        out_shape=jax.ShapeDtypeStruct((npair, FFT_N1, cols), jnp.float32),
        compiler_params=_col_params(),
        name="idft_stage1_gate",
    )(f_inv, a, z, z, p, p, bias_row)
    return out.reshape(2 * npair, half, cols)


def _stage2_conv_kernel(a_ref, g_ref, gt_ref, h_ref, o_ref):
    for j in range(KB_K1):
        x = jnp.concatenate([a_ref[0, 0, j], a_ref[0, 1, j]], axis=0)
        z = jnp.dot(g_ref[j], x, preferred_element_type=jnp.float32)
        zr, zi = z[:FFT_N2], z[FFT_N2:]
        hr, hi = h_ref[0, j], h_ref[1, j]
        w = jnp.concatenate([zr * hr - zi * hi, zr * hi + zi * hr], axis=0).astype(jnp.bfloat16)
        y = jnp.dot(gt_ref[j], w, preferred_element_type=jnp.float32)
        o_ref[0, 0, j] = y[:FFT_N2].astype(o_ref.dtype)
        o_ref[0, 1, j] = y[FFT_N2:].astype(o_ref.dtype)


def stage2_conv(a, g, gt, hf, order):
    npair = a.shape[0]
    C = HYENA_WIDTH
    ablk = pl.BlockSpec((1, 2, KB_K1, FFT_N2, C), lambda p, i: (p, 0, i, 0, 0))
    gblk = pl.BlockSpec((KB_K1, 2 * FFT_N2, 2 * FFT_N2), lambda p, i: (i, 0, 0))
    return pl.pallas_call(
        _stage2_conv_kernel,
        grid=(npair, FFT_N1 // KB_K1),
        in_specs=[ablk, gblk, gblk,
                  pl.BlockSpec((2, KB_K1, FFT_N2, C), lambda p, i: (0, i, 0, order))],
        out_specs=ablk,
        out_shape=jax.ShapeDtypeStruct(a.shape, a.dtype),
        compiler_params=_col_params(),
        name="stage2_conv",
    )(a, g, gt, hf)


def _stage2_filter_kernel(a_ref, g_ref, s_ref, o_ref):
    for j in range(KB_K1):
        x = jnp.concatenate([a_ref[0, j], a_ref[1, j]], axis=0)
        z = jnp.dot(g_ref[j], x, preferred_element_type=jnp.float32) * s_ref[...]
        o_ref[0, j] = z[:FFT_N2]
        o_ref[1, j] = z[FFT_N2:]


def stage2_filter(a, g, inv_norm):
    W = a.shape[-1]
    blk = pl.BlockSpec((2, KB_K1, FFT_N2, W), lambda i: (0, i, 0, 0))
    return pl.pallas_call(
        _stage2_filter_kernel,
        grid=(FFT_N1 // KB_K1,),
        in_specs=[blk, pl.BlockSpec((KB_K1, 2 * FFT_N2, 2 * FFT_N2), lambda i: (i, 0, 0)),
                  pl.BlockSpec((1, W), lambda i: (0, 0))],
        out_specs=blk,
        out_shape=jax.ShapeDtypeStruct(a.shape, jnp.float32),
        compiler_params=pltpu.CompilerParams(dimension_semantics=("arbitrary",), vmem_limit_bytes=VMEM_LIMIT_BYTES),
        name="stage2_filter",
    )(a, g, inv_norm)


def _filter_gen_kernel(bands_ref, w0_ref, wc_ref, ws_ref, wmid_ref, fb_ref, ff_ref, wdir_ref, wbwd_ref,
                       adelta_ref, two_ref, norm_ref, *, seq):
    i = pl.program_id(0)
    bf = jnp.bfloat16
    row = i * FILT_TM + lax.broadcasted_iota(jnp.int32, (FILT_TM, 1), 0)
    pos_i = jnp.where(row < seq, row, 2 * seq - row)
    pos = pos_i.astype(jnp.float32)
    t = pos / max(seq - 1, 1)
    w = (jnp.float32(2.0 * math.pi) * pos) / seq
    ang = w * bands_ref[...]
    pre = (jnp.dot(jnp.cos(ang).astype(bf), wc_ref[...], preferred_element_type=jnp.float32)
           + jnp.dot((-jnp.sin(ang)).astype(bf), ws_ref[...], preferred_element_type=jnp.float32)
           + t.astype(bf).astype(jnp.float32) * w0_ref[...])
    h = jnp.sin(ff_ref[0:1, :] * (pre + fb_ref[0:1, :]))
    for m in range(FILTER_INNER):
        pre = jnp.dot(h.astype(bf), wmid_ref[m], preferred_element_type=jnp.float32)
        h = jnp.sin(ff_ref[m + 1:m + 2, :] * (pre + fb_ref[m + 1:m + 2, :]))
    hb = h.astype(bf)
    decay = jnp.exp(-t * adelta_ref[...])
    out = jnp.dot(hb, wdir_ref[0], preferred_element_type=jnp.float32) * decay
    out = jnp.where(pos_i < seq, out, 0.0)

    @pl.when(i == 0)
    def _():
        extra = jnp.dot(hb, wbwd_ref[...], preferred_element_type=jnp.float32) * decay
        first = jnp.where(row == 0, out + extra, out)
        two_ref[...] = first
        norm_ref[...] = jnp.sum(jnp.abs(first), axis=0, keepdims=True)

    @pl.when(i != 0)
    def _():
        two_ref[...] = out
        norm_ref[...] += jnp.sum(jnp.abs(out), axis=0, keepdims=True)


def hyena_filter_rows(seq, f_w_in, f_w_mid, f_b, f_freq, f_w_out):
    f32, bf = jnp.float32, jnp.bfloat16
    C, H = HYENA_WIDTH, FILTER_HIDDEN
    bands = jnp.linspace(1e-4, FILTER_BANDS - 1, FILTER_BANDS, dtype=f32)
    bands = jnp.pad(bands, (0, VREG_LANES - FILTER_BANDS)).reshape(1, VREG_LANES)
    w_in = f_w_in.astype(bf)
    w0 = w_in[0:1].astype(f32)
    pad_rows = ((0, VREG_LANES - FILTER_BANDS), (0, 0))
    wc = jnp.pad(w_in[1:1 + FILTER_BANDS], pad_rows)
    ws = jnp.pad(w_in[1 + FILTER_BANDS:], pad_rows)
    w_out = f_w_out.astype(bf).reshape(H, HYENA_ORDER, 2, C)
    wdir = jnp.stack([w_out[:, :, 0].reshape(H, HYENA_ORDER * C), w_out[:, :, 1].reshape(H, HYENA_ORDER * C)])
    max_decay = math.log(DECAY_TARGET) / FAST_DECAY_PCT
    min_decay = math.log(DECAY_TARGET) / SLOW_DECAY_PCT
    adelta = jnp.abs(jnp.linspace(min_decay, max_decay, C, dtype=f32))
    adelta = jnp.tile(adelta, HYENA_ORDER).reshape(1, HYENA_ORDER * C)
    n_tiles = 2 * seq // FILT_TM
    half_tiles = seq // FILT_TM
    whole = lambda a: pl.BlockSpec(a.shape, lambda i: (0,) * a.ndim)
    wmid = f_w_mid.astype(bf)
    fb, ff = f_b.astype(f32), f_freq.astype(f32)
    return pl.pallas_call(
        functools.partial(_filter_gen_kernel, seq=seq),
        grid=(n_tiles,),
        in_specs=[whole(bands), whole(w0), whole(wc), whole(ws), whole(wmid), whole(fb), whole(ff),
                  pl.BlockSpec((1, H, HYENA_ORDER * C), lambda i: (i // half_tiles, 0, 0)),
                  pl.BlockSpec((None, H, HYENA_ORDER * C), lambda i: (1, 0, 0)),
                  whole(adelta)],
        out_specs=[pl.BlockSpec((FILT_TM, HYENA_ORDER * C), lambda i: (i, 0)),
                   pl.BlockSpec((1, HYENA_ORDER * C), lambda i: (0, 0))],
        out_shape=[jax.ShapeDtypeStruct((2 * seq, HYENA_ORDER * C), f32),
                   jax.ShapeDtypeStruct((1, HYENA_ORDER * C), f32)],
        compiler_params=pltpu.CompilerParams(dimension_semantics=("arbitrary",), vmem_limit_bytes=VMEM_LIMIT_BYTES),
        name="hyena_filter_gen",
    )(bands, w0, wc, ws, wmid, fb, ff, wdir, wdir, adelta)


def _short_conv_kernel(x_ref, w_ref, b_ref, o_ref):
    x = x_ref[0]
    n = x.shape[0]
    t = lax.broadcasted_iota(jnp.int32, x.shape, 0)
    prev = jnp.where(t == 0, 0.0, pltpu.roll(x, 1, 0))
    nxt = jnp.where(t == n - 1, 0.0, pltpu.roll(x, n - 1, 0))
    o_ref[0] = ((b_ref[...] + prev * w_ref[0:1, :]) + x * w_ref[1:2, :]) + nxt * w_ref[2:3, :]


def short_conv_pallas(u, w, b):
    B, L, W = u.shape
    tc = VREG_LANES
    return pl.pallas_call(
        _short_conv_kernel,
        grid=(B, W // tc),
        in_specs=[pl.BlockSpec((1, L, tc), lambda i, j: (i, 0, j)),
                  pl.BlockSpec((SHORT_CONV, tc), lambda i, j: (0, j)),
                  pl.BlockSpec((1, tc), lambda i, j: (0, j))],
        out_specs=pl.BlockSpec((1, L, tc), lambda i, j: (i, 0, j)),
        out_shape=jax.ShapeDtypeStruct((B, L, W), jnp.float32),
        compiler_params=_col_params(),
        name="short_conv",
    )(u, w, b.reshape(1, W))


def hyena_mixer_pallas(u, conv_w, conv_b, f_w_in, f_w_mid, f_b, f_freq, f_w_out, hyena_bias):
    B, L, _ = u.shape
    assert 2 * L == FFT_N
    C = HYENA_WIDTH
    f_fwd, f_inv, f_real, g, gt = _dft_constants()
    two, norm = hyena_filter_rows(L, f_w_in, f_w_mid, f_b, f_freq, f_w_out)
    af = dft_stage1_real(f_real, two.reshape(1, FFT_N1, FFT_N2 * HYENA_ORDER * C))
    af = af.reshape(2, FFT_N1, FFT_N2, HYENA_ORDER * C)
    hf = stage2_filter(af, g, 1.0 / norm)
    uc = short_conv_pallas(u, conv_w, conv_b)
    half = FFT_N1 // 2
    parts = [uc[..., o * C:(o + 1) * C].reshape(B, half, HY_COLS) for o in range(HYENA_ORDER + 1)]
    z = parts[0]
    for o in range(HYENA_ORDER):
        a = dft_stage1_pairs(f_fwd, z)
        npair = a.shape[0]
        a = stage2_conv(a.reshape(npair, 2, FFT_N1, FFT_N2, C), g, gt, hf, o)
        bias_row = jnp.tile(hyena_bias[o], LMUL_TN // C).reshape(1, LMUL_TN)
        z = idft_stage1_gate(f_inv, a.reshape(npair, 2 * FFT_N1, HY_COLS), z, parts[o + 1], bias_row)
    return z[:B].reshape(B, L, C)


PEER_SEL = PEER_HEADS * PEER_TOPK
PEER_TB = 128
VREG_SUBLANES = 8
VREG_LANES = 128
EXPERT_ROWS = 4
TABLE_LEAD_ROWS = 8
TILE_ROWS = 16
V_UNROLL = 16
IDX_SPLIT = 16
IDX_PER = PEER_SEL // IDX_SPLIT


def _row_reader(idx_refs, t):
    return lambda k: idx_refs[k // IDX_PER][0, k % IDX_PER, t]


def pack_expert_table(tab):
    e = tab.shape[0]
    b = lax.bitcast_convert_type(tab.astype(jnp.bfloat16), jnp.uint16).astype(jnp.uint32)
    b = b.reshape(e, 2, EXPERT_ROWS, VREG_LANES)
    w = (b[:, 0] | (b[:, 1] << 16)).reshape(e * EXPERT_ROWS, VREG_LANES)
    return jnp.pad(w, ((TABLE_LEAD_ROWS, VREG_SUBLANES), (0, 0)))


def from_tile_rows(y):
    T = y.shape[0]
    return y.reshape(T, EXPERT_ROWS, 2, VREG_LANES).transpose(0, 2, 1, 3).reshape(T, D_MODEL)


def _split_bf16(v):
    hi = v.astype(jnp.bfloat16)
    return hi, (v - hi.astype(jnp.float32)).astype(jnp.bfloat16)


def _gelu_exact(x):
    return 0.5 * x * (1.0 + lax.erf(x * (2.0 ** -0.5)))


_COMBINE_POS = (3, 7, 1, 5, 2, 6, 0, 4)


def _sublane_sums(c, sub):
    mv = (sub & 2) != 0
    e = []
    for c1, c2 in ((c[0], c[1]), (c[2], c[3])):
        e.append(jnp.where(mv, c1 + pltpu.roll(c1, 2, 0), c2 + pltpu.roll(c2, 6, 0)))
    mo = (sub & 1) != 0
    return jnp.where(mo, e[0] + pltpu.roll(e[0], 1, 0), e[1] + pltpu.roll(e[1], 7, 0))


def _peer_u_kernel(*refs):
    idx_refs = refs[:IDX_SPLIT]
    x_ref, g_ref, tab_ref, o_ref, act_ref, r_ref = refs[IDX_SPLIT:]
    sub = lax.broadcasted_iota(jnp.int32, (VREG_SUBLANES, VREG_LANES), 0)
    lo4 = sub < EXPERT_ROWS

    def token(t, carry):
        sel_row = _row_reader(idx_refs, t)
        x8 = x_ref[t]
        xr = pltpu.roll(x8, EXPERT_ROWS, 0)
        x_lo = jnp.where(lo4, x8, xr)
        x_hi = jnp.where(lo4, xr, x8)
        rs = []
        for grp in range(PEER_SEL // VREG_SUBLANES):
            pairs = []
            for i in range(VREG_SUBLANES // 2):
                wa = tab_ref[pl.ds(sel_row(grp * VREG_SUBLANES + _COMBINE_POS[2 * i]), VREG_SUBLANES), :]
                wb = tab_ref[pl.ds(sel_row(grp * VREG_SUBLANES + _COMBINE_POS[2 * i + 1]) - EXPERT_ROWS,
                                   VREG_SUBLANES), :]
                w = jnp.where(lo4, wa, wb)
                lo = lax.bitcast_convert_type(w << 16, jnp.float32)
                hi = lax.bitcast_convert_type(w & jnp.uint32(0xFFFF0000), jnp.float32)
                pairs.append(lo * x_lo + hi * x_hi)
            rs.append(_sublane_sums(pairs, sub))
        r0 = pl.multiple_of(t * PEER_SEL, PEER_SEL)
        r_ref[pl.ds(r0, PEER_SEL), :] = jnp.concatenate(rs, axis=0).astype(jnp.bfloat16)
        return carry

    lax.fori_loop(0, PEER_TB, token, 0)
    ones = jnp.ones((VREG_LANES, VREG_LANES), jnp.bfloat16)
    keep = (lax.broadcasted_iota(jnp.int32, (PEER_SEL, VREG_LANES), 0)
            == lax.broadcasted_iota(jnp.int32, (PEER_SEL, VREG_LANES), 1))
    chunk = VREG_SUBLANES * PEER_SEL
    for c in range(PEER_TB // VREG_SUBLANES):
        s = jnp.dot(r_ref[c * chunk:(c + 1) * chunk, :], ones, preferred_element_type=jnp.float32)
        for j in range(VREG_SUBLANES):
            blk = jnp.where(keep, s[j * PEER_SEL:(j + 1) * PEER_SEL], 0.0)
            act_ref[c * VREG_SUBLANES + j:c * VREG_SUBLANES + j + 1, :] = jnp.sum(blk, axis=0, keepdims=True)
    o_ref[...] = g_ref[...] * _gelu_exact(act_ref[...])


def _peer_v_kernel(*refs):
    idx_refs = refs[:IDX_SPLIT]
    coef_ref, e_ref, d_ref, tab_ref, o_ref, chi_ref, clo_ref = refs[IDX_SPLIT:]
    c_hi, c_lo = _split_bf16(coef_ref[...])
    chi_ref[...] = jnp.dot(c_hi, e_ref[...], preferred_element_type=jnp.float32)
    clo_ref[...] = jnp.dot(c_lo, e_ref[...], preferred_element_type=jnp.float32)

    def token(t, carry):
        sel_row = _row_reader(idx_refs, t)
        cmat = jnp.concatenate([chi_ref[pl.ds(t, 1), :] * d_ref[...], clo_ref[pl.ds(t, 1), :] * d_ref[...]],
                               axis=0).astype(jnp.bfloat16)
        w = jnp.concatenate([pltpu.bitcast(tab_ref[pl.ds(sel_row(k), VREG_SUBLANES), :], jnp.bfloat16)
                             for k in range(PEER_SEL)], axis=0)
        acc = jnp.dot(cmat, w, preferred_element_type=jnp.float32)
        o_ref[t] = acc[:VREG_SUBLANES] + acc[VREG_SUBLANES:]
        return carry

    lax.fori_loop(0, PEER_TB, token, 0, unroll=V_UNROLL)


def peer_experts(x, eidx, g, tab_u, tab_v):
    T = x.shape[0]
    grid = (T // PEER_TB,)
    f32, bf = jnp.float32, jnp.bfloat16
    sel = jnp.arange(PEER_SEL, dtype=jnp.int32)
    col = jnp.arange(PEER_SEL * TILE_ROWS, dtype=jnp.int32)
    expand = (col[None, :] // TILE_ROWS == sel[:, None]).astype(bf)
    diag = (col[None, :] % TILE_ROWS == jnp.arange(VREG_SUBLANES, dtype=jnp.int32)[:, None]).astype(f32)
    assert eidx.shape == (T // PEER_TB, PEER_SEL, PEER_TB)
    idx_specs = [pl.BlockSpec((1, IDX_PER, PEER_TB), lambda i, p=p: (i, p, 0), memory_space=pltpu.SMEM)
                 for p in range(IDX_SPLIT)]
    idx_parts = [eidx] * IDX_SPLIT
    vec_blk = pl.BlockSpec((PEER_TB, PEER_SEL), lambda i: (i, 0))
    row_blk = pl.BlockSpec((PEER_TB, VREG_SUBLANES, VREG_LANES), lambda i: (i, 0, 0))
    whole = lambda a: pl.BlockSpec(a.shape, lambda i: (0,) * a.ndim)
    tab_spec = pl.BlockSpec(memory_space=pltpu.VMEM)
    params = pltpu.CompilerParams(dimension_semantics=("arbitrary",), vmem_limit_bytes=VMEM_LIMIT_BYTES)
    coef = pl.pallas_call(
        _peer_u_kernel,
        grid=grid,
        in_specs=idx_specs + [row_blk, vec_blk, tab_spec],
        out_specs=vec_blk,
        out_shape=jax.ShapeDtypeStruct((T, PEER_SEL), f32),
        scratch_shapes=[pltpu.VMEM((PEER_TB, PEER_SEL), f32),
                        pltpu.VMEM((PEER_TB * PEER_SEL, VREG_LANES), bf)],
        compiler_params=params,
        name="peer_u",
    )(*idx_parts, x.reshape(T, VREG_SUBLANES, VREG_LANES), g, tab_u)
    out = pl.pallas_call(
        _peer_v_kernel,
        grid=grid,
        in_specs=idx_specs + [vec_blk, whole(expand), whole(diag), tab_spec],
        out_specs=row_blk,
        out_shape=jax.ShapeDtypeStruct((T, VREG_SUBLANES, VREG_LANES), f32),
        scratch_shapes=[pltpu.VMEM((PEER_TB, PEER_SEL * TILE_ROWS), f32),
                        pltpu.VMEM((PEER_TB, PEER_SEL * TILE_ROWS), f32)],
        compiler_params=params,
        name="peer_v",
    )(*idx_parts, coef, expand, diag, tab_v)
    return from_tile_rows(out)


ROUTE_TM = 512


def _top16_rows(s, key_id):
    row16 = lax.broadcasted_iota(jnp.int32, (PEER_TOPK, VREG_LANES), 0)
    vals = jnp.zeros((PEER_TOPK, VREG_LANES), jnp.float32)
    ids = jnp.zeros((PEER_TOPK, VREG_LANES), jnp.float32)
    big = jnp.float32(2 ** 30)
    for j in range(PEER_TOPK):
        m = jnp.max(s, axis=0, keepdims=True)
        am = jnp.min(jnp.where(s == m, key_id, big), axis=0, keepdims=True)
        vals = jnp.where(row16 == j, m, vals)
        ids = jnp.where(row16 == j, am, ids)
        if j + 1 < PEER_TOPK:
            s = jnp.where(key_id == am, -jnp.inf, s)
    return vals, ids


_PAIR_GROUPS = ((0, 0, 8), (0, 8, 8), (1, 0, 8), (2, 0, 5), (3, 0, 4), (4, 0, 3), (5, 0, 2), (6, 0, 2), (7, 0, 2))


def _route_head(s0, i0, s1, i1):
    sub_i = lax.broadcasted_iota(jnp.int32, (VREG_SUBLANES, VREG_LANES), 0)
    sub = sub_i.astype(jnp.float32)
    cands, flat, eids = [], [], []
    for a, b0, nb in _PAIR_GROUPS:
        c = s0[a:a + 1] + s1[b0:b0 + VREG_SUBLANES]
        cands.append(jnp.where(sub_i < nb, c, -jnp.inf) if nb < VREG_SUBLANES else c)
        flat.append(a * PEER_TOPK + b0 + sub)
        eids.append(i0[a:a + 1] * N_KEYS + i1[b0:b0 + VREG_SUBLANES])
    cands.append(s0[VREG_SUBLANES:] + s1[0:1])
    flat.append((sub + VREG_SUBLANES) * PEER_TOPK)
    eids.append(i0[VREG_SUBLANES:] * N_KEYS + i1[0:1])
    cand = jnp.concatenate(cands, axis=0)
    flat = jnp.concatenate(flat, axis=0)
    eid = jnp.concatenate(eids, axis=0)
    row16 = lax.broadcasted_iota(jnp.int32, (PEER_TOPK, VREG_LANES), 0)
    sc = jnp.zeros((PEER_TOPK, VREG_LANES), jnp.float32)
    sel = jnp.zeros((PEER_TOPK, VREG_LANES), jnp.float32)
    big = jnp.float32(2 ** 30)
    for j in range(PEER_TOPK):
        m = jnp.max(cand, axis=0, keepdims=True)
        am = jnp.min(jnp.where(cand == m, flat, big), axis=0, keepdims=True)
        hit = flat == am
        e = jnp.max(jnp.where(hit, eid, -1.0), axis=0, keepdims=True)
        sc = jnp.where(row16 == j, m, sc)
        sel = jnp.where(row16 == j, e, sel)
        if j + 1 < PEER_TOPK:
            cand = jnp.where(hit, -jnp.inf, cand)
    p = jnp.exp(sc - sc[0:1])
    return sel, p / jnp.sum(p, axis=0, keepdims=True)


def _route_kernel(x_ref, wq_ref, sk_ref, rows_ref, g_ref, q_ref):
    q = jnp.dot(x_ref[...].astype(jnp.bfloat16), wq_ref[...], preferred_element_type=jnp.float32)
    q_ref[...] = q.astype(jnp.bfloat16)
    key_id = lax.broadcasted_iota(jnp.int32, (N_KEYS, VREG_LANES), 0).astype(jnp.float32)

    def head(h, carry):
        tops = []
        for c in range(2):
            hc = h * 2 + c
            qhc = q_ref[:, pl.ds(pl.multiple_of(hc * PEER_HALF, PEER_HALF), PEER_HALF)]
            s = lax.dot_general(sk_ref[hc], qhc, _NT_DIMS, preferred_element_type=jnp.float32)
            tops.append([_top16_rows(s[:, j * VREG_LANES:(j + 1) * VREG_LANES], key_id)
                         for j in range(ROUTE_TM // VREG_LANES)])
        r0 = pl.multiple_of(h * PEER_TOPK, PEER_TOPK)
        for j in range(ROUTE_TM // VREG_LANES):
            (s0, i0), (s1, i1) = tops[0][j], tops[1][j]
            sel, g = _route_head(s0, i0, s1, i1)
            rows_ref[j, pl.ds(r0, PEER_TOPK), :] = (sel * EXPERT_ROWS + TABLE_LEAD_ROWS).astype(jnp.int32)
            g_ref[pl.ds(r0, PEER_TOPK), j * VREG_LANES:(j + 1) * VREG_LANES] = g
        return carry

    lax.fori_loop(0, PEER_HEADS, head, 0)


def peer_route(x, w_query, sub_keys):
    T = x.shape[0]
    assert PEER_TB == VREG_LANES
    wq = w_query.astype(jnp.bfloat16)
    sk = sub_keys.reshape(PEER_HEADS * 2, N_KEYS, PEER_HALF).astype(jnp.bfloat16)
    out_blk = pl.BlockSpec((PEER_SEL, ROUTE_TM), lambda i: (0, i))
    rows, g = pl.pallas_call(
        _route_kernel,
        grid=(T // ROUTE_TM,),
        in_specs=[
            pl.BlockSpec((ROUTE_TM, D_MODEL), lambda i: (i, 0)),
            pl.BlockSpec(wq.shape, lambda i: (0, 0)),
            pl.BlockSpec(sk.shape, lambda i: (0, 0, 0)),
        ],
        out_specs=[pl.BlockSpec((ROUTE_TM // PEER_TB, PEER_SEL, PEER_TB), lambda i: (i, 0, 0)), out_blk],
        out_shape=[jax.ShapeDtypeStruct((T // PEER_TB, PEER_SEL, PEER_TB), jnp.int32),
                   jax.ShapeDtypeStruct((PEER_SEL, T), jnp.float32)],
        scratch_shapes=[pltpu.VMEM((ROUTE_TM, PEER_HEADS * PEER_QDIM), jnp.bfloat16)],
        compiler_params=pltpu.CompilerParams(dimension_semantics=("arbitrary",),
                                             vmem_limit_bytes=VMEM_LIMIT_BYTES),
        name="peer_route",
    )(x, wq, sk)
    return rows, g.T


def encoder_layer(x, mem, rel_bias, w_in, b_in, conv_w, conv_b, attn_sink, f_w_in, f_w_mid, f_b, f_freq,
                  f_w_out, hyena_bias, w_mem_kv, w_branch, w_out, ln1_g, ln1_b, w_query, sub_keys,
                  expert_u, expert_v, ln2_g, ln2_b):
    B, S, _ = x.shape
    T = B * S
    xf = x.reshape(T, D_MODEL)
    qkv, hy, q_m, gates = in_projection(xf, w_in, b_in)
    qkv = qkv.reshape(B, S, O_HY)
    hy = hy.reshape(B, S, O_MQ - O_HY)
    q_m = q_m.reshape(B, S, MEM_WIDTH)
    M = mem.shape[1]
    kv = linear(mem.reshape(B * M, D_MODEL), w_mem_kv, jnp.zeros((2 * MEM_WIDTH,), jnp.float32), tm=B * M)
    att = window_attention_pallas(qkv, window_bias_table(rel_bias), attn_sink)
    hyo = hyena_mixer_pallas(hy, conv_w, conv_b, f_w_in, f_w_mid, f_b, f_freq, f_w_out, hyena_bias)
    mat = memory_attention_pallas(q_m, kv.reshape(B, M, 2 * MEM_WIDTH))
    x1 = merge_norm(att.reshape(T, ATT_WIDTH), hyo.reshape(T, HYENA_WIDTH), mat.reshape(T, MEM_WIDTH), gates, xf,
                    w_branch, w_out, ln1_g, ln1_b)
    rows, g = peer_route(x1, w_query, sub_keys)
    peer = peer_experts(x1, rows, g, pack_expert_table(expert_u), pack_expert_table(expert_v))
    return residual_norm(x1, peer, ln2_g, ln2_b).reshape(B, S, D_MODEL)


def kernel(x_prompt, x_sample, mem_prompt, mem_sample, rel_bias, w_in, b_in, conv_w, conv_b, attn_sink,
           f_w_in, f_w_mid, f_b, f_freq, f_w_out, hyena_bias, w_mem_kv, w_branch, w_out, ln1_g, ln1_b,
           w_query, sub_keys, expert_u, expert_v, ln2_g, ln2_b):
    nb = x_prompt.shape[0]
    x = jnp.concatenate([x_prompt, x_sample], axis=0)
    mem = jnp.concatenate([mem_prompt, mem_sample], axis=0)
    for l in range(DEPTH):
        x = encoder_layer(x, mem, rel_bias, w_in[l], b_in[l], conv_w[l], conv_b[l], attn_sink[l],
                          f_w_in[l], f_w_mid[l], f_b[l], f_freq[l], f_w_out[l], hyena_bias[l],
                          w_mem_kv[l], w_branch[l], w_out[l], ln1_g[l], ln1_b[l], w_query[l],
                          sub_keys[l], expert_u[l], expert_v[l], ln2_g[l], ln2_b[l])
    return (x[:nb], x[nb:])
```
